```python
import jax, jax.numpy as jnp
from jax import lax
import numpy as np

D_MODEL = 1024
BATCH = 2
SEQ = 8192
DEPTH = 1
DEC_BATCH = 128
DEC_SEQ = 4
PAST_LEN = 16384
PAGE_SIZE = 128

D_LRU = D_MODEL // 2
LRU_BLOCKS = 8
LRU_BLOCK = D_LRU // LRU_BLOCKS
LRU_CONV_W = 4
LRU_C = 8.0
N_Q_HEADS = 8
N_KV_HEADS = 2
GROUP = N_Q_HEADS // N_KV_HEADS
HEAD_DIM = 64
D_ATTN = N_Q_HEADS * HEAD_DIM
D_KV = N_KV_HEADS * HEAD_DIM
WINDOW = 128
BLOCK_Q = WINDOW
W_BUF = min(WINDOW, PAST_LEN)
D_MIX = D_LRU + D_ATTN
D_IN = 2 * D_LRU + D_ATTN + 2 * D_KV
SPLITS = (D_LRU, 2 * D_LRU, 2 * D_LRU + D_ATTN, 2 * D_LRU + D_ATTN + D_KV)
MEM_LEN = 256
N_MEM_HEADS = 4
MEM_HEAD_DIM = 128
D_MEM = N_MEM_HEADS * MEM_HEAD_DIM
D_FF = 3 * D_MODEL
FFN_CONV_W = 3
EPS = 1e-6
NEG_INF = -1e30

kernel_name = 'hymba_rglru_swa_sink_convffn_step'


def rmsnorm(x, g):
    xf = x.astype(jnp.float32)
    r = xf * lax.rsqrt(jnp.mean(xf * xf, axis=-1, keepdims=True) + EPS)
    return (r * g.astype(jnp.float32)).astype(x.dtype)


def alibi_slopes():
    m = 2.0 ** (-8.0 * np.arange(1, N_Q_HEADS + 1) / N_Q_HEADS)
    return jnp.asarray(m, jnp.float32).reshape(N_KV_HEADS, GROUP)


def causal_dwconv(x, prev, w, b):
    K = w.shape[0]
    T = x.shape[1]
    xp = jnp.concatenate([prev.astype(x.dtype), x], axis=1)
    y = b + sum(w[k] * xp[:, k:k + T] for k in range(K))
    return y, xp[:, T:]


def _lru_combine(e1, e2):
    a1, b1 = e1
    a2, b2 = e2
    return a1 * a2, a2 * b1 + b2


def rg_lru(x, h0, w_a, b_a, w_x, b_x, lam):
    B, T, _ = x.shape
    xf = x.astype(jnp.float32)
    xb = xf.reshape(B, T, LRU_BLOCKS, LRU_BLOCK)
    r = jax.nn.sigmoid(jnp.einsum('btnc,ncd->btnd', xb, w_a.astype(jnp.float32)).reshape(B, T, D_LRU) + b_a)
    i = jax.nn.sigmoid(jnp.einsum('btnc,ncd->btnd', xb, w_x.astype(jnp.float32)).reshape(B, T, D_LRU) + b_x)
    log_a = -LRU_C * r * jax.nn.softplus(-lam.astype(jnp.float32))
    a = jnp.exp(log_a)
    b = jnp.sqrt(-jnp.expm1(2.0 * log_a)) * (i * xf)
    b = b.at[:, 0].add(a[:, 0] * h0.astype(jnp.float32))
    _, h = lax.associative_scan(_lru_combine, (a, b), axis=1)
    return h.astype(x.dtype), h[:, -1].astype(x.dtype)


def sink_attention(q, k, v, dist, valid, slopes, sinks):
    s = jnp.einsum('...qhgd,...khd->...hgqk', q, k).astype(jnp.float32) * (HEAD_DIM ** -0.5)
    s = s - slopes[:, :, None, None] * dist.astype(jnp.float32)
    s = jnp.where(valid, s, NEG_INF)
    sink = jnp.broadcast_to(sinks.astype(jnp.float32).reshape(N_KV_HEADS, GROUP)[:, :, None, None], s.shape[:-1] + (1,))
    p = jax.nn.softmax(jnp.concatenate([s, sink], axis=-1), axis=-1)[..., :-1]
    return jnp.einsum('...hgqk,...khd->...qhgd', p.astype(v.dtype), v)


def swa_prompt(q, k, v, slopes, sinks):
    B, T = q.shape[:2]
    nb = T // BLOCK_Q
    qb = q.reshape(B, nb, BLOCK_Q, N_KV_HEADS, GROUP, HEAD_DIM)
    kb = k.reshape(B, nb, BLOCK_Q, N_KV_HEADS, HEAD_DIM)
    vb = v.reshape(B, nb, BLOCK_Q, N_KV_HEADS, HEAD_DIM)
    kk = jnp.concatenate([jnp.concatenate([jnp.zeros_like(kb[:, :1]), kb[:, :-1]], axis=1), kb], axis=2)
    vv = jnp.concatenate([jnp.concatenate([jnp.zeros_like(vb[:, :1]), vb[:, :-1]], axis=1), vb], axis=2)
    qi = jnp.arange(BLOCK_Q)[:, None]
    kj = jnp.arange(2 * BLOCK_Q)[None, :]
    dist = qi + BLOCK_Q - kj
    kpos = jnp.arange(nb)[:, None] * BLOCK_Q - BLOCK_Q + kj
    valid = (dist >= 0) & (dist < WINDOW) & (kpos[:, None, :] >= 0)
    o = sink_attention(qb, kk, vv, dist, valid[:, None, None], slopes, sinks)
    return o.reshape(B, T, D_ATTN), k[:, -W_BUF:], v[:, -W_BUF:]


def swa_sample(q, k, v, cache_k, cache_v, slopes, sinks):
    B, T = q.shape[:2]
    kk = jnp.concatenate([cache_k.astype(k.dtype), k], axis=1)
    vv = jnp.concatenate([cache_v.astype(v.dtype), v], axis=1)
    qi = jnp.arange(T)[:, None]
    kj = jnp.arange(W_BUF + T)[None, :]
    dist = qi + W_BUF - kj
    valid = (dist >= 0) & (dist < WINDOW)
    o = sink_attention(q.reshape(B, T, N_KV_HEADS, GROUP, HEAD_DIM), kk, vv, dist, valid, slopes, sinks)
    return o.reshape(B, T, D_ATTN), kk[:, -W_BUF:], vv[:, -W_BUF:]


def layer(x, mem_k, mem_v, conv_prev, h0, ffn_prev, swa_buf, p, slopes):
    B, T, _ = x.shape
    n1 = rmsnorm(x, p['g_mix'])
    xr, gate, q, k, v = jnp.split(n1 @ p['w_in'], SPLITS, axis=-1)
    xc, conv_state = causal_dwconv(xr, conv_prev, p['w_lru_conv'], p['b_lru_conv'])
    h, h_last = rg_lru(xc, h0, p['w_lru_a'], p['b_lru_a'], p['w_lru_x'], p['b_lru_x'], p['lru_lambda'])
    y_lru = h * jax.nn.gelu(gate)
    k = k.reshape(B, T, N_KV_HEADS, HEAD_DIM)
    v = v.reshape(B, T, N_KV_HEADS, HEAD_DIM)
    if swa_buf is None:
        y_att, buf_k, buf_v = swa_prompt(q, k, v, slopes, p['attn_sinks'])
    else:
        y_att, buf_k, buf_v = swa_sample(q, k, v, swa_buf[0], swa_buf[1], slopes, p['attn_sinks'])
    x = x + jnp.concatenate([y_lru, y_att], axis=-1) @ p['w_out']
    qc = (rmsnorm(x, p['g_cross']) @ p['w_mem_q']).reshape(B, T, N_MEM_HEADS, MEM_HEAD_DIM)
    s = jnp.einsum('bthd,bshd->bhts', qc, mem_k.astype(qc.dtype)).astype(jnp.float32) * (MEM_HEAD_DIM ** -0.5)
    pr = jax.nn.softmax(s, axis=-1)
    oc = jnp.einsum('bhts,bshd->bthd', pr.astype(x.dtype), mem_v.astype(x.dtype)).reshape(B, T, D_MEM)
    x = x + oc @ p['w_mem_o']
    n3 = rmsnorm(x, p['g_ffn'])
    g, ffn_state = causal_dwconv(n3 @ p['w_ffn_gate'], ffn_prev, p['w_ffn_conv'], p['b_ffn_conv'])
    x = x + (jax.nn.gelu(g) * (n3 @ p['w_ffn_up'])) @ p['w_ffn_down']
    return x, (buf_k, buf_v, conv_state, h_last, ffn_state)


def setup_inputs(seed: int = 0) -> dict:
    key = jax.random.key(seed)
    ks = iter(jax.random.split(key, 40))

    def nrm(shape, scale=1.0):
        return scale * jax.random.normal(next(ks), shape, jnp.float32)

    def gain(shape):
        return 1.0 + nrm(shape, 0.02)

    u = jax.random.uniform(next(ks), (DEPTH, D_LRU), jnp.float32, minval=0.9, maxval=0.999)
    sa = u ** (1.0 / LRU_C)
    lam = jnp.log(sa) - jnp.log1p(-sa)
    return {
        'x_prompt': nrm((BATCH, SEQ, D_MODEL)),
        'x_sample': nrm((DEC_BATCH, DEC_SEQ, D_MODEL)),
        'cache_swa_k': nrm((DEPTH, DEC_BATCH, W_BUF, N_KV_HEADS, HEAD_DIM)),
        'cache_swa_v': nrm((DEPTH, DEC_BATCH, W_BUF, N_KV_HEADS, HEAD_DIM)),
        'cache_mem_k': nrm((DEPTH, DEC_BATCH, MEM_LEN, N_MEM_HEADS, MEM_HEAD_DIM)),
        'cache_mem_v': nrm((DEPTH, DEC_BATCH, MEM_LEN, N_MEM_HEADS, MEM_HEAD_DIM)),
        'state_lru_conv': nrm((DEPTH, DEC_BATCH, LRU_CONV_W - 1, D_LRU)),
        'state_lru_h': nrm((DEPTH, DEC_BATCH, D_LRU)),
        'state_ffn_conv': nrm((DEPTH, DEC_BATCH, FFN_CONV_W - 1, D_FF)),
        'mem_prompt': nrm((BATCH, MEM_LEN, D_MODEL)),
        'g_mix': gain((DEPTH, D_MODEL)),
        'w_in': nrm((DEPTH, D_MODEL, D_IN), D_MODEL ** -0.5),
        'w_lru_conv': nrm((DEPTH, LRU_CONV_W, D_LRU), LRU_CONV_W ** -0.5),
        'b_lru_conv': nrm((DEPTH, D_LRU), 0.02),
        'w_lru_a': nrm((DEPTH, LRU_BLOCKS, LRU_BLOCK, LRU_BLOCK), LRU_BLOCK ** -0.5),
        'b_lru_a': nrm((DEPTH, D_LRU), 0.02),
        'w_lru_x': nrm((DEPTH, LRU_BLOCKS, LRU_BLOCK, LRU_BLOCK), LRU_BLOCK ** -0.5),
        'b_lru_x': nrm((DEPTH, D_LRU), 0.02),
        'lru_lambda': lam,
        'attn_sinks': nrm((DEPTH, N_Q_HEADS)),
        'w_out': nrm((DEPTH, D_MIX, D_MODEL), D_MIX ** -0.5),
        'g_cross': gain((DEPTH, D_MODEL)),
        'g_mem': gain((DEPTH, D_MODEL)),
        'w_mem_q': nrm((DEPTH, D_MODEL, D_MEM), D_MODEL ** -0.5),
        'w_mem_k': nrm((DEPTH, D_MODEL, D_MEM), D_MODEL ** -0.5),
        'w_mem_v': nrm((DEPTH, D_MODEL, D_MEM), D_MODEL ** -0.5),
        'w_mem_o': nrm((DEPTH, D_MEM, D_MODEL), D_MEM ** -0.5),
        'g_ffn': gain((DEPTH, D_MODEL)),
        'w_ffn_gate': nrm((DEPTH, D_MODEL, D_FF), D_MODEL ** -0.5),
        'w_ffn_up': nrm((DEPTH, D_MODEL, D_FF), D_MODEL ** -0.5),
        'w_ffn_conv': nrm((DEPTH, FFN_CONV_W, D_FF), FFN_CONV_W ** -0.5),
        'b_ffn_conv': nrm((DEPTH, D_FF), 0.02),
        'w_ffn_down': nrm((DEPTH, D_FF, D_MODEL), D_FF ** -0.5),
        'g_final': gain((D_MODEL,)),
    }


def reference(x_prompt, x_sample, cache_swa_k, cache_swa_v, cache_mem_k, cache_mem_v, state_lru_conv, state_lru_h, state_ffn_conv, mem_prompt, g_mix, w_in, w_lru_conv, b_lru_conv, w_lru_a, b_lru_a, w_lru_x, b_lru_x, lru_lambda, attn_sinks, w_out, g_cross, g_mem, w_mem_q, w_mem_k, w_mem_v, w_mem_o, g_ffn, w_ffn_gate, w_ffn_up, w_ffn_conv, b_ffn_conv, w_ffn_down, g_final):
    slopes = alibi_slopes()
    xp, xs = x_prompt, x_sample
    B = x_prompt.shape[0]
    p_states, s_states = [], []
    for l in range(DEPTH):
        p = dict(g_mix=g_mix[l], w_in=w_in[l], w_lru_conv=w_lru_conv[l], b_lru_conv=b_lru_conv[l],
                 w_lru_a=w_lru_a[l], b_lru_a=b_lru_a[l], w_lru_x=w_lru_x[l], b_lru_x=b_lru_x[l],
                 lru_lambda=lru_lambda[l], attn_sinks=attn_sinks[l], w_out=w_out[l], g_cross=g_cross[l],
                 w_mem_q=w_mem_q[l], w_mem_o=w_mem_o[l], g_ffn=g_ffn[l], w_ffn_gate=w_ffn_gate[l],
                 w_ffn_up=w_ffn_up[l], w_ffn_conv=w_ffn_conv[l], b_ffn_conv=b_ffn_conv[l], w_ffn_down=w_ffn_down[l])
        mem_n = rmsnorm(mem_prompt, g_mem[l])
        mk = (mem_n @ w_mem_k[l]).reshape(B, MEM_LEN, N_MEM_HEADS, MEM_HEAD_DIM)
        mv = (mem_n @ w_mem_v[l]).reshape(B, MEM_LEN, N_MEM_HEADS, MEM_HEAD_DIM)
        xp, (pk, pv, pconv, ph, pffn) = layer(
            xp, mk, mv,
            jnp.zeros((B, LRU_CONV_W - 1, D_LRU), xp.dtype),
            jnp.zeros((B, D_LRU), xp.dtype),
            jnp.zeros((B, FFN_CONV_W - 1, D_FF), xp.dtype),
            None, p, slopes)
        p_states.append((pk, pv, mk, mv, pconv, ph, pffn))
        xs, (sk, sv, sconv, sh, sffn) = layer(
            xs, cache_mem_k[l], cache_mem_v[l], state_lru_conv[l], state_lru_h[l], state_ffn_conv[l],
            (cache_swa_k[l], cache_swa_v[l]), p, slopes)
        s_states.append((sk, sv, sconv, sh, sffn))
    p_swa_k, p_swa_v, p_mem_k, p_mem_v, p_lru_conv, p_lru_h, p_ffn_conv = [jnp.stack(a, axis=0) for a in zip(*p_states)]
    s_swa_k, s_swa_v, s_lru_conv, s_lru_h, s_ffn_conv = [jnp.stack(a, axis=0) for a in zip(*s_states)]
    y_prompt = rmsnorm(xp, g_final)
    y_sample = rmsnorm(xs, g_final)
    return (y_prompt, y_sample, p_swa_k, p_swa_v, p_mem_k, p_mem_v, p_lru_conv, p_lru_h, p_ffn_conv, s_swa_k, s_swa_v, s_lru_conv, s_lru_h, s_ffn_conv)
```

```python
import functools

import numpy as np
import jax
import jax.numpy as jnp
from jax import lax
from jax.experimental import pallas as pl
from jax.experimental.pallas import tpu as pltpu

D_MODEL = 1024
D_LRU = 512
LRU_BLOCKS = 8
LRU_BLOCK = 64
LRU_CONV_W = 4
LRU_C = 8.0
N_Q_HEADS = 8
N_KV_HEADS = 2
HEAD_DIM = 64
D_ATTN = 512
D_KV = 128
WINDOW = 128
D_IN = 1792
MEM_LEN = 256
N_MEM_HEADS = 4
MEM_HEAD_DIM = 128
D_MEM = 512
D_FF = 3072
FFN_CONV_W = 3
EPS = 1e-6
NEG_INF = -1e30

F32 = jnp.float32
BF16 = jnp.bfloat16

SLOPES = [float(2.0 ** (-8.0 * (i + 1) / N_Q_HEADS)) for i in range(N_Q_HEADS)]
ATTN_SCALE = HEAD_DIM ** -0.5
MEM_SCALE = MEM_HEAD_DIM ** -0.5

TM = 512
FF_CHUNK = 512
S_NB = 16
S_ROWS = 4 * S_NB
C_NB = 8
C_ROWS = 4 * C_NB
VMEM_LIMIT = 56 * 1024 * 1024


def _dot(a, b):
    return jnp.dot(a, b, preferred_element_type=F32)


def _dot_nt(a, b):
    return lax.dot_general(a, b, (((1,), (1,)), ((), ())), preferred_element_type=F32)


def _rmsnorm(x, g):
    ms = jnp.mean(x * x, axis=-1, keepdims=True)
    return x * lax.rsqrt(ms + EPS) * g


def _gelu(x):
    c = 0.7978845608028654
    return x * (0.5 * (1.0 + jnp.tanh(c * (x + 0.044715 * (x * x * x)))))


def _sigmoid(x):
    return 1.0 / (1.0 + jnp.exp(-x))


def _softplus(x):
    return jnp.maximum(x, 0.0) + jnp.log1p(jnp.exp(-jnp.abs(x)))


def _lru_gates(xc, wax_ref, ba, bx, lam):
    xcb = xc.astype(BF16)
    pa, px = [], []
    for gi in range(2):
        pre = _dot(xcb[:, gi * 256:(gi + 1) * 256], wax_ref[gi])
        pa.append(pre[:, :256])
        px.append(pre[:, 256:])
    r = _sigmoid(jnp.concatenate(pa, axis=1) + ba)
    i = _sigmoid(jnp.concatenate(px, axis=1) + bx)
    log_a = (-LRU_C * _softplus(-lam)) * r
    a = jnp.exp(log_a)
    b = jnp.sqrt(-jnp.tanh(log_a) * (a * a + 1.0)) * (i * xc)
    return a, b


def _head_variants(t):
    lo = lax.broadcasted_iota(jnp.int32, t.shape, 1) < HEAD_DIM
    tr = pltpu.roll(t, HEAD_DIM, 1)
    z = jnp.zeros_like(t)
    x0 = jnp.where(lo, t, z)
    y0 = jnp.where(lo, z, tr)
    x1 = jnp.where(lo, tr, z)
    y1 = jnp.where(lo, z, t)
    return [v.astype(BF16) for v in (x0, y0, x1, y1)]


def _mem_kv_kernel(mem_ref, g_ref, wk_ref, wv_ref, mk_ref, mv_ref, mkb_ref, mvb_ref):
    n = _rmsnorm(mem_ref[0], g_ref[...]).astype(BF16)
    mk = _dot(n, wk_ref[...])
    mv = _dot(n, wv_ref[...])
    mk_ref[0] = mk
    mv_ref[0] = mv
    mkb_ref[0] = mk.astype(BF16)
    mvb_ref[0] = mv.astype(BF16)


def _prompt_mixer_kernel(sinks_ref, x_ref, g_ref, win_ref, wconv_ref, bconv_ref, wax_ref, ba_ref, bx_ref,
                         lam_ref, wout_ref,
                         x1_ref, pk_ref, pv_ref, pconv_ref, ph_ref,
                         proj_s, xr_s, a_s, b_s, h_s, hc_s, kv_s, ymix_s):
    t = pl.program_id(1)

    @pl.when(t == 0)
    def _():
        xr_s[0:8, :] = jnp.zeros((8, D_LRU), F32)
        hc_s[...] = jnp.zeros((8, D_LRU), F32)
        kv_s[:, 0:WINDOW, :] = jnp.zeros((8, WINDOW, D_KV), BF16)

    x = x_ref[0]
    n1 = _rmsnorm(x, g_ref[...]).astype(BF16)
    proj_s[...] = _dot(n1, win_ref[...])

    xr = proj_s[:, 0:D_LRU]
    xr_s[8:TM + 8, :] = xr
    wc = wconv_ref[...]
    xc = (bconv_ref[...] + wc[0:1] * xr_s[5:5 + TM, :] + wc[1:2] * xr_s[6:6 + TM, :]
          + wc[2:3] * xr_s[7:7 + TM, :] + wc[3:4] * xr)
    tail = xr_s[TM:TM + 8, :]
    pconv_ref[0] = tail
    xr_s[0:8, :] = tail

    a, b = _lru_gates(xc, wax_ref, ba_ref[...], bx_ref[...], lam_ref[...])
    a_s[...] = a
    b_s[...] = b

    row8 = lax.broadcasted_iota(jnp.int32, (8, D_LRU), 0)

    def scan_body(g, h):
        r0 = pl.multiple_of(g * 8, 8)
        av = a_s[pl.ds(r0, 8), :]
        bv = b_s[pl.ds(r0, 8), :]
        for s in (1, 2, 4):
            m = row8 >= s
            a_sh = pltpu.roll(av, s, 0)
            b_sh = pltpu.roll(bv, s, 0)
            bv = jnp.where(m, av * b_sh + bv, bv)
            av = jnp.where(m, av * a_sh, av)
        hg = av * h + bv
        h_s[pl.ds(r0, 8), :] = hg
        return jnp.broadcast_to(hg[7:8, :], (8, D_LRU))

    h_last = lax.fori_loop(0, TM // 8, scan_body, hc_s[...], unroll=4)
    hc_s[...] = h_last
    ph_ref[0] = h_last

    gate = proj_s[:, D_LRU:2 * D_LRU]
    ymix_s[:, 0:D_LRU] = (h_s[...] * _gelu(gate)).astype(BF16)

    k = proj_s[:, 1536:1664]
    v = proj_s[:, 1664:1792]
    pk_ref[0] = k[TM - WINDOW:, :]
    pv_ref[0] = v[TM - WINDOW:, :]
    for i, arr in enumerate(_head_variants(k) + _head_variants(v)):
        kv_s[i, WINDOW:WINDOW + TM, :] = arr

    qi = lax.broadcasted_iota(jnp.int32, (WINDOW, 2 * WINDOW), 0)
    kj = lax.broadcasted_iota(jnp.int32, (WINDOW, 2 * WINDOW), 1)
    dist = qi + WINDOW - kj
    valid = (dist >= 0) & (dist < WINDOW)
    valid0 = valid & (kj >= jnp.where(t == 0, WINDOW, 0))
    distf = dist.astype(F32)
    bias = [jnp.where(valid, -SLOPES[h] * distf, NEG_INF) for h in range(N_Q_HEADS)]
    bias0 = [jnp.where(valid0, -SLOPES[h] * distf, NEG_INF) for h in range(N_Q_HEADS)]
    lane_lo = lax.broadcasted_iota(jnp.int32, (WINDOW, 2 * HEAD_DIM), 1) < HEAD_DIM

    for j in range(TM // WINDOW):
        rows = slice(j * WINDOW, (j + 1) * WINDOW)
        win = slice(j * WINDOW, (j + 2) * WINDOW)
        for c in range(4):
            hk = c // 2
            qc = (proj_s[rows, 1024 + c * 128:1024 + (c + 1) * 128] * ATTN_SCALE).astype(BF16)
            kcat = jnp.concatenate([kv_s[2 * hk, win, :], kv_s[2 * hk + 1, win, :]], axis=0)
            vcat = jnp.concatenate([kv_s[4 + 2 * hk, win, :], kv_s[5 + 2 * hk, win, :]], axis=0)
            s = _dot_nt(qc, kcat)
            ps, invs = [], []
            for half in range(2):
                h = 2 * c + half
                sink = sinks_ref[h]
                sh = s[:, half * 256:(half + 1) * 256] + (bias0[h] if j == 0 else bias[h])
                m = jnp.maximum(jnp.max(sh, axis=-1, keepdims=True), sink)
                p = jnp.exp(sh - m)
                l = jnp.sum(p, axis=-1, keepdims=True) + jnp.exp(sink - m)
                ps.append(p.astype(BF16))
                invs.append(1.0 / l)
            o = _dot(jnp.concatenate(ps, axis=1), vcat)
            o = o * jnp.where(lane_lo, invs[0], invs[1])
            ymix_s[rows, D_LRU + c * 128:D_LRU + (c + 1) * 128] = o.astype(BF16)

    kv_s[:, 0:WINDOW, :] = kv_s[:, TM:TM + WINDOW, :]

    x1_ref[0] = x + _dot(ymix_s[...], wout_ref[...])


def _ffn_chunks(n3, acc, gprev_fn, wg_ref, wu_ref, wfc_ref, bfc_ref, wd_ref, on_gate):
    for c in range(D_FF // FF_CHUNK):
        cs = slice(c * FF_CHUNK, (c + 1) * FF_CHUNK)
        g = _dot(n3, wg_ref[:, cs])
        g2, g1 = gprev_fn(c, g)
        on_gate(c, g)
        wfc = wfc_ref[:, cs]
        conv = bfc_ref[:, cs] + wfc[0:1] * g2 + wfc[1:2] * g1 + wfc[2:3] * g
        u = _dot(n3, wu_ref[:, cs])
        hmid = (_gelu(conv) * u).astype(BF16)
        acc = acc + _dot(hmid, wd_ref[cs, :])
    return acc


def _prompt_ffn_kernel(x1_ref, mk_ref, mv_ref, gc_ref, wq_ref, wo_ref, gf_ref, wg_ref, wu_ref, wfc_ref, bfc_ref,
                       wd_ref, gfin_ref,
                       y_ref, pffn_ref,
                       oc_s, gbuf_s, gcar_s):
    t = pl.program_id(1)

    @pl.when(t == 0)
    def _():
        gcar_s[...] = jnp.zeros((8, D_FF), F32)

    x1 = x1_ref[0]
    qc = _dot(_rmsnorm(x1, gc_ref[...]).astype(BF16), wq_ref[...]).astype(BF16)
    for h in range(N_MEM_HEADS):
        hs = slice(h * MEM_HEAD_DIM, (h + 1) * MEM_HEAD_DIM)
        s = _dot_nt(qc[:, hs], mk_ref[0, :, hs]) * MEM_SCALE
        m = jnp.max(s, axis=-1, keepdims=True)
        p = jnp.exp(s - m)
        l = jnp.sum(p, axis=-1, keepdims=True)
        o = _dot(p.astype(BF16), mv_ref[0, :, hs]) * (1.0 / l)
        oc_s[:, hs] = o.astype(BF16)
    x2 = x1 + _dot(oc_s[...], wo_ref[...])
    n3 = _rmsnorm(x2, gf_ref[...]).astype(BF16)

    def gprev(c, g):
        cs = slice(c * FF_CHUNK, (c + 1) * FF_CHUNK)
        buf = gbuf_s.at[c % 2]
        buf[0:8, :] = gcar_s[:, cs]
        buf[8:TM + 8, :] = g
        return buf[6:6 + TM, :], buf[7:7 + TM, :]

    def on_gate(c, g):
        cs = slice(c * FF_CHUNK, (c + 1) * FF_CHUNK)
        tail = g[TM - 8:, :]
        gcar_s[:, cs] = tail
        pffn_ref[0, :, cs] = tail

    acc = _ffn_chunks(n3, x2, gprev, wg_ref, wu_ref, wfc_ref, bfc_ref, wd_ref, on_gate)
    y_ref[0] = _rmsnorm(acc, gfin_ref[...])


def _sample_mixer_kernel(sinks_ref, x_ref, prev4_ref, h0_ref, ck_ref, cv_ref, g_ref, win_ref, wconv_ref, bconv_ref,
                         wax_ref, ba_ref, bx_ref, lam_ref, wout_ref,
                         x1_ref, sk_ref, sv_ref, xr_ref, h_ref,
                         q_s, k_s, v_s, kall_s, vall_s, yatt_s):
    R = S_ROWS

    @pl.when(pl.program_id(0) == 0)
    def _():
        kall_s[...] = jnp.zeros(kall_s.shape, F32)
        vall_s[...] = jnp.zeros(vall_s.shape, F32)

    x = x_ref[...]
    n1 = _rmsnorm(x, g_ref[...]).astype(BF16)
    proj = _dot(n1, win_ref[...])
    xr = proj[:, 0:D_LRU]
    gate = proj[:, D_LRU:2 * D_LRU]
    q_s[...] = proj[:, 1024:1536] * ATTN_SCALE
    k_s[...] = proj[:, 1536:1664]
    v_s[...] = proj[:, 1664:1792]
    xr_ref[...] = xr

    tmod = lax.broadcasted_iota(jnp.int32, (R, D_LRU), 0) & 3
    prev4 = prev4_ref[...]
    xs1 = jnp.where(tmod >= 1, pltpu.roll(xr, 1, 0), pltpu.roll(prev4, R - 2, 0))
    xs2 = jnp.where(tmod >= 2, pltpu.roll(xr, 2, 0), pltpu.roll(prev4, R - 1, 0))
    xs3 = jnp.where(tmod >= 3, pltpu.roll(xr, 3, 0), prev4)
    wc = wconv_ref[...]
    xc = bconv_ref[...] + wc[0:1] * xs3 + wc[1:2] * xs2 + wc[2:3] * xs1 + wc[3:4] * xr

    a, b = _lru_gates(xc, wax_ref, ba_ref[...], bx_ref[...], lam_ref[...])
    for s in (1, 2):
        m = tmod >= s
        a_sh = pltpu.roll(a, s, 0)
        b_sh = pltpu.roll(b, s, 0)
        b = jnp.where(m, a * b_sh + b, b)
        a = jnp.where(m, a * a_sh, a)
    h = a * h0_ref[...] + b
    h_ref[...] = h
    y_lru = h * _gelu(gate)

    r16 = lax.broadcasted_iota(jnp.int32, (16, 512), 0)
    c16 = lax.broadcasted_iota(jnp.int32, (16, 512), 1)
    dist = (r16 & 3) + WINDOW - (c16 & 255)
    valid = (dist >= 0) & (dist < WINDOW)
    distf = dist.astype(F32)
    first8 = r16 < 8
    lowhalf = c16 < 256
    biases, sinkcols = [], []
    for hk in range(N_KV_HEADS):
        hd = [[2 * (2 * hk + c) + half for half in range(2)] for c in range(2)]
        slope = jnp.where(first8,
                          jnp.where(lowhalf, SLOPES[hd[0][0]], SLOPES[hd[0][1]]),
                          jnp.where(lowhalf, SLOPES[hd[1][0]], SLOPES[hd[1][1]]))
        biases.append(jnp.where(valid, -slope * distf, NEG_INF))
        sinkcols.append([jnp.where(first8[:, 0:1], sinks_ref[hd[0][half]], sinks_ref[hd[1][half]])
                         for half in range(2)])
    bat0_16 = (r16 & 7) < 4
    r8 = lax.broadcasted_iota(jnp.int32, (8, D_KV), 0)
    lane_lo = lax.broadcasted_iota(jnp.int32, (16, D_KV), 1) < HEAD_DIM

    def pair_body(p, carry):
        r0 = pl.multiple_of(p * 8, 8)
        q8 = q_s[pl.ds(r0, 8), :]
        k8 = k_s[pl.ds(r0, 8), :]
        v8 = v_s[pl.ds(r0, 8), :]
        kvar, vvar = [], []
        for bb in range(2):
            bidx = 2 * p + bb
            if bb == 0:
                kn, vn = k8, v8
            else:
                kn, vn = pltpu.roll(k8, 4, 0), pltpu.roll(v8, 4, 0)
            kall_s[bb, 0:WINDOW, :] = ck_ref[bidx]
            vall_s[bb, 0:WINDOW, :] = cv_ref[bidx]
            kall_s[bb, WINDOW:WINDOW + 8, :] = jnp.where(r8 < 4, kn, 0.0)
            vall_s[bb, WINDOW:WINDOW + 8, :] = jnp.where(r8 < 4, vn, 0.0)
            sk_ref[bidx] = kall_s[bb, pl.ds(4, WINDOW), :]
            sv_ref[bidx] = vall_s[bb, pl.ds(4, WINDOW), :]
            kvar.append(_head_variants(kall_s[bb]))
            vvar.append(_head_variants(vall_s[bb]))
        for hk in range(N_KV_HEADS):
            c0 = 2 * hk
            lhs = jnp.concatenate([q8[:, c0 * 128:(c0 + 1) * 128], q8[:, (c0 + 1) * 128:(c0 + 2) * 128]],
                                  axis=0).astype(BF16)
            sb = [_dot_nt(lhs, jnp.concatenate([kvar[bb][2 * hk], kvar[bb][2 * hk + 1]], axis=0))
                  for bb in range(2)]
            s = jnp.where(bat0_16, sb[0], sb[1]) + biases[hk]
            ps, invs = [], []
            for half in range(2):
                sh = s[:, half * 256:(half + 1) * 256]
                sink = sinkcols[hk][half]
                m = jnp.maximum(jnp.max(sh, axis=-1, keepdims=True), sink)
                pe = jnp.exp(sh - m)
                l = jnp.sum(pe, axis=-1, keepdims=True) + jnp.exp(sink - m)
                ps.append(pe.astype(BF16))
                invs.append(1.0 / l)
            pcat = jnp.concatenate(ps, axis=1)
            ob = [_dot(pcat, jnp.concatenate([vvar[bb][2 * hk], vvar[bb][2 * hk + 1]], axis=0))
                  for bb in range(2)]
            o = jnp.where(bat0_16[:, 0:D_KV], ob[0], ob[1]) * jnp.where(lane_lo, invs[0], invs[1])
            yatt_s[pl.ds(r0, 8), c0 * 128:(c0 + 1) * 128] = o[0:8]
            yatt_s[pl.ds(r0, 8), (c0 + 1) * 128:(c0 + 2) * 128] = o[8:16]
        return carry

    lax.fori_loop(0, S_NB // 2, pair_body, 0)

    ymix = jnp.concatenate([y_lru, yatt_s[...]], axis=1).astype(BF16)
    x1_ref[...] = x + _dot(ymix, wout_ref[...])


def _sample_cross_kernel(x1_ref, mk_ref, mv_ref, gc_ref, wq_ref, wo_ref, x2_ref, q_s, oc_s):
    x1 = x1_ref[...]
    q_s[...] = _dot(_rmsnorm(x1, gc_ref[...]).astype(BF16), wq_ref[...])

    lane_head = lax.broadcasted_iota(jnp.int32, (8, D_MEM), 1) // MEM_HEAD_DIM
    bat0 = (lax.broadcasted_iota(jnp.int32, (32, D_MEM), 0) & 7) < 4
    bat0_s = (lax.broadcasted_iota(jnp.int32, (32, MEM_LEN), 0) & 7) < 4

    for p in range(C_NB // 2):
        q8 = q_s[p * 8:(p + 1) * 8, :]
        lhs = jnp.concatenate([jnp.where(lane_head == h, q8, 0.0) for h in range(N_MEM_HEADS)],
                              axis=0).astype(BF16)
        sb = [_dot_nt(lhs, mk_ref[2 * p + bb].astype(BF16)) for bb in range(2)]
        s = jnp.where(bat0_s, sb[0], sb[1]) * MEM_SCALE
        m = jnp.max(s, axis=-1, keepdims=True)
        pe = jnp.exp(s - m)
        l = jnp.sum(pe, axis=-1, keepdims=True)
        pb = pe.astype(BF16)
        ob = [_dot(pb, mv_ref[2 * p + bb].astype(BF16)) for bb in range(2)]
        o = jnp.where(bat0, ob[0], ob[1]) * (1.0 / l)
        out = jnp.where(lane_head == 0, o[0:8], 0.0)
        for h in range(1, N_MEM_HEADS):
            out = out + jnp.where(lane_head == h, o[h * 8:(h + 1) * 8], 0.0)
        oc_s[p * 8:(p + 1) * 8, :] = out

    x2_ref[...] = x1 + _dot(oc_s[...].astype(BF16), wo_ref[...])


def _sample_ffn_kernel(x2_ref, prev4_ref, gf_ref, wg_ref, wu_ref, wfc_ref, bfc_ref, wd_ref, gfin_ref,
                       y_ref, gate_ref):
    R = x2_ref.shape[0]
    x2 = x2_ref[...]
    n3 = _rmsnorm(x2, gf_ref[...]).astype(BF16)
    tmod = lax.broadcasted_iota(jnp.int32, (R, FF_CHUNK), 0) & 3

    def gprev(c, g):
        cs = slice(c * FF_CHUNK, (c + 1) * FF_CHUNK)
        prev4 = prev4_ref[:, cs]
        g1 = jnp.where(tmod >= 1, pltpu.roll(g, 1, 0), pltpu.roll(prev4, R - 1, 0))
        g2 = jnp.where(tmod >= 2, pltpu.roll(g, 2, 0), prev4)
        return g2, g1

    def on_gate(c, g):
        gate_ref[:, c * FF_CHUNK:(c + 1) * FF_CHUNK] = g

    acc = _ffn_chunks(n3, x2, gprev, wg_ref, wu_ref, wfc_ref, bfc_ref, wd_ref, on_gate)
    y_ref[...] = _rmsnorm(acc, gfin_ref[...])


def _const_spec(shape, grid_rank):
    zeros = (0,) * len(shape)
    if grid_rank == 1:
        return pl.BlockSpec(shape, lambda i: zeros, pipeline_mode=pl.Buffered(1))
    return pl.BlockSpec(shape, lambda i, j: zeros, pipeline_mode=pl.Buffered(1))


def _block_diag4(w):
    eye = jnp.eye(4, dtype=w.dtype)
    return (w[:, :, None, :] * eye[:, None, :, None]).reshape(4 * LRU_BLOCK, 4 * LRU_BLOCK)


def _params(sem):
    return pltpu.CompilerParams(dimension_semantics=sem, vmem_limit_bytes=VMEM_LIMIT)


def kernel(x_prompt, x_sample, cache_swa_k, cache_swa_v, cache_mem_k, cache_mem_v, state_lru_conv, state_lru_h, state_ffn_conv, mem_prompt, g_mix, w_in, w_lru_conv, b_lru_conv, w_lru_a, b_lru_a, w_lru_x, b_lru_x, lru_lambda, attn_sinks, w_out, g_cross, g_mem, w_mem_q, w_mem_k, w_mem_v, w_mem_o, g_ffn, w_ffn_gate, w_ffn_up, w_ffn_conv, b_ffn_conv, w_ffn_down, g_final):
    B, T, _ = x_prompt.shape
    NB = x_sample.shape[0]
    NT = T // TM
    SR = NB * 4

    win = w_in[0].astype(BF16)
    wout = w_out[0].astype(BF16)
    wq = w_mem_q[0].astype(BF16)
    wk = w_mem_k[0].astype(BF16)
    wv = w_mem_v[0].astype(BF16)
    wo = w_mem_o[0].astype(BF16)
    wg = w_ffn_gate[0].astype(BF16)
    wu = w_ffn_up[0].astype(BF16)
    wd = w_ffn_down[0].astype(BF16)
    wax = jnp.stack([
        jnp.concatenate([_block_diag4(w_lru_a[0, 4 * gi:4 * gi + 4]), _block_diag4(w_lru_x[0, 4 * gi:4 * gi + 4])],
                        axis=1) for gi in range(2)]).astype(BF16)
    gmix, gcross, gmem, gffn = g_mix, g_cross, g_mem, g_ffn
    gfin = g_final.reshape(1, D_MODEL)
    wconv, bconv = w_lru_conv[0], b_lru_conv
    ba, bx, lam = b_lru_a, b_lru_x, lru_lambda
    wfc, bfc = w_ffn_conv[0], b_ffn_conv
    sinks = attn_sinks[0]
    smem = pl.BlockSpec(memory_space=pltpu.SMEM)

    mk, mv, mkb, mvb = pl.pallas_call(
        _mem_kv_kernel,
        grid=(B,),
        in_specs=[pl.BlockSpec((1, MEM_LEN, D_MODEL), lambda b: (b, 0, 0)),
                  _const_spec((1, D_MODEL), 1), _const_spec((D_MODEL, D_MEM), 1), _const_spec((D_MODEL, D_MEM), 1)],
        out_specs=[pl.BlockSpec((1, MEM_LEN, D_MEM), lambda b: (b, 0, 0))] * 4,
        out_shape=[jax.ShapeDtypeStruct((B, MEM_LEN, D_MEM), F32)] * 2
        + [jax.ShapeDtypeStruct((B, MEM_LEN, D_MEM), BF16)] * 2,
        compiler_params=_params(("arbitrary",)),
        name="mem_kv",
    )(mem_prompt, gmem, wk, wv)

    x1p, pk, pv, pconv8, ph8 = pl.pallas_call(
        _prompt_mixer_kernel,
        grid=(B, NT),
        in_specs=[smem,
                  pl.BlockSpec((1, TM, D_MODEL), lambda b, t: (b, t, 0)),
                  _const_spec((1, D_MODEL), 2), _const_spec((D_MODEL, D_IN), 2),
                  _const_spec((LRU_CONV_W, D_LRU), 2), _const_spec((1, D_LRU), 2),
                  _const_spec((2, 256, 512), 2), _const_spec((1, D_LRU), 2), _const_spec((1, D_LRU), 2),
                  _const_spec((1, D_LRU), 2), _const_spec((D_MODEL, D_MODEL), 2)],
        out_specs=[pl.BlockSpec((1, TM, D_MODEL), lambda b, t: (b, t, 0)),
                   pl.BlockSpec((1, WINDOW, D_KV), lambda b, t: (b, 0, 0)),
                   pl.BlockSpec((1, WINDOW, D_KV), lambda b, t: (b, 0, 0)),
                   pl.BlockSpec((1, 8, D_LRU), lambda b, t: (b, 0, 0)),
                   pl.BlockSpec((1, 8, D_LRU), lambda b, t: (b, 0, 0))],
        out_shape=[jax.ShapeDtypeStruct((B, T, D_MODEL), F32),
                   jax.ShapeDtypeStruct((B, WINDOW, D_KV), F32),
                   jax.ShapeDtypeStruct((B, WINDOW, D_KV), F32),
                   jax.ShapeDtypeStruct((B, 8, D_LRU), F32),
                   jax.ShapeDtypeStruct((B, 8, D_LRU), F32)],
        scratch_shapes=[pltpu.VMEM((TM, D_IN), F32),
                        pltpu.VMEM((TM + 8, D_LRU), F32),
                        pltpu.VMEM((TM, D_LRU), F32),
                        pltpu.VMEM((TM, D_LRU), F32),
                        pltpu.VMEM((TM, D_LRU), F32),
                        pltpu.VMEM((8, D_LRU), F32),
                        pltpu.VMEM((8, TM + WINDOW, D_KV), BF16),
                        pltpu.VMEM((TM, D_MODEL), BF16)],
        compiler_params=_params(("arbitrary", "arbitrary")),
        name="prompt_mixer",
    )(sinks, x_prompt, gmix, win, wconv, bconv, wax, ba, bx, lam, wout)

    y_prompt, pffn8 = pl.pallas_call(
        _prompt_ffn_kernel,
        grid=(B, NT),
        in_specs=[pl.BlockSpec((1, TM, D_MODEL), lambda b, t: (b, t, 0)),
                  pl.BlockSpec((1, MEM_LEN, D_MEM), lambda b, t: (b, 0, 0)),
                  pl.BlockSpec((1, MEM_LEN, D_MEM), lambda b, t: (b, 0, 0)),
                  _const_spec((1, D_MODEL), 2), _const_spec((D_MODEL, D_MEM), 2), _const_spec((D_MEM, D_MODEL), 2),
                  _const_spec((1, D_MODEL), 2), _const_spec((D_MODEL, D_FF), 2), _const_spec((D_MODEL, D_FF), 2),
                  _const_spec((FFN_CONV_W, D_FF), 2), _const_spec((1, D_FF), 2), _const_spec((D_FF, D_MODEL), 2),
                  _const_spec((1, D_MODEL), 2)],
        out_specs=[pl.BlockSpec((1, TM, D_MODEL), lambda b, t: (b, t, 0)),
                   pl.BlockSpec((1, 8, D_FF), lambda b, t: (b, 0, 0))],
        out_shape=[jax.ShapeDtypeStruct((B, T, D_MODEL), F32),
                   jax.ShapeDtypeStruct((B, 8, D_FF), F32)],
        scratch_shapes=[pltpu.VMEM((TM, D_MEM), BF16),
                        pltpu.VMEM((2, TM + 8, FF_CHUNK), F32),
                        pltpu.VMEM((8, D_FF), F32)],
        compiler_params=_params(("arbitrary", "arbitrary")),
        name="prompt_ffn",
    )(x1p, mkb, mvb, gcross, wq, wo, gffn, wg, wu, wfc, bfc, wd, gfin)

    xs = x_sample.reshape(SR, D_MODEL)
    conv_prev4 = jnp.pad(state_lru_conv[0], ((0, 0), (0, 1), (0, 0))).reshape(SR, D_LRU)
    h0rep = jnp.repeat(state_lru_h[0], 4, axis=0)
    ck = cache_swa_k[0].reshape(NB, WINDOW, D_KV)
    cv = cache_swa_v[0].reshape(NB, WINDOW, D_KV)
    row_spec = lambda w: pl.BlockSpec((S_ROWS, w), lambda i: (i, 0))
    cache_spec = pl.BlockSpec((S_NB, WINDOW, D_KV), lambda i: (i, 0, 0))
    x1s, sk, sv, xr_s, h_s = pl.pallas_call(
        _sample_mixer_kernel,
        grid=(NB // S_NB,),
        in_specs=[smem, row_spec(D_MODEL), row_spec(D_LRU), row_spec(D_LRU), cache_spec, cache_spec,
                  _const_spec((1, D_MODEL), 1), _const_spec((D_MODEL, D_IN), 1),
                  _const_spec((LRU_CONV_W, D_LRU), 1), _const_spec((1, D_LRU), 1),
                  _const_spec((2, 256, 512), 1), _const_spec((1, D_LRU), 1), _const_spec((1, D_LRU), 1),
                  _const_spec((1, D_LRU), 1), _const_spec((D_MODEL, D_MODEL), 1)],
        out_specs=[row_spec(D_MODEL), cache_spec, cache_spec, row_spec(D_LRU), row_spec(D_LRU)],
        out_shape=[jax.ShapeDtypeStruct((SR, D_MODEL), F32),
                   jax.ShapeDtypeStruct((NB, WINDOW, D_KV), F32),
                   jax.ShapeDtypeStruct((NB, WINDOW, D_KV), F32),
                   jax.ShapeDtypeStruct((SR, D_LRU), F32),
                   jax.ShapeDtypeStruct((SR, D_LRU), F32)],
        scratch_shapes=[pltpu.VMEM((S_ROWS, D_ATTN), F32),
                        pltpu.VMEM((S_ROWS, D_KV), F32),
                        pltpu.VMEM((S_ROWS, D_KV), F32),
                        pltpu.VMEM((2, 2 * WINDOW, D_KV), F32),
                        pltpu.VMEM((2, 2 * WINDOW, D_KV), F32),
                        pltpu.VMEM((S_ROWS, D_ATTN), F32)],
        compiler_params=_params(("arbitrary",)),
        name="sample_mixer",
    )(sinks, xs, conv_prev4, h0rep, ck, cv, gmix, win, wconv, bconv, wax, ba, bx, lam, wout)

    cmk = cache_mem_k[0].reshape(NB, MEM_LEN, D_MEM)
    cmv = cache_mem_v[0].reshape(NB, MEM_LEN, D_MEM)
    crow = pl.BlockSpec((C_ROWS, D_MODEL), lambda i: (i, 0))
    cmem = pl.BlockSpec((C_NB, MEM_LEN, D_MEM), lambda i: (i, 0, 0))
    x2s = pl.pallas_call(
        _sample_cross_kernel,
        grid=(NB // C_NB,),
        in_specs=[crow, cmem, cmem, _const_spec((1, D_MODEL), 1), _const_spec((D_MODEL, D_MEM), 1),
                  _const_spec((D_MEM, D_MODEL), 1)],
        out_specs=crow,
        out_shape=jax.ShapeDtypeStruct((SR, D_MODEL), F32),
        scratch_shapes=[pltpu.VMEM((C_ROWS, D_MEM), F32), pltpu.VMEM((C_ROWS, D_MEM), F32)],
        compiler_params=_params(("arbitrary",)),
        name="sample_cross",
    )(x1s, cmk, cmv, gcross, wq, wo)

    ffn_prev4 = jnp.pad(state_ffn_conv[0], ((0, 0), (0, 2), (0, 0))).reshape(SR, D_FF)
    y_s, gate_s = pl.pallas_call(
        _sample_ffn_kernel,
        grid=(1,),
        in_specs=[_const_spec((SR, D_MODEL), 1), _const_spec((SR, D_FF), 1),
                  _const_spec((1, D_MODEL), 1), _const_spec((D_MODEL, D_FF), 1), _const_spec((D_MODEL, D_FF), 1),
                  _const_spec((FFN_CONV_W, D_FF), 1), _const_spec((1, D_FF), 1), _const_spec((D_FF, D_MODEL), 1),
                  _const_spec((1, D_MODEL), 1)],
        out_specs=[pl.BlockSpec((SR, D_MODEL), lambda i: (0, 0)), pl.BlockSpec((SR, D_FF), lambda i: (0, 0))],
        out_shape=[jax.ShapeDtypeStruct((SR, D_MODEL), F32), jax.ShapeDtypeStruct((SR, D_FF), F32)],
        compiler_params=_params(("arbitrary",)),
        name="sample_ffn",
    )(x2s, ffn_prev4, gffn, wg, wu, wfc, bfc, wd, gfin)

    p_swa_k = pk.reshape(1, B, WINDOW, N_KV_HEADS, HEAD_DIM)
    p_swa_v = pv.reshape(1, B, WINDOW, N_KV_HEADS, HEAD_DIM)
    p_mem_k = mk.reshape(1, B, MEM_LEN, N_MEM_HEADS, MEM_HEAD_DIM)
    p_mem_v = mv.reshape(1, B, MEM_LEN, N_MEM_HEADS, MEM_HEAD_DIM)
    p_lru_conv = pconv8[None, :, 8 - (LRU_CONV_W - 1):, :]
    p_lru_h = ph8[None, :, 0, :]
    p_ffn_conv = pffn8[None, :, 8 - (FFN_CONV_W - 1):, :]
    y_sample = y_s.reshape(NB, 4, D_MODEL)
    s_swa_k = sk.reshape(1, NB, WINDOW, N_KV_HEADS, HEAD_DIM)
    s_swa_v = sv.reshape(1, NB, WINDOW, N_KV_HEADS, HEAD_DIM)
    s_lru_conv = xr_s.reshape(NB, 4, D_LRU)[None, :, 1:, :]
    s_lru_h = h_s.reshape(NB, 4, D_LRU)[None, :, 3, :]
    s_ffn_conv = gate_s.reshape(NB, 4, D_FF)[None, :, 2:, :]
    return (y_prompt, y_sample, p_swa_k, p_swa_v, p_mem_k, p_mem_v, p_lru_conv, p_lru_h, p_ffn_conv,
            s_swa_k, s_swa_v, s_lru_conv, s_lru_h, s_ffn_conv)
```

```python
import functools

import numpy as np
import jax
import jax.numpy as jnp
from jax import lax
from jax.experimental import pallas as pl
from jax.experimental.pallas import tpu as pltpu

D_MODEL = 1024
D_LRU = 512
LRU_BLOCKS = 8
LRU_BLOCK = 64
LRU_CONV_W = 4
LRU_C = 8.0
N_Q_HEADS = 8
N_KV_HEADS = 2
HEAD_DIM = 64
D_ATTN = 512
D_KV = 128
WINDOW = 128
D_IN = 1792
MEM_LEN = 256
N_MEM_HEADS = 4
MEM_HEAD_DIM = 128
D_MEM = 512
D_FF = 3072
FFN_CONV_W = 3
EPS = 1e-6
NEG_INF = -1e30

F32 = jnp.float32
BF16 = jnp.bfloat16

SLOPES = [float(2.0 ** (-8.0 * (i + 1) / N_Q_HEADS)) for i in range(N_Q_HEADS)]
ATTN_SCALE = HEAD_DIM ** -0.5
MEM_SCALE = MEM_HEAD_DIM ** -0.5
LOG2E = 1.4426950408889634
F32_TINY = 1.1754944e-38

TM = 512
FF_CHUNK = 512
S_NB = 16
S_ROWS = 4 * S_NB
C_NB = 8
C_ROWS = 4 * C_NB
VMEM_LIMIT = 56 * 1024 * 1024


def _dot(a, b):
    return jnp.dot(a, b, preferred_element_type=F32)


def _dot_nt(a, b):
    return lax.dot_general(a, b, (((1,), (1,)), ((), ())), preferred_element_type=F32)


def _rmsnorm(x, g):
    ms = jnp.mean(x * x, axis=-1, keepdims=True)
    return x * lax.rsqrt(ms + EPS) * g


def _gelu(x):
    c = 0.7978845608028654
    return x * (0.5 * (1.0 + jnp.tanh(c * (x + 0.044715 * (x * x * x)))))


def _sigmoid(x):
    return 1.0 / (1.0 + jnp.exp(-x))


def _softplus(x):
    return jnp.maximum(x, 0.0) + jnp.log1p(jnp.exp(-jnp.abs(x)))


def _lru_gates(xc, wax_ref, ba, bx, lam):
    xcb = xc.astype(BF16)
    pa, px = [], []
    for gi in range(2):
        pre = _dot(xcb[:, gi * 256:(gi + 1) * 256], wax_ref[gi])
        pa.append(pre[:, :256])
        px.append(pre[:, 256:])
    r = _sigmoid(jnp.concatenate(pa, axis=1) + ba)
    i = _sigmoid(jnp.concatenate(px, axis=1) + bx)
    log_a = (-LRU_C * _softplus(-lam)) * r
    a = jnp.exp(log_a)
    om = -jnp.tanh(log_a) * (a * a + 1.0)
    b = (om * lax.rsqrt(jnp.maximum(om, F32_TINY))) * (i * xc)
    return a, b


def _head_variants(t):
    lo = lax.broadcasted_iota(jnp.int32, t.shape, 1) < HEAD_DIM
    tr = pltpu.roll(t, HEAD_DIM, 1)
    z = jnp.zeros_like(t)
    x0 = jnp.where(lo, t, z)
    y0 = jnp.where(lo, z, tr)
    x1 = jnp.where(lo, tr, z)
    y1 = jnp.where(lo, z, t)
    return [v.astype(BF16) for v in (x0, y0, x1, y1)]


def _mem_kv_kernel(mem_ref, g_ref, wk_ref, wv_ref, mk_ref, mv_ref, mkb_ref, mvb_ref):
    n = _rmsnorm(mem_ref[0], g_ref[...]).astype(BF16)
    mk = _dot(n, wk_ref[...])
    mv = _dot(n, wv_ref[...])
    mk_ref[0] = mk
    mv_ref[0] = mv
    mkb_ref[0] = mk.astype(BF16)
    mvb_ref[0] = mv.astype(BF16)


def _prompt_mixer_kernel(sinks_ref, x_ref, g_ref, win_ref, wconv_ref, bconv_ref, wax_ref, ba_ref, bx_ref,
                         lam_ref, wout_ref,
                         x1_ref, pk_ref, pv_ref, pconv_ref, ph_ref,
                         proj_s, xr_s, a_s, b_s, h_s, hc_s, kv_s, ymix_s):
    t = pl.program_id(1)

    @pl.when(t == 0)
    def _():
        xr_s[0:8, :] = jnp.zeros((8, D_LRU), F32)
        hc_s[...] = jnp.zeros((8, D_LRU), F32)
        kv_s[:, 0:WINDOW, :] = jnp.zeros((8, WINDOW, D_KV), BF16)

    x = x_ref[0]
    n1 = _rmsnorm(x, g_ref[...]).astype(BF16)
    proj_s[...] = _dot(n1, win_ref[...])

    xr = proj_s[:, 0:D_LRU]
    xr_s[8:TM + 8, :] = xr
    wc = wconv_ref[...]
    xc = (bconv_ref[...] + wc[0:1] * xr_s[5:5 + TM, :] + wc[1:2] * xr_s[6:6 + TM, :]
          + wc[2:3] * xr_s[7:7 + TM, :] + wc[3:4] * xr)
    tail = xr_s[TM:TM + 8, :]
    pconv_ref[0] = tail
    xr_s[0:8, :] = tail

    a, b = _lru_gates(xc, wax_ref, ba_ref[...], bx_ref[...], lam_ref[...])
    a_s[...] = a
    b_s[...] = b

    row8 = lax.broadcasted_iota(jnp.int32, (8, D_LRU), 0)

    h = hc_s[...]
    for g in range(TM // 8):
        r0 = g * 8
        av = a_s[r0:r0 + 8, :]
        bv = b_s[r0:r0 + 8, :]
        for s in (1, 2, 4):
            m = row8 >= s
            a_sh = pltpu.roll(av, s, 0)
            b_sh = pltpu.roll(bv, s, 0)
            bv = jnp.where(m, av * b_sh + bv, bv)
            av = jnp.where(m, av * a_sh, av)
        hg = av * h + bv
        h_s[r0:r0 + 8, :] = hg
        h = jnp.broadcast_to(hg[7:8, :], (8, D_LRU))
    hc_s[...] = h
    ph_ref[0] = h

    gate = proj_s[:, D_LRU:2 * D_LRU]
    ymix_s[:, 0:D_LRU] = (h_s[...] * _gelu(gate)).astype(BF16)

    k = proj_s[:, 1536:1664]
    v = proj_s[:, 1664:1792]
    pk_ref[0] = k[TM - WINDOW:, :]
    pv_ref[0] = v[TM - WINDOW:, :]
    for i, arr in enumerate(_head_variants(k) + _head_variants(v)):
        kv_s[i, WINDOW:WINDOW + TM, :] = arr

    qi = lax.broadcasted_iota(jnp.int32, (WINDOW, 2 * WINDOW), 0)
    kj = lax.broadcasted_iota(jnp.int32, (WINDOW, 2 * WINDOW), 1)
    dist = qi + WINDOW - kj
    valid = (dist >= 0) & (dist < WINDOW)
    valid0 = valid & (kj >= jnp.where(t == 0, WINDOW, 0))
    distf = dist.astype(F32)
    bias = [jnp.where(valid, -SLOPES[h] * distf, NEG_INF) for h in range(N_Q_HEADS)]
    bias0 = [jnp.where(valid0, -SLOPES[h] * distf, NEG_INF) for h in range(N_Q_HEADS)]
    lane_lo = lax.broadcasted_iota(jnp.int32, (WINDOW, 2 * HEAD_DIM), 1) < HEAD_DIM

    r512 = lax.broadcasted_iota(jnp.int32, (4 * WINDOW, 2 * HEAD_DIM), 0)
    l512 = lax.broadcasted_iota(jnp.int32, (4 * WINDOW, 2 * HEAD_DIM), 1)
    ones_cols = jnp.where((r512 < 2 * WINDOW) == (l512 < HEAD_DIM), 1.0, 0.0).astype(BF16)
    n_iter = (TM // WINDOW) * 4

    def qk(n):
        j, c = divmod(n, 4)
        hk = c // 2
        rows = slice(j * WINDOW, (j + 1) * WINDOW)
        win = slice(j * WINDOW, (j + 2) * WINDOW)
        qc = (proj_s[rows, 1024 + c * 128:1024 + (c + 1) * 128] * ATTN_SCALE).astype(BF16)
        kcat = jnp.concatenate([kv_s[2 * hk, win, :], kv_s[2 * hk + 1, win, :]], axis=0)
        return _dot_nt(qc, kcat)

    def softmax_pv(n, s):
        j, c = divmod(n, 4)
        hk = c // 2
        rows = slice(j * WINDOW, (j + 1) * WINDOW)
        win = slice(j * WINDOW, (j + 2) * WINDOW)
        vcat = jnp.concatenate([kv_s[4 + 2 * hk, win, :], kv_s[5 + 2 * hk, win, :]], axis=0)
        vaug = jnp.concatenate([vcat, ones_cols], axis=1)
        ps, es = [], []
        for half in range(2):
            h = 2 * c + half
            sink = sinks_ref[h]
            sh = s[:, half * 256:(half + 1) * 256] + (bias0[h] if j == 0 else bias[h])
            m = jnp.maximum(jnp.max(sh, axis=-1, keepdims=True), sink)
            ps.append(jnp.exp(sh - m).astype(BF16))
            es.append(jnp.exp(sink - m))
        oa = _dot(jnp.concatenate(ps, axis=1), vaug)
        den = oa[:, 128:256] + jnp.where(lane_lo, es[0], es[1])
        ymix_s[rows, D_LRU + c * 128:D_LRU + (c + 1) * 128] = (oa[:, 0:128] * (1.0 / den)).astype(BF16)

    depth = 2
    pend = [qk(n) for n in range(depth)]
    for n in range(n_iter):
        s = pend.pop(0)
        if n + depth < n_iter:
            pend.append(qk(n + depth))
        softmax_pv(n, s)

    kv_s[:, 0:WINDOW, :] = kv_s[:, TM:TM + WINDOW, :]

    x1_ref[0] = x + _dot(ymix_s[...], wout_ref[...])


def _ffn_chunks(n3, acc, gprev_fn, wg_ref, wu_ref, wfc_ref, bfc_ref, wd_ref, on_gate):
    nc = D_FF // FF_CHUNK

    def up(c):
        cs = slice(c * FF_CHUNK, (c + 1) * FF_CHUNK)
        return _dot(n3, wg_ref[:, cs]), _dot(n3, wu_ref[:, cs])

    nxt = up(0)
    for c in range(nc):
        cs = slice(c * FF_CHUNK, (c + 1) * FF_CHUNK)
        g, u = nxt
        if c + 1 < nc:
            nxt = up(c + 1)
        g2, g1 = gprev_fn(c, g)
        on_gate(c, g)
        wfc = wfc_ref[:, cs]
        conv = bfc_ref[:, cs] + wfc[0:1] * g2 + wfc[1:2] * g1 + wfc[2:3] * g
        hmid = (_gelu(conv) * u).astype(BF16)
        acc = acc + _dot(hmid, wd_ref[cs, :])
    return acc


def _prompt_ffn_kernel(x1_ref, mk_ref, mv_ref, gc_ref, wq_ref, wo_ref, gf_ref, wg_ref, wu_ref, wfc_ref, bfc_ref,
                       wd_ref, gfin_ref,
                       y_ref, pffn_ref,
                       oc_s, gbuf_s, gcar_s):
    t = pl.program_id(1)

    @pl.when(t == 0)
    def _():
        gcar_s[...] = jnp.zeros((8, D_FF), F32)

    x1 = x1_ref[0]
    qc = _dot(_rmsnorm(x1, gc_ref[...]).astype(BF16), wq_ref[...]).astype(BF16)
    hsl = [slice(h * MEM_HEAD_DIM, (h + 1) * MEM_HEAD_DIM) for h in range(N_MEM_HEADS)]
    ss = [_dot_nt(qc[:, hs], mk_ref[0, :, hs]) for hs in hsl]
    for h, hs in enumerate(hsl):
        s = ss[h]
        m = jnp.max(s, axis=-1, keepdims=True)
        p = jnp.exp2((s - m) * (MEM_SCALE * LOG2E))
        l = jnp.sum(p, axis=-1, keepdims=True)
        o = _dot(p.astype(BF16), mv_ref[0, :, hs]) * (1.0 / l)
        oc_s[:, hs] = o.astype(BF16)
    x2 = x1 + _dot(oc_s[...], wo_ref[...])
    n3 = _rmsnorm(x2, gf_ref[...]).astype(BF16)

    def gprev(c, g):
        cs = slice(c * FF_CHUNK, (c + 1) * FF_CHUNK)
        buf = gbuf_s.at[c % 2]
        buf[0:8, :] = gcar_s[:, cs]
        buf[8:TM + 8, :] = g
        return buf[6:6 + TM, :], buf[7:7 + TM, :]

    def on_gate(c, g):
        cs = slice(c * FF_CHUNK, (c + 1) * FF_CHUNK)
        tail = g[TM - 8:, :]
        gcar_s[:, cs] = tail
        pffn_ref[0, :, cs] = tail

    acc = _ffn_chunks(n3, x2, gprev, wg_ref, wu_ref, wfc_ref, bfc_ref, wd_ref, on_gate)
    y_ref[0] = _rmsnorm(acc, gfin_ref[...])


def _sample_mixer_kernel(sinks_ref, x_ref, prev4_ref, h0_ref, ck_ref, cv_ref, g_ref, win_ref, wconv_ref, bconv_ref,
                         wax_ref, ba_ref, bx_ref, lam_ref, wout_ref,
                         x1_ref, sk_ref, sv_ref, xr_ref, h_ref,
                         q_s, k_s, v_s, kall_s, vall_s, yatt_s):
    R = S_ROWS

    @pl.when(pl.program_id(0) == 0)
    def _():
        kall_s[...] = jnp.zeros(kall_s.shape, F32)
        vall_s[...] = jnp.zeros(vall_s.shape, F32)

    x = x_ref[...]
    n1 = _rmsnorm(x, g_ref[...]).astype(BF16)
    proj = _dot(n1, win_ref[...])
    xr = proj[:, 0:D_LRU]
    gate = proj[:, D_LRU:2 * D_LRU]
    q_s[...] = proj[:, 1024:1536] * ATTN_SCALE
    k_s[...] = proj[:, 1536:1664]
    v_s[...] = proj[:, 1664:1792]
    xr_ref[...] = xr

    tmod = lax.broadcasted_iota(jnp.int32, (R, D_LRU), 0) & 3
    prev4 = prev4_ref[...]
    xs1 = jnp.where(tmod >= 1, pltpu.roll(xr, 1, 0), pltpu.roll(prev4, R - 2, 0))
    xs2 = jnp.where(tmod >= 2, pltpu.roll(xr, 2, 0), pltpu.roll(prev4, R - 1, 0))
    xs3 = jnp.where(tmod >= 3, pltpu.roll(xr, 3, 0), prev4)
    wc = wconv_ref[...]
    xc = bconv_ref[...] + wc[0:1] * xs3 + wc[1:2] * xs2 + wc[2:3] * xs1 + wc[3:4] * xr

    a, b = _lru_gates(xc, wax_ref, ba_ref[...], bx_ref[...], lam_ref[...])
    for s in (1, 2):
        m = tmod >= s
        a_sh = pltpu.roll(a, s, 0)
        b_sh = pltpu.roll(b, s, 0)
        b = jnp.where(m, a * b_sh + b, b)
        a = jnp.where(m, a * a_sh, a)
    h = a * h0_ref[...] + b
    h_ref[...] = h
    y_lru = h * _gelu(gate)

    r16 = lax.broadcasted_iota(jnp.int32, (16, 512), 0)
    c16 = lax.broadcasted_iota(jnp.int32, (16, 512), 1)
    dist = (r16 & 3) + WINDOW - (c16 & 255)
    valid = (dist >= 0) & (dist < WINDOW)
    distf = dist.astype(F32)
    first8 = r16 < 8
    lowhalf = c16 < 256
    biases, sinkcols = [], []
    for hk in range(N_KV_HEADS):
        hd = [[2 * (2 * hk + c) + half for half in range(2)] for c in range(2)]
        slope = jnp.where(first8,
                          jnp.where(lowhalf, SLOPES[hd[0][0]], SLOPES[hd[0][1]]),
                          jnp.where(lowhalf, SLOPES[hd[1][0]], SLOPES[hd[1][1]]))
        biases.append(jnp.where(valid, -slope * distf, NEG_INF))
        sinkcols.append([jnp.where(first8[:, 0:1], sinks_ref[hd[0][half]], sinks_ref[hd[1][half]])
                         for half in range(2)])
    bat0_16 = (r16 & 7) < 4
    r8 = lax.broadcasted_iota(jnp.int32, (8, D_KV), 0)
    lane_lo = lax.broadcasted_iota(jnp.int32, (16, D_KV), 1) < HEAD_DIM

    def pair_body(p, carry):
        r0 = pl.multiple_of(p * 8, 8)
        q8 = q_s[pl.ds(r0, 8), :]
        k8 = k_s[pl.ds(r0, 8), :]
        v8 = v_s[pl.ds(r0, 8), :]
        kvar, vvar = [], []
        for bb in range(2):
            bidx = 2 * p + bb
            if bb == 0:
                kn, vn = k8, v8
            else:
                kn, vn = pltpu.roll(k8, 4, 0), pltpu.roll(v8, 4, 0)
            kall_s[bb, 0:WINDOW, :] = ck_ref[bidx]
            vall_s[bb, 0:WINDOW, :] = cv_ref[bidx]
            kall_s[bb, WINDOW:WINDOW + 8, :] = jnp.where(r8 < 4, kn, 0.0)
            vall_s[bb, WINDOW:WINDOW + 8, :] = jnp.where(r8 < 4, vn, 0.0)
            sk_ref[bidx] = kall_s[bb, pl.ds(4, WINDOW), :]
            sv_ref[bidx] = vall_s[bb, pl.ds(4, WINDOW), :]
            kvar.append(_head_variants(kall_s[bb]))
            vvar.append(_head_variants(vall_s[bb]))
        for hk in range(N_KV_HEADS):
            c0 = 2 * hk
            lhs = jnp.concatenate([q8[:, c0 * 128:(c0 + 1) * 128], q8[:, (c0 + 1) * 128:(c0 + 2) * 128]],
                                  axis=0).astype(BF16)
            sb = [_dot_nt(lhs, jnp.concatenate([kvar[bb][2 * hk], kvar[bb][2 * hk + 1]], axis=0))
                  for bb in range(2)]
            s = jnp.where(bat0_16, sb[0], sb[1]) + biases[hk]
            ps, invs = [], []
            for half in range(2):
                sh = s[:, half * 256:(half + 1) * 256]
                sink = sinkcols[hk][half]
                m = jnp.maximum(jnp.max(sh, axis=-1, keepdims=True), sink)
                pe = jnp.exp(sh - m)
                l = jnp.sum(pe, axis=-1, keepdims=True) + jnp.exp(sink - m)
                ps.append(pe.astype(BF16))
                invs.append(1.0 / l)
            pcat = jnp.concatenate(ps, axis=1)
            ob = [_dot(pcat, jnp.concatenate([vvar[bb][2 * hk], vvar[bb][2 * hk + 1]], axis=0))
                  for bb in range(2)]
            o = jnp.where(bat0_16[:, 0:D_KV], ob[0], ob[1]) * jnp.where(lane_lo, invs[0], invs[1])
            yatt_s[pl.ds(r0, 8), c0 * 128:(c0 + 1) * 128] = o[0:8]
            yatt_s[pl.ds(r0, 8), (c0 + 1) * 128:(c0 + 2) * 128] = o[8:16]
        return carry

    lax.fori_loop(0, S_NB // 2, pair_body, 0)

    ymix = jnp.concatenate([y_lru, yatt_s[...]], axis=1).astype(BF16)
    x1_ref[...] = x + _dot(ymix, wout_ref[...])


def _load_mem_heads(ref, b):
    return jnp.concatenate([ref[b, pl.ds(h, MEM_LEN, stride=N_MEM_HEADS), :] for h in range(N_MEM_HEADS)],
                           axis=1).astype(BF16)


def _sample_cross_kernel(x1_ref, mk_ref, mv_ref, gc_ref, wq_ref, wo_ref, x2_ref, q_s, oc_s):
    x1 = x1_ref[...]
    q_s[...] = _dot(_rmsnorm(x1, gc_ref[...]).astype(BF16), wq_ref[...])

    lane_head = lax.broadcasted_iota(jnp.int32, (8, D_MEM), 1) // MEM_HEAD_DIM
    bat0 = (lax.broadcasted_iota(jnp.int32, (32, D_MEM), 0) & 7) < 4
    bat0_s = (lax.broadcasted_iota(jnp.int32, (32, MEM_LEN), 0) & 7) < 4

    for p in range(C_NB // 2):
        q8 = q_s[p * 8:(p + 1) * 8, :]
        lhs = jnp.concatenate([jnp.where(lane_head == h, q8, 0.0) for h in range(N_MEM_HEADS)],
                              axis=0).astype(BF16)
        sb = [_dot_nt(lhs, _load_mem_heads(mk_ref, 2 * p + bb)) for bb in range(2)]
        s = jnp.where(bat0_s, sb[0], sb[1]) * MEM_SCALE
        m = jnp.max(s, axis=-1, keepdims=True)
        pe = jnp.exp(s - m)
        l = jnp.sum(pe, axis=-1, keepdims=True)
        pb = pe.astype(BF16)
        ob = [_dot(pb, _load_mem_heads(mv_ref, 2 * p + bb)) for bb in range(2)]
        o = jnp.where(bat0, ob[0], ob[1]) * (1.0 / l)
        out = jnp.where(lane_head == 0, o[0:8], 0.0)
        for h in range(1, N_MEM_HEADS):
            out = out + jnp.where(lane_head == h, o[h * 8:(h + 1) * 8], 0.0)
        oc_s[p * 8:(p + 1) * 8, :] = out

    x2_ref[...] = x1 + _dot(oc_s[...].astype(BF16), wo_ref[...])


def _sample_ffn_kernel(x2_ref, prev4_ref, gf_ref, wg_ref, wu_ref, wfc_ref, bfc_ref, wd_ref, gfin_ref,
                       y_ref, gate_ref):
    R = x2_ref.shape[0]
    x2 = x2_ref[...]
    n3 = _rmsnorm(x2, gf_ref[...]).astype(BF16)
    tmod = lax.broadcasted_iota(jnp.int32, (R, FF_CHUNK), 0) & 3

    def gprev(c, g):
        cs = slice(c * FF_CHUNK, (c + 1) * FF_CHUNK)
        prev4 = prev4_ref[:, cs]
        g1 = jnp.where(tmod >= 1, pltpu.roll(g, 1, 0), pltpu.roll(prev4, R - 1, 0))
        g2 = jnp.where(tmod >= 2, pltpu.roll(g, 2, 0), prev4)
        return g2, g1

    def on_gate(c, g):
        gate_ref[:, c * FF_CHUNK:(c + 1) * FF_CHUNK] = g

    acc = _ffn_chunks(n3, x2, gprev, wg_ref, wu_ref, wfc_ref, bfc_ref, wd_ref, on_gate)
    y_ref[...] = _rmsnorm(acc, gfin_ref[...])


def _const_spec(shape, grid_rank):
    zeros = (0,) * len(shape)
    if grid_rank == 1:
        return pl.BlockSpec(shape, lambda i: zeros, pipeline_mode=pl.Buffered(1))
    return pl.BlockSpec(shape, lambda i, j: zeros, pipeline_mode=pl.Buffered(1))


def _block_diag4(w):
    eye = jnp.eye(4, dtype=w.dtype)
    return (w[:, :, None, :] * eye[:, None, :, None]).reshape(4 * LRU_BLOCK, 4 * LRU_BLOCK)


def _params(sem):
    return pltpu.CompilerParams(dimension_semantics=sem, vmem_limit_bytes=VMEM_LIMIT)


def kernel(x_prompt, x_sample, cache_swa_k, cache_swa_v, cache_mem_k, cache_mem_v, state_lru_conv, state_lru_h, state_ffn_conv, mem_prompt, g_mix, w_in, w_lru_conv, b_lru_conv, w_lru_a, b_lru_a, w_lru_x, b_lru_x, lru_lambda, attn_sinks, w_out, g_cross, g_mem, w_mem_q, w_mem_k, w_mem_v, w_mem_o, g_ffn, w_ffn_gate, w_ffn_up, w_ffn_conv, b_ffn_conv, w_ffn_down, g_final):
    B, T, _ = x_prompt.shape
    NB = x_sample.shape[0]
    NT = T // TM
    SR = NB * 4

    win = w_in[0].astype(BF16)
    wout = w_out[0].astype(BF16)
    wq = w_mem_q[0].astype(BF16)
    wk = w_mem_k[0].astype(BF16)
    wv = w_mem_v[0].astype(BF16)
    wo = w_mem_o[0].astype(BF16)
    wg = w_ffn_gate[0].astype(BF16)
    wu = w_ffn_up[0].astype(BF16)
    wd = w_ffn_down[0].astype(BF16)
    wax = jnp.stack([
        jnp.concatenate([_block_diag4(w_lru_a[0, 4 * gi:4 * gi + 4]), _block_diag4(w_lru_x[0, 4 * gi:4 * gi + 4])],
                        axis=1) for gi in range(2)]).astype(BF16)
    gmix, gcross, gmem, gffn = g_mix, g_cross, g_mem, g_ffn
    gfin = g_final.reshape(1, D_MODEL)
    wconv, bconv = w_lru_conv[0], b_lru_conv
    ba, bx, lam = b_lru_a, b_lru_x, lru_lambda
    wfc, bfc = w_ffn_conv[0], b_ffn_conv
    sinks = attn_sinks[0]
    smem = pl.BlockSpec(memory_space=pltpu.SMEM)

    mk, mv, mkb, mvb = pl.pallas_call(
        _mem_kv_kernel,
        grid=(B,),
        in_specs=[pl.BlockSpec((1, MEM_LEN, D_MODEL), lambda b: (b, 0, 0)),
                  _const_spec((1, D_MODEL), 1), _const_spec((D_MODEL, D_MEM), 1), _const_spec((D_MODEL, D_MEM), 1)],
        out_specs=[pl.BlockSpec((1, MEM_LEN, D_MEM), lambda b: (b, 0, 0))] * 4,
        out_shape=[jax.ShapeDtypeStruct((B, MEM_LEN, D_MEM), F32)] * 2
        + [jax.ShapeDtypeStruct((B, MEM_LEN, D_MEM), BF16)] * 2,
        compiler_params=_params(("arbitrary",)),
        name="mem_kv",
    )(mem_prompt, gmem, wk, wv)

    x1p, pk, pv, pconv8, ph8 = pl.pallas_call(
        _prompt_mixer_kernel,
        grid=(B, NT),
        in_specs=[smem,
                  pl.BlockSpec((1, TM, D_MODEL), lambda b, t: (b, t, 0)),
                  _const_spec((1, D_MODEL), 2), _const_spec((D_MODEL, D_IN), 2),
                  _const_spec((LRU_CONV_W, D_LRU), 2), _const_spec((1, D_LRU), 2),
                  _const_spec((2, 256, 512), 2), _const_spec((1, D_LRU), 2), _const_spec((1, D_LRU), 2),
                  _const_spec((1, D_LRU), 2), _const_spec((D_MODEL, D_MODEL), 2)],
        out_specs=[pl.BlockSpec((1, TM, D_MODEL), lambda b, t: (b, t, 0)),
                   pl.BlockSpec((1, WINDOW, D_KV), lambda b, t: (b, 0, 0)),
                   pl.BlockSpec((1, WINDOW, D_KV), lambda b, t: (b, 0, 0)),
                   pl.BlockSpec((1, 8, D_LRU), lambda b, t: (b, 0, 0)),
                   pl.BlockSpec((1, 8, D_LRU), lambda b, t: (b, 0, 0))],
        out_shape=[jax.ShapeDtypeStruct((B, T, D_MODEL), F32),
                   jax.ShapeDtypeStruct((B, WINDOW, D_KV), F32),
                   jax.ShapeDtypeStruct((B, WINDOW, D_KV), F32),
                   jax.ShapeDtypeStruct((B, 8, D_LRU), F32),
                   jax.ShapeDtypeStruct((B, 8, D_LRU), F32)],
        scratch_shapes=[pltpu.VMEM((TM, D_IN), F32),
                        pltpu.VMEM((TM + 8, D_LRU), F32),
                        pltpu.VMEM((TM, D_LRU), F32),
                        pltpu.VMEM((TM, D_LRU), F32),
                        pltpu.VMEM((TM, D_LRU), F32),
                        pltpu.VMEM((8, D_LRU), F32),
                        pltpu.VMEM((8, TM + WINDOW, D_KV), BF16),
                        pltpu.VMEM((TM, D_MODEL), BF16)],
        compiler_params=_params(("arbitrary", "arbitrary")),
        name="prompt_mixer",
    )(sinks, x_prompt, gmix, win, wconv, bconv, wax, ba, bx, lam, wout)

    y_prompt, pffn8 = pl.pallas_call(
        _prompt_ffn_kernel,
        grid=(B, NT),
        in_specs=[pl.BlockSpec((1, TM, D_MODEL), lambda b, t: (b, t, 0)),
                  pl.BlockSpec((1, MEM_LEN, D_MEM), lambda b, t: (b, 0, 0)),
                  pl.BlockSpec((1, MEM_LEN, D_MEM), lambda b, t: (b, 0, 0)),
                  _const_spec((1, D_MODEL), 2), _const_spec((D_MODEL, D_MEM), 2), _const_spec((D_MEM, D_MODEL), 2),
                  _const_spec((1, D_MODEL), 2), _const_spec((D_MODEL, D_FF), 2), _const_spec((D_MODEL, D_FF), 2),
                  _const_spec((FFN_CONV_W, D_FF), 2), _const_spec((1, D_FF), 2), _const_spec((D_FF, D_MODEL), 2),
                  _const_spec((1, D_MODEL), 2)],
        out_specs=[pl.BlockSpec((1, TM, D_MODEL), lambda b, t: (b, t, 0)),
                   pl.BlockSpec((1, 8, D_FF), lambda b, t: (b, 0, 0))],
        out_shape=[jax.ShapeDtypeStruct((B, T, D_MODEL), F32),
                   jax.ShapeDtypeStruct((B, 8, D_FF), F32)],
        scratch_shapes=[pltpu.VMEM((TM, D_MEM), BF16),
                        pltpu.VMEM((2, TM + 8, FF_CHUNK), F32),
                        pltpu.VMEM((8, D_FF), F32)],
        compiler_params=_params(("arbitrary", "arbitrary")),
        name="prompt_ffn",
    )(x1p, mkb, mvb, gcross, wq, wo, gffn, wg, wu, wfc, bfc, wd, gfin)

    xs = x_sample.reshape(SR, D_MODEL)
    conv_prev4 = jnp.pad(state_lru_conv[0], ((0, 0), (0, 1), (0, 0))).reshape(SR, D_LRU)
    h0rep = jnp.repeat(state_lru_h[0], 4, axis=0)
    ck = cache_swa_k[0].reshape(NB, WINDOW, D_KV)
    cv = cache_swa_v[0].reshape(NB, WINDOW, D_KV)
    row_spec = lambda w: pl.BlockSpec((S_ROWS, w), lambda i: (i, 0))
    cache_spec = pl.BlockSpec((S_NB, WINDOW, D_KV), lambda i: (i, 0, 0))
    x1s, sk, sv, xr_s, h_s = pl.pallas_call(
        _sample_mixer_kernel,
        grid=(NB // S_NB,),
        in_specs=[smem, row_spec(D_MODEL), row_spec(D_LRU), row_spec(D_LRU), cache_spec, cache_spec,
                  _const_spec((1, D_MODEL), 1), _const_spec((D_MODEL, D_IN), 1),
                  _const_spec((LRU_CONV_W, D_LRU), 1), _const_spec((1, D_LRU), 1),
                  _const_spec((2, 256, 512), 1), _const_spec((1, D_LRU), 1), _const_spec((1, D_LRU), 1),
                  _const_spec((1, D_LRU), 1), _const_spec((D_MODEL, D_MODEL), 1)],
        out_specs=[row_spec(D_MODEL), cache_spec, cache_spec, row_spec(D_LRU), row_spec(D_LRU)],
        out_shape=[jax.ShapeDtypeStruct((SR, D_MODEL), F32),
                   jax.ShapeDtypeStruct((NB, WINDOW, D_KV), F32),
                   jax.ShapeDtypeStruct((NB, WINDOW, D_KV), F32),
                   jax.ShapeDtypeStruct((SR, D_LRU), F32),
                   jax.ShapeDtypeStruct((SR, D_LRU), F32)],
        scratch_shapes=[pltpu.VMEM((S_ROWS, D_ATTN), F32),
                        pltpu.VMEM((S_ROWS, D_KV), F32),
                        pltpu.VMEM((S_ROWS, D_KV), F32),
                        pltpu.VMEM((2, 2 * WINDOW, D_KV), F32),
                        pltpu.VMEM((2, 2 * WINDOW, D_KV), F32),
                        pltpu.VMEM((S_ROWS, D_ATTN), F32)],
        compiler_params=_params(("arbitrary",)),
        name="sample_mixer",
    )(sinks, xs, conv_prev4, h0rep, ck, cv, gmix, win, wconv, bconv, wax, ba, bx, lam, wout)

    cmk = cache_mem_k.reshape(NB, MEM_LEN * N_MEM_HEADS, MEM_HEAD_DIM)
    cmv = cache_mem_v.reshape(NB, MEM_LEN * N_MEM_HEADS, MEM_HEAD_DIM)
    crow = pl.BlockSpec((C_ROWS, D_MODEL), lambda i: (i, 0))
    cmem = pl.BlockSpec((C_NB, MEM_LEN * N_MEM_HEADS, MEM_HEAD_DIM), lambda i: (i, 0, 0))
    x2s = pl.pallas_call(
        _sample_cross_kernel,
        grid=(NB // C_NB,),
        in_specs=[crow, cmem, cmem, _const_spec((1, D_MODEL), 1), _const_spec((D_MODEL, D_MEM), 1),
                  _const_spec((D_MEM, D_MODEL), 1)],
        out_specs=crow,
        out_shape=jax.ShapeDtypeStruct((SR, D_MODEL), F32),
        scratch_shapes=[pltpu.VMEM((C_ROWS, D_MEM), F32), pltpu.VMEM((C_ROWS, D_MEM), F32)],
        compiler_params=_params(("arbitrary",)),
        name="sample_cross",
    )(x1s, cmk, cmv, gcross, wq, wo)

    ffn_prev4 = jnp.pad(state_ffn_conv[0], ((0, 0), (0, 2), (0, 0))).reshape(SR, D_FF)
    y_s, gate_s = pl.pallas_call(
        _sample_ffn_kernel,
        grid=(1,),
        in_specs=[_const_spec((SR, D_MODEL), 1), _const_spec((SR, D_FF), 1),
                  _const_spec((1, D_MODEL), 1), _const_spec((D_MODEL, D_FF), 1), _const_spec((D_MODEL, D_FF), 1),
                  _const_spec((FFN_CONV_W, D_FF), 1), _const_spec((1, D_FF), 1), _const_spec((D_FF, D_MODEL), 1),
                  _const_spec((1, D_MODEL), 1)],
        out_specs=[pl.BlockSpec((SR, D_MODEL), lambda i: (0, 0)), pl.BlockSpec((SR, D_FF), lambda i: (0, 0))],
        out_shape=[jax.ShapeDtypeStruct((SR, D_MODEL), F32), jax.ShapeDtypeStruct((SR, D_FF), F32)],
        compiler_params=_params(("arbitrary",)),
        name="sample_ffn",
    )(x2s, ffn_prev4, gffn, wg, wu, wfc, bfc, wd, gfin)

    p_swa_k = pk.reshape(1, B, WINDOW, N_KV_HEADS, HEAD_DIM)
    p_swa_v = pv.reshape(1, B, WINDOW, N_KV_HEADS, HEAD_DIM)
    p_mem_k = mk.reshape(1, B, MEM_LEN, N_MEM_HEADS, MEM_HEAD_DIM)
    p_mem_v = mv.reshape(1, B, MEM_LEN, N_MEM_HEADS, MEM_HEAD_DIM)
    p_lru_conv = pconv8[None, :, 8 - (LRU_CONV_W - 1):, :]
    p_lru_h = ph8[None, :, 0, :]
    p_ffn_conv = pffn8[None, :, 8 - (FFN_CONV_W - 1):, :]
    y_sample = y_s.reshape(NB, 4, D_MODEL)
    s_swa_k = sk.reshape(1, NB, WINDOW, N_KV_HEADS, HEAD_DIM)
    s_swa_v = sv.reshape(1, NB, WINDOW, N_KV_HEADS, HEAD_DIM)
    s_lru_conv = xr_s.reshape(NB, 4, D_LRU)[None, :, 1:, :]
    s_lru_h = h_s.reshape(NB, 4, D_LRU)[None, :, 3, :]
    s_ffn_conv = gate_s.reshape(NB, 4, D_FF)[None, :, 2:, :]
    return (y_prompt, y_sample, p_swa_k, p_swa_v, p_mem_k, p_mem_v, p_lru_conv, p_lru_h, p_ffn_conv,
            s_swa_k, s_swa_v, s_lru_conv, s_lru_h, s_ffn_conv)
```

```python
import functools

import numpy as np
import jax
import jax.numpy as jnp
from jax import lax
from jax.experimental import pallas as pl
from jax.experimental.pallas import tpu as pltpu

D_MODEL = 1024
D_LRU = 512
LRU_BLOCKS = 8
LRU_BLOCK = 64
LRU_CONV_W = 4
LRU_C = 8.0
N_Q_HEADS = 8
N_KV_HEADS = 2
HEAD_DIM = 64
D_ATTN = 512
D_KV = 128
WINDOW = 128
D_IN = 1792
MEM_LEN = 256
N_MEM_HEADS = 4
MEM_HEAD_DIM = 128
D_MEM = 512
D_FF = 3072
FFN_CONV_W = 3
EPS = 1e-6
NEG_INF = -1e30

F32 = jnp.float32
BF16 = jnp.bfloat16

SLOPES = [float(2.0 ** (-8.0 * (i + 1) / N_Q_HEADS)) for i in range(N_Q_HEADS)]
ATTN_SCALE = HEAD_DIM ** -0.5
MEM_SCALE = MEM_HEAD_DIM ** -0.5
LOG2E = 1.4426950408889634
F32_TINY = 1.1754944e-38

TM = 1024
HR = 512
SEG = HR // 8
TF = 512
FF_CHUNK = 512
S_NB = 16
S_ROWS = 4 * S_NB
C_NB = 8
C_ROWS = 4 * C_NB
VMEM_LIMIT = 56 * 1024 * 1024


def _dot(a, b):
    return jnp.dot(a, b, preferred_element_type=F32)


def _dot_nt(a, b):
    return lax.dot_general(a, b, (((1,), (1,)), ((), ())), preferred_element_type=F32)


def _rmsnorm(x, g):
    ms = jnp.mean(x * x, axis=-1, keepdims=True)
    return x * lax.rsqrt(ms + EPS) * g


def _gelu(x):
    c = 0.7978845608028654
    return x * (0.5 * (1.0 + jnp.tanh(c * (x + 0.044715 * (x * x * x)))))


def _sigmoid(x):
    return 1.0 / (1.0 + jnp.exp(-x))


def _softplus(x):
    return jnp.maximum(x, 0.0) + jnp.log1p(jnp.exp(-jnp.abs(x)))


def _lru_gates(xc, wax_ref, ba, bx, lam):
    xcb = xc.astype(BF16)
    pa, px = [], []
    for gi in range(2):
        pre = _dot(xcb[:, gi * 256:(gi + 1) * 256], wax_ref[gi])
        pa.append(pre[:, :256])
        px.append(pre[:, 256:])
    r = _sigmoid(jnp.concatenate(pa, axis=1) + ba)
    i = _sigmoid(jnp.concatenate(px, axis=1) + bx)
    log_a = (-LRU_C * _softplus(-lam)) * r
    a = jnp.exp(log_a)
    om = -jnp.tanh(log_a) * (a * a + 1.0)
    b = (om * lax.rsqrt(jnp.maximum(om, F32_TINY))) * (i * xc)
    return a, b


def _head_variants(t):
    lo = lax.broadcasted_iota(jnp.int32, t.shape, 1) < HEAD_DIM
    tr = pltpu.roll(t, HEAD_DIM, 1)
    z = jnp.zeros_like(t)
    x0 = jnp.where(lo, t, z)
    y0 = jnp.where(lo, z, tr)
    x1 = jnp.where(lo, tr, z)
    y1 = jnp.where(lo, z, t)
    return [v.astype(BF16) for v in (x0, y0, x1, y1)]


def _mem_kv_kernel(mem_ref, g_ref, wk_ref, wv_ref, mk_ref, mv_ref, mkb_ref, mvb_ref):
    n = _rmsnorm(mem_ref[0], g_ref[...]).astype(BF16)
    mk = _dot(n, wk_ref[...])
    mv = _dot(n, wv_ref[...])
    mk_ref[0] = mk
    mv_ref[0] = mv
    mkb_ref[0] = mk.astype(BF16)
    mvb_ref[0] = mv.astype(BF16)


def _prompt_mixer_kernel(sinks_ref, x_ref, g_ref, win_ref, wconv_ref, bconv_ref, wax_ref, ba_ref, bx_ref,
                         lam_ref, wout_ref,
                         x1_ref, pk_ref, pv_ref, pconv_ref, ph_ref,
                         proj_s, pin_s, pout_s, xtail_s, a_s, b_s, hl_s, h_s, hc_s, kv_s, ymix_s):
    t = pl.program_id(1)

    @pl.when(t == 0)
    def _():
        xtail_s[...] = jnp.zeros((24, D_LRU), F32)
        hc_s[...] = jnp.zeros((8, D_LRU), F32)
        kv_s[:, 0:WINDOW, :] = jnp.zeros((8, WINDOW, D_KV), BF16)

    def sec(i):
        return slice(i * HR, (i + 1) * HR)

    def in_proj(i):
        n1 = _rmsnorm(x_ref[0, sec(i), :], g_ref[...]).astype(BF16)
        proj_s[sec(i), :] = _dot(n1, win_ref[...])

    def seg_rows(j):
        return pl.ds(SEG * (j % 8) + j // 8, 8, stride=8)

    row8 = lax.broadcasted_iota(jnp.int32, (8, D_LRU), 0)

    def lru_gates(i):
        r0 = i * HR
        for j in range(HR // 8):
            for l in range(D_LRU // 128):
                pin_s[l, seg_rows(j), :] = proj_s[r0 + 8 * j:r0 + 8 * j + 8, l * 128:(l + 1) * 128]
        xs = jnp.concatenate([pin_s[l] for l in range(D_LRU // 128)], axis=1)
        tail = xs[HR - 24:, :]
        prev = xtail_s[...]
        xtail_s[...] = tail
        heads = [jnp.where(row8 == 0, pltpu.roll(prev[8 * q:8 * q + 8, :], 1, 0),
                           pltpu.roll(tail[8 * q:8 * q + 8, :], 1, 0)) for q in range(3)]
        sh1 = jnp.concatenate(heads[2:] + [xs[:HR - 8, :]], axis=0)
        sh2 = jnp.concatenate(heads[1:] + [xs[:HR - 16, :]], axis=0)
        sh3 = jnp.concatenate(heads + [xs[:HR - 24, :]], axis=0)
        wc = wconv_ref[...]
        xc = bconv_ref[...] + wc[0:1] * sh3 + wc[1:2] * sh2 + wc[2:3] * sh1 + wc[3:4] * xs
        a, b = _lru_gates(xc, wax_ref, ba_ref[...], bx_ref[...], lam_ref[...])
        a_s[sec(i), :] = a
        b_s[sec(i), :] = b

    def lru_scan(i, hcar):
        r0 = i * HR
        hl = b_s[r0:r0 + 8, :]
        ac = a_s[r0:r0 + 8, :]
        hl_s[0:8, :] = hl
        for g in range(1, SEG):
            av = a_s[r0 + 8 * g:r0 + 8 * g + 8, :]
            hl = av * hl + b_s[r0 + 8 * g:r0 + 8 * g + 8, :]
            ac = av * ac
            hl_s[8 * g:8 * g + 8, :] = hl
            a_s[r0 + 8 * g:r0 + 8 * g + 8, :] = ac
        hin = hcar
        for s in range(8):
            hend = hl + ac * hin
            if s < 7:
                hin = jnp.where(row8 == s + 1, pltpu.roll(hend, 1, 0), hin)
        hcar = jnp.broadcast_to(hend[7:8, :], (8, D_LRU))
        for g in range(SEG):
            hg = hl_s[8 * g:8 * g + 8, :] + a_s[r0 + 8 * g:r0 + 8 * g + 8, :] * hin
            for l in range(D_LRU // 128):
                pout_s[l, 8 * g:8 * g + 8, :] = hg[:, l * 128:(l + 1) * 128]
        for j in range(HR // 8):
            h_s[r0 + 8 * j:r0 + 8 * j + 8, :] = jnp.concatenate(
                [pout_s[l, seg_rows(j), :] for l in range(D_LRU // 128)], axis=1)
        gate = proj_s[sec(i), D_LRU:2 * D_LRU]
        ymix_s[sec(i), 0:D_LRU] = (h_s[sec(i), :] * _gelu(gate)).astype(BF16)
        return hcar

    def kv_prep(i):
        k = proj_s[sec(i), 1536:1664]
        v = proj_s[sec(i), 1664:1792]
        for n, arr in enumerate(_head_variants(k) + _head_variants(v)):
            kv_s[n, WINDOW + i * HR:WINDOW + (i + 1) * HR, :] = arr

    qi = lax.broadcasted_iota(jnp.int32, (WINDOW, 2 * WINDOW), 0)
    kj = lax.broadcasted_iota(jnp.int32, (WINDOW, 2 * WINDOW), 1)
    dist = qi + WINDOW - kj
    valid = (dist >= 0) & (dist < WINDOW)
    valid0 = valid & (kj >= jnp.where(t == 0, WINDOW, 0))
    distf = dist.astype(F32)
    bias = [jnp.where(valid, -SLOPES[h] * distf, NEG_INF) for h in range(N_Q_HEADS)]
    bias0 = [jnp.where(valid0, -SLOPES[h] * distf, NEG_INF) for h in range(N_Q_HEADS)]
    lane_lo = lax.broadcasted_iota(jnp.int32, (WINDOW, 2 * HEAD_DIM), 1) < HEAD_DIM

    r512 = lax.broadcasted_iota(jnp.int32, (4 * WINDOW, 2 * HEAD_DIM), 0)
    l512 = lax.broadcasted_iota(jnp.int32, (4 * WINDOW, 2 * HEAD_DIM), 1)
    ones_cols = jnp.where((r512 < 2 * WINDOW) == (l512 < HEAD_DIM), 1.0, 0.0).astype(BF16)
    n_iter = (HR // WINDOW) * 4

    def qk(n):
        j, c = divmod(n, 4)
        hk = c // 2
        rows = slice(j * WINDOW, (j + 1) * WINDOW)
        win = slice(j * WINDOW, (j + 2) * WINDOW)
        qc = (proj_s[rows, 1024 + c * 128:1024 + (c + 1) * 128] * ATTN_SCALE).astype(BF16)
        kcat = jnp.concatenate([kv_s[2 * hk, win, :], kv_s[2 * hk + 1, win, :]], axis=0)
        return _dot_nt(qc, kcat)

    def softmax_pv(n, s):
        j, c = divmod(n, 4)
        hk = c // 2
        rows = slice(j * WINDOW, (j + 1) * WINDOW)
        win = slice(j * WINDOW, (j + 2) * WINDOW)
        vcat = jnp.concatenate([kv_s[4 + 2 * hk, win, :], kv_s[5 + 2 * hk, win, :]], axis=0)
        vaug = jnp.concatenate([vcat, ones_cols], axis=1)
        ps, es = [], []
        for half in range(2):
            h = 2 * c + half
            sink = sinks_ref[h]
            sh = s[:, half * 256:(half + 1) * 256] + (bias0[h] if j == 0 else bias[h])
            m = jnp.maximum(jnp.max(sh, axis=-1, keepdims=True), sink)
            ps.append(jnp.exp(sh - m).astype(BF16))
            es.append(jnp.exp(sink - m))
        oa = _dot(jnp.concatenate(ps, axis=1), vaug)
        den = oa[:, 128:256] + jnp.where(lane_lo, es[0], es[1])
        ymix_s[rows, D_LRU + c * 128:D_LRU + (c + 1) * 128] = (oa[:, 0:128] * (1.0 / den)).astype(BF16)

    def attention(i):
        depth = 2
        base = i * n_iter
        pend = [qk(base + n) for n in range(depth)]
        for n in range(n_iter):
            s = pend.pop(0)
            if n + depth < n_iter:
                pend.append(qk(base + n + depth))
            softmax_pv(base + n, s)

    def out_proj(i):
        x1_ref[0, sec(i), :] = x_ref[0, sec(i), :] + _dot(ymix_s[sec(i), :], wout_ref[...])

    n_sec = TM // HR
    h = hc_s[...]
    in_proj(0)
    lru_gates(0)
    for i in range(n_sec):
        if i + 1 < n_sec:
            in_proj(i + 1)
        h = lru_scan(i, h)
        kv_prep(i)
        attention(i)
        if i + 1 < n_sec:
            lru_gates(i + 1)
        out_proj(i)

    hc_s[...] = h
    ph_ref[0] = h
    pconv_ref[0] = proj_s[TM - 8:, 0:D_LRU]
    pk_ref[0] = proj_s[TM - WINDOW:, 1536:1664]
    pv_ref[0] = proj_s[TM - WINDOW:, 1664:1792]
    kv_s[:, 0:WINDOW, :] = kv_s[:, TM:TM + WINDOW, :]


def _ffn_chunks(n3, acc, gprev_fn, wg_ref, wu_ref, wfc_ref, bfc_ref, wd_ref, on_gate):
    nc = D_FF // FF_CHUNK

    def up(c):
        cs = slice(c * FF_CHUNK, (c + 1) * FF_CHUNK)
        return _dot(n3, wg_ref[:, cs]), _dot(n3, wu_ref[:, cs])

    nxt = up(0)
    for c in range(nc):
        cs = slice(c * FF_CHUNK, (c + 1) * FF_CHUNK)
        g, u = nxt
        if c + 1 < nc:
            nxt = up(c + 1)
        g2, g1 = gprev_fn(c, g)
        on_gate(c, g)
        wfc = wfc_ref[:, cs]
        conv = bfc_ref[:, cs] + wfc[0:1] * g2 + wfc[1:2] * g1 + wfc[2:3] * g
        hmid = (_gelu(conv) * u).astype(BF16)
        acc = acc + _dot(hmid, wd_ref[cs, :])
    return acc


def _prompt_ffn_kernel(x1_ref, mk_ref, mv_ref, gc_ref, wq_ref, wo_ref, gf_ref, wg_ref, wu_ref, wfc_ref, bfc_ref,
                       wd_ref, gfin_ref,
                       y_ref, pffn_ref,
                       oc_s, gbuf_s, gcar_s):
    t = pl.program_id(1)

    @pl.when(t == 0)
    def _():
        gcar_s[...] = jnp.zeros((8, D_FF), F32)

    x1 = x1_ref[0]
    qc = _dot(_rmsnorm(x1, gc_ref[...]).astype(BF16), wq_ref[...]).astype(BF16)
    hsl = [slice(h * MEM_HEAD_DIM, (h + 1) * MEM_HEAD_DIM) for h in range(N_MEM_HEADS)]
    ss = [_dot_nt(qc[:, hs], mk_ref[0, :, hs]) for hs in hsl]
    for h, hs in enumerate(hsl):
        s = ss[h]
        m = jnp.max(s, axis=-1, keepdims=True)
        p = jnp.exp2((s - m) * (MEM_SCALE * LOG2E))
        l = jnp.sum(p, axis=-1, keepdims=True)
        o = _dot(p.astype(BF16), mv_ref[0, :, hs]) * (1.0 / l)
        oc_s[:, hs] = o.astype(BF16)
    x2 = x1 + _dot(oc_s[...], wo_ref[...])
    n3 = _rmsnorm(x2, gf_ref[...]).astype(BF16)

    def gprev(c, g):
        cs = slice(c * FF_CHUNK, (c + 1) * FF_CHUNK)
        buf = gbuf_s.at[c % 2]
        buf[0:8, :] = gcar_s[:, cs]
        buf[8:TF + 8, :] = g
        return buf[6:6 + TF, :], buf[7:7 + TF, :]

    def on_gate(c, g):
        cs = slice(c * FF_CHUNK, (c + 1) * FF_CHUNK)
        tail = g[TF - 8:, :]
        gcar_s[:, cs] = tail
        pffn_ref[0, :, cs] = tail

    acc = _ffn_chunks(n3, x2, gprev, wg_ref, wu_ref, wfc_ref, bfc_ref, wd_ref, on_gate)
    y_ref[0] = _rmsnorm(acc, gfin_ref[...])


def _sample_mixer_kernel(sinks_ref, x_ref, prev4_ref, h0_ref, ck_ref, cv_ref, g_ref, win_ref, wconv_ref, bconv_ref,
                         wax_ref, ba_ref, bx_ref, lam_ref, wout_ref,
                         x1_ref, sk_ref, sv_ref, xr_ref, h_ref,
                         q_s, k_s, v_s, kall_s, vall_s, yatt_s):
    R = S_ROWS

    @pl.when(pl.program_id(0) == 0)
    def _():
        kall_s[...] = jnp.zeros(kall_s.shape, F32)
        vall_s[...] = jnp.zeros(vall_s.shape, F32)

    x = x_ref[...]
    n1 = _rmsnorm(x, g_ref[...]).astype(BF16)
    proj = _dot(n1, win_ref[...])
    xr = proj[:, 0:D_LRU]
    gate = proj[:, D_LRU:2 * D_LRU]
    q_s[...] = proj[:, 1024:1536] * ATTN_SCALE
    k_s[...] = proj[:, 1536:1664]
    v_s[...] = proj[:, 1664:1792]
    xr_ref[...] = xr

    tmod = lax.broadcasted_iota(jnp.int32, (R, D_LRU), 0) & 3
    prev4 = prev4_ref[...]
    xs1 = jnp.where(tmod >= 1, pltpu.roll(xr, 1, 0), pltpu.roll(prev4, R - 2, 0))
    xs2 = jnp.where(tmod >= 2, pltpu.roll(xr, 2, 0), pltpu.roll(prev4, R - 1, 0))
    xs3 = jnp.where(tmod >= 3, pltpu.roll(xr, 3, 0), prev4)
    wc = wconv_ref[...]
    xc = bconv_ref[...] + wc[0:1] * xs3 + wc[1:2] * xs2 + wc[2:3] * xs1 + wc[3:4] * xr

    a, b = _lru_gates(xc, wax_ref, ba_ref[...], bx_ref[...], lam_ref[...])
    for s in (1, 2):
        m = tmod >= s
        a_sh = pltpu.roll(a, s, 0)
        b_sh = pltpu.roll(b, s, 0)
        b = jnp.where(m, a * b_sh + b, b)
        a = jnp.where(m, a * a_sh, a)
    h = a * h0_ref[...] + b
    h_ref[...] = h
    y_lru = h * _gelu(gate)

    r16 = lax.broadcasted_iota(jnp.int32, (16, 512), 0)
    c16 = lax.broadcasted_iota(jnp.int32, (16, 512), 1)
    dist = (r16 & 3) + WINDOW - (c16 & 255)
    valid = (dist >= 0) & (dist < WINDOW)
    distf = dist.astype(F32)
    first8 = r16 < 8
    lowhalf = c16 < 256
    biases, sinkcols = [], []
    for hk in range(N_KV_HEADS):
        hd = [[2 * (2 * hk + c) + half for half in range(2)] for c in range(2)]
        slope = jnp.where(first8,
                          jnp.where(lowhalf, SLOPES[hd[0][0]], SLOPES[hd[0][1]]),
                          jnp.where(lowhalf, SLOPES[hd[1][0]], SLOPES[hd[1][1]]))
        biases.append(jnp.where(valid, -slope * distf, NEG_INF))
        sinkcols.append([jnp.where(first8[:, 0:1], sinks_ref[hd[0][half]], sinks_ref[hd[1][half]])
                         for half in range(2)])
    bat0_16 = (r16 & 7) < 4
    r8 = lax.broadcasted_iota(jnp.int32, (8, D_KV), 0)
    lane_lo = lax.broadcasted_iota(jnp.int32, (16, D_KV), 1) < HEAD_DIM

    def pair_body(p, carry):
        r0 = pl.multiple_of(p * 8, 8)
        q8 = q_s[pl.ds(r0, 8), :]
        k8 = k_s[pl.ds(r0, 8), :]
        v8 = v_s[pl.ds(r0, 8), :]
        kvar, vvar = [], []
        for bb in range(2):
            bidx = 2 * p + bb
            if bb == 0:
                kn, vn = k8, v8
            else:
                kn, vn = pltpu.roll(k8, 4, 0), pltpu.roll(v8, 4, 0)
            kall_s[bb, 0:WINDOW, :] = ck_ref[bidx]
            vall_s[bb, 0:WINDOW, :] = cv_ref[bidx]
            kall_s[bb, WINDOW:WINDOW + 8, :] = jnp.where(r8 < 4, kn, 0.0)
            vall_s[bb, WINDOW:WINDOW + 8, :] = jnp.where(r8 < 4, vn, 0.0)
            sk_ref[bidx] = kall_s[bb, pl.ds(4, WINDOW), :]
            sv_ref[bidx] = vall_s[bb, pl.ds(4, WINDOW), :]
            kvar.append(_head_variants(kall_s[bb]))
            vvar.append(_head_variants(vall_s[bb]))
        for hk in range(N_KV_HEADS):
            c0 = 2 * hk
            lhs = jnp.concatenate([q8[:, c0 * 128:(c0 + 1) * 128], q8[:, (c0 + 1) * 128:(c0 + 2) * 128]],
                                  axis=0).astype(BF16)
            sb = [_dot_nt(lhs, jnp.concatenate([kvar[bb][2 * hk], kvar[bb][2 * hk + 1]], axis=0))
                  for bb in range(2)]
            s = jnp.where(bat0_16, sb[0], sb[1]) + biases[hk]
            ps, invs = [], []
            for half in range(2):
                sh = s[:, half * 256:(half + 1) * 256]
                sink = sinkcols[hk][half]
                m = jnp.maximum(jnp.max(sh, axis=-1, keepdims=True), sink)
                pe = jnp.exp(sh - m)
                l = jnp.sum(pe, axis=-1, keepdims=True) + jnp.exp(sink - m)
                ps.append(pe.astype(BF16))
                invs.append(1.0 / l)
            pcat = jnp.concatenate(ps, axis=1)
            ob = [_dot(pcat, jnp.concatenate([vvar[bb][2 * hk], vvar[bb][2 * hk + 1]], axis=0))
                  for bb in range(2)]
            o = jnp.where(bat0_16[:, 0:D_KV], ob[0], ob[1]) * jnp.where(lane_lo, invs[0], invs[1])
            yatt_s[pl.ds(r0, 8), c0 * 128:(c0 + 1) * 128] = o[0:8]
            yatt_s[pl.ds(r0, 8), (c0 + 1) * 128:(c0 + 2) * 128] = o[8:16]
        return carry

    lax.fori_loop(0, S_NB // 2, pair_body, 0)

    ymix = jnp.concatenate([y_lru, yatt_s[...]], axis=1).astype(BF16)
    x1_ref[...] = x + _dot(ymix, wout_ref[...])


def _load_mem_heads(ref, b):
    return jnp.concatenate([ref[b, pl.ds(h, MEM_LEN, stride=N_MEM_HEADS), :] for h in range(N_MEM_HEADS)],
                           axis=1).astype(BF16)


def _sample_cross_kernel(x1_ref, mk_ref, mv_ref, gc_ref, wq_ref, wo_ref, x2_ref, q_s, oc_s):
    x1 = x1_ref[...]
    q_s[...] = _dot(_rmsnorm(x1, gc_ref[...]).astype(BF16), wq_ref[...])

    lane_head = lax.broadcasted_iota(jnp.int32, (8, D_MEM), 1) // MEM_HEAD_DIM
    bat0 = (lax.broadcasted_iota(jnp.int32, (32, D_MEM), 0) & 7) < 4
    bat0_s = (lax.broadcasted_iota(jnp.int32, (32, MEM_LEN), 0) & 7) < 4

    for p in range(C_NB // 2):
        q8 = q_s[p * 8:(p + 1) * 8, :]
        lhs = jnp.concatenate([jnp.where(lane_head == h, q8, 0.0) for h in range(N_MEM_HEADS)],
                              axis=0).astype(BF16)
        sb = [_dot_nt(lhs, _load_mem_heads(mk_ref, 2 * p + bb)) for bb in range(2)]
        s = jnp.where(bat0_s, sb[0], sb[1]) * MEM_SCALE
        m = jnp.max(s, axis=-1, keepdims=True)
        pe = jnp.exp(s - m)
        l = jnp.sum(pe, axis=-1, keepdims=True)
        pb = pe.astype(BF16)
        ob = [_dot(pb, _load_mem_heads(mv_ref, 2 * p + bb)) for bb in range(2)]
        o = jnp.where(bat0, ob[0], ob[1]) * (1.0 / l)
        out = jnp.where(lane_head == 0, o[0:8], 0.0)
        for h in range(1, N_MEM_HEADS):
            out = out + jnp.where(lane_head == h, o[h * 8:(h + 1) * 8], 0.0)
        oc_s[p * 8:(p + 1) * 8, :] = out

    x2_ref[...] = x1 + _dot(oc_s[...].astype(BF16), wo_ref[...])


def _sample_ffn_kernel(x2_ref, prev4_ref, gf_ref, wg_ref, wu_ref, wfc_ref, bfc_ref, wd_ref, gfin_ref,
                       y_ref, gate_ref):
    R = x2_ref.shape[0]
    x2 = x2_ref[...]
    n3 = _rmsnorm(x2, gf_ref[...]).astype(BF16)
    tmod = lax.broadcasted_iota(jnp.int32, (R, FF_CHUNK), 0) & 3

    def gprev(c, g):
        cs = slice(c * FF_CHUNK, (c + 1) * FF_CHUNK)
        prev4 = prev4_ref[:, cs]
        g1 = jnp.where(tmod >= 1, pltpu.roll(g, 1, 0), pltpu.roll(prev4, R - 1, 0))
        g2 = jnp.where(tmod >= 2, pltpu.roll(g, 2, 0), prev4)
        return g2, g1

    def on_gate(c, g):
        gate_ref[:, c * FF_CHUNK:(c + 1) * FF_CHUNK] = g

    acc = _ffn_chunks(n3, x2, gprev, wg_ref, wu_ref, wfc_ref, bfc_ref, wd_ref, on_gate)
    y_ref[...] = _rmsnorm(acc, gfin_ref[...])


def _const_spec(shape, grid_rank):
    zeros = (0,) * len(shape)
    if grid_rank == 1:
        return pl.BlockSpec(shape, lambda i: zeros, pipeline_mode=pl.Buffered(1))
    return pl.BlockSpec(shape, lambda i, j: zeros, pipeline_mode=pl.Buffered(1))


def _block_diag4(w):
    eye = jnp.eye(4, dtype=w.dtype)
    return (w[:, :, None, :] * eye[:, None, :, None]).reshape(4 * LRU_BLOCK, 4 * LRU_BLOCK)


def _params(sem):
    return pltpu.CompilerParams(dimension_semantics=sem, vmem_limit_bytes=VMEM_LIMIT)


def kernel(x_prompt, x_sample, cache_swa_k, cache_swa_v, cache_mem_k, cache_mem_v, state_lru_conv, state_lru_h, state_ffn_conv, mem_prompt, g_mix, w_in, w_lru_conv, b_lru_conv, w_lru_a, b_lru_a, w_lru_x, b_lru_x, lru_lambda, attn_sinks, w_out, g_cross, g_mem, w_mem_q, w_mem_k, w_mem_v, w_mem_o, g_ffn, w_ffn_gate, w_ffn_up, w_ffn_conv, b_ffn_conv, w_ffn_down, g_final):
    B, T, _ = x_prompt.shape
    NB = x_sample.shape[0]
    NT = T // TM
    SR = NB * 4

    win = w_in[0].astype(BF16)
    wout = w_out[0].astype(BF16)
    wq = w_mem_q[0].astype(BF16)
    wk = w_mem_k[0].astype(BF16)
    wv = w_mem_v[0].astype(BF16)
    wo = w_mem_o[0].astype(BF16)
    wg = w_ffn_gate[0].astype(BF16)
    wu = w_ffn_up[0].astype(BF16)
    wd = w_ffn_down[0].astype(BF16)
    wax = jnp.stack([
        jnp.concatenate([_block_diag4(w_lru_a[0, 4 * gi:4 * gi + 4]), _block_diag4(w_lru_x[0, 4 * gi:4 * gi + 4])],
                        axis=1) for gi in range(2)]).astype(BF16)
    gmix, gcross, gmem, gffn = g_mix, g_cross, g_mem, g_ffn
    gfin = g_final.reshape(1, D_MODEL)
    wconv, bconv = w_lru_conv[0], b_lru_conv
    ba, bx, lam = b_lru_a, b_lru_x, lru_lambda
    wfc, bfc = w_ffn_conv[0], b_ffn_conv
    sinks = attn_sinks[0]
    smem = pl.BlockSpec(memory_space=pltpu.SMEM)

    mk, mv, mkb, mvb = pl.pallas_call(
        _mem_kv_kernel,
        grid=(B,),
        in_specs=[pl.BlockSpec((1, MEM_LEN, D_MODEL), lambda b: (b, 0, 0)),
                  _const_spec((1, D_MODEL), 1), _const_spec((D_MODEL, D_MEM), 1), _const_spec((D_MODEL, D_MEM), 1)],
        out_specs=[pl.BlockSpec((1, MEM_LEN, D_MEM), lambda b: (b, 0, 0))] * 4,
        out_shape=[jax.ShapeDtypeStruct((B, MEM_LEN, D_MEM), F32)] * 2
        + [jax.ShapeDtypeStruct((B, MEM_LEN, D_MEM), BF16)] * 2,
        compiler_params=_params(("arbitrary",)),
        name="mem_kv",
    )(mem_prompt, gmem, wk, wv)

    x1p, pk, pv, pconv8, ph8 = pl.pallas_call(
        _prompt_mixer_kernel,
        grid=(B, NT),
        in_specs=[smem,
                  pl.BlockSpec((1, TM, D_MODEL), lambda b, t: (b, t, 0)),
                  _const_spec((1, D_MODEL), 2), _const_spec((D_MODEL, D_IN), 2),
                  _const_spec((LRU_CONV_W, D_LRU), 2), _const_spec((1, D_LRU), 2),
                  _const_spec((2, 256, 512), 2), _const_spec((1, D_LRU), 2), _const_spec((1, D_LRU), 2),
                  _const_spec((1, D_LRU), 2), _const_spec((D_MODEL, D_MODEL), 2)],
        out_specs=[pl.BlockSpec((1, TM, D_MODEL), lambda b, t: (b, t, 0)),
                   pl.BlockSpec((1, WINDOW, D_KV), lambda b, t: (b, 0, 0)),
                   pl.BlockSpec((1, WINDOW, D_KV), lambda b, t: (b, 0, 0)),
                   pl.BlockSpec((1, 8, D_LRU), lambda b, t: (b, 0, 0)),
                   pl.BlockSpec((1, 8, D_LRU), lambda b, t: (b, 0, 0))],
        out_shape=[jax.ShapeDtypeStruct((B, T, D_MODEL), F32),
                   jax.ShapeDtypeStruct((B, WINDOW, D_KV), F32),
                   jax.ShapeDtypeStruct((B, WINDOW, D_KV), F32),
                   jax.ShapeDtypeStruct((B, 8, D_LRU), F32),
                   jax.ShapeDtypeStruct((B, 8, D_LRU), F32)],
        scratch_shapes=[pltpu.VMEM((TM, D_IN), F32),
                        pltpu.VMEM((D_LRU // 128, HR, 128), F32),
                        pltpu.VMEM((D_LRU // 128, HR, 128), F32),
                        pltpu.VMEM((24, D_LRU), F32),
                        pltpu.VMEM((TM, D_LRU), F32),
                        pltpu.VMEM((TM, D_LRU), F32),
                        pltpu.VMEM((HR, D_LRU), F32),
                        pltpu.VMEM((TM, D_LRU), F32),
                        pltpu.VMEM((8, D_LRU), F32),
                        pltpu.VMEM((8, TM + WINDOW, D_KV), BF16),
                        pltpu.VMEM((TM, D_MODEL), BF16)],
        compiler_params=_params(("arbitrary", "arbitrary")),
        name="prompt_mixer",
    )(sinks, x_prompt, gmix, win, wconv, bconv, wax, ba, bx, lam, wout)

    y_prompt, pffn8 = pl.pallas_call(
        _prompt_ffn_kernel,
        grid=(B, T // TF),
        in_specs=[pl.BlockSpec((1, TF, D_MODEL), lambda b, t: (b, t, 0)),
                  pl.BlockSpec((1, MEM_LEN, D_MEM), lambda b, t: (b, 0, 0)),
                  pl.BlockSpec((1, MEM_LEN, D_MEM), lambda b, t: (b, 0, 0)),
                  _const_spec((1, D_MODEL), 2), _const_spec((D_MODEL, D_MEM), 2), _const_spec((D_MEM, D_MODEL), 2),
                  _const_spec((1, D_MODEL), 2), _const_spec((D_MODEL, D_FF), 2), _const_spec((D_MODEL, D_FF), 2),
                  _const_spec((FFN_CONV_W, D_FF), 2), _const_spec((1, D_FF), 2), _const_spec((D_FF, D_MODEL), 2),
                  _const_spec((1, D_MODEL), 2)],
        out_specs=[pl.BlockSpec((1, TF, D_MODEL), lambda b, t: (b, t, 0)),
                   pl.BlockSpec((1, 8, D_FF), lambda b, t: (b, 0, 0))],
        out_shape=[jax.ShapeDtypeStruct((B, T, D_MODEL), F32),
                   jax.ShapeDtypeStruct((B, 8, D_FF), F32)],
        scratch_shapes=[pltpu.VMEM((TF, D_MEM), BF16),
                        pltpu.VMEM((2, TF + 8, FF_CHUNK), F32),
                        pltpu.VMEM((8, D_FF), F32)],
        compiler_params=_params(("arbitrary", "arbitrary")),
        name="prompt_ffn",
    )(x1p, mkb, mvb, gcross, wq, wo, gffn, wg, wu, wfc, bfc, wd, gfin)

    xs = x_sample.reshape(SR, D_MODEL)
    conv_prev4 = jnp.pad(state_lru_conv[0], ((0, 0), (0, 1), (0, 0))).reshape(SR, D_LRU)
    h0rep = jnp.repeat(state_lru_h[0], 4, axis=0)
    ck = cache_swa_k[0].reshape(NB, WINDOW, D_KV)
    cv = cache_swa_v[0].reshape(NB, WINDOW, D_KV)
    row_spec = lambda w: pl.BlockSpec((S_ROWS, w), lambda i: (i, 0))
    cache_spec = pl.BlockSpec((S_NB, WINDOW, D_KV), lambda i: (i, 0, 0))
    x1s, sk, sv, xr_s, h_s = pl.pallas_call(
        _sample_mixer_kernel,
        grid=(NB // S_NB,),
        in_specs=[smem, row_spec(D_MODEL), row_spec(D_LRU), row_spec(D_LRU), cache_spec, cache_spec,
                  _const_spec((1, D_MODEL), 1), _const_spec((D_MODEL, D_IN), 1),
                  _const_spec((LRU_CONV_W, D_LRU), 1), _const_spec((1, D_LRU), 1),
                  _const_spec((2, 256, 512), 1), _const_spec((1, D_LRU), 1), _const_spec((1, D_LRU), 1),
                  _const_spec((1, D_LRU), 1), _const_spec((D_MODEL, D_MODEL), 1)],
        out_specs=[row_spec(D_MODEL), cache_spec, cache_spec, row_spec(D_LRU), row_spec(D_LRU)],
        out_shape=[jax.ShapeDtypeStruct((SR, D_MODEL), F32),
                   jax.ShapeDtypeStruct((NB, WINDOW, D_KV), F32),
                   jax.ShapeDtypeStruct((NB, WINDOW, D_KV), F32),
                   jax.ShapeDtypeStruct((SR, D_LRU), F32),
                   jax.ShapeDtypeStruct((SR, D_LRU), F32)],
        scratch_shapes=[pltpu.VMEM((S_ROWS, D_ATTN), F32),
                        pltpu.VMEM((S_ROWS, D_KV), F32),
                        pltpu.VMEM((S_ROWS, D_KV), F32),
                        pltpu.VMEM((2, 2 * WINDOW, D_KV), F32),
                        pltpu.VMEM((2, 2 * WINDOW, D_KV), F32),
                        pltpu.VMEM((S_ROWS, D_ATTN), F32)],
        compiler_params=_params(("arbitrary",)),
        name="sample_mixer",
    )(sinks, xs, conv_prev4, h0rep, ck, cv, gmix, win, wconv, bconv, wax, ba, bx, lam, wout)

    cmk = cache_mem_k.reshape(NB, MEM_LEN * N_MEM_HEADS, MEM_HEAD_DIM)
    cmv = cache_mem_v.reshape(NB, MEM_LEN * N_MEM_HEADS, MEM_HEAD_DIM)
    crow = pl.BlockSpec((C_ROWS, D_MODEL), lambda i: (i, 0))
    cmem = pl.BlockSpec((C_NB, MEM_LEN * N_MEM_HEADS, MEM_HEAD_DIM), lambda i: (i, 0, 0))
    x2s = pl.pallas_call(
        _sample_cross_kernel,
        grid=(NB // C_NB,),
        in_specs=[crow, cmem, cmem, _const_spec((1, D_MODEL), 1), _const_spec((D_MODEL, D_MEM), 1),
                  _const_spec((D_MEM, D_MODEL), 1)],
        out_specs=crow,
        out_shape=jax.ShapeDtypeStruct((SR, D_MODEL), F32),
        scratch_shapes=[pltpu.VMEM((C_ROWS, D_MEM), F32), pltpu.VMEM((C_ROWS, D_MEM), F32)],
        compiler_params=_params(("arbitrary",)),
        name="sample_cross",
    )(x1s, cmk, cmv, gcross, wq, wo)

    ffn_prev4 = jnp.pad(state_ffn_conv[0], ((0, 0), (0, 2), (0, 0))).reshape(SR, D_FF)
    y_s, gate_s = pl.pallas_call(
        _sample_ffn_kernel,
        grid=(1,),
        in_specs=[_const_spec((SR, D_MODEL), 1), _const_spec((SR, D_FF), 1),
                  _const_spec((1, D_MODEL), 1), _const_spec((D_MODEL, D_FF), 1), _const_spec((D_MODEL, D_FF), 1),
                  _const_spec((FFN_CONV_W, D_FF), 1), _const_spec((1, D_FF), 1), _const_spec((D_FF, D_MODEL), 1),
                  _const_spec((1, D_MODEL), 1)],
        out_specs=[pl.BlockSpec((SR, D_MODEL), lambda i: (0, 0)), pl.BlockSpec((SR, D_FF), lambda i: (0, 0))],
        out_shape=[jax.ShapeDtypeStruct((SR, D_MODEL), F32), jax.ShapeDtypeStruct((SR, D_FF), F32)],
        compiler_params=_params(("arbitrary",)),
        name="sample_ffn",
    )(x2s, ffn_prev4, gffn, wg, wu, wfc, bfc, wd, gfin)

    p_swa_k = pk.reshape(1, B, WINDOW, N_KV_HEADS, HEAD_DIM)
    p_swa_v = pv.reshape(1, B, WINDOW, N_KV_HEADS, HEAD_DIM)
    p_mem_k = mk.reshape(1, B, MEM_LEN, N_MEM_HEADS, MEM_HEAD_DIM)
    p_mem_v = mv.reshape(1, B, MEM_LEN, N_MEM_HEADS, MEM_HEAD_DIM)
    p_lru_conv = pconv8[None, :, 8 - (LRU_CONV_W - 1):, :]
    p_lru_h = ph8[None, :, 0, :]
    p_ffn_conv = pffn8[None, :, 8 - (FFN_CONV_W - 1):, :]
    y_sample = y_s.reshape(NB, 4, D_MODEL)
    s_swa_k = sk.reshape(1, NB, WINDOW, N_KV_HEADS, HEAD_DIM)
    s_swa_v = sv.reshape(1, NB, WINDOW, N_KV_HEADS, HEAD_DIM)
    s_lru_conv = xr_s.reshape(NB, 4, D_LRU)[None, :, 1:, :]
    s_lru_h = h_s.reshape(NB, 4, D_LRU)[None, :, 3, :]
    s_ffn_conv = gate_s.reshape(NB, 4, D_FF)[None, :, 2:, :]
    return (y_prompt, y_sample, p_swa_k, p_swa_v, p_mem_k, p_mem_v, p_lru_conv, p_lru_h, p_ffn_conv,
            s_swa_k, s_swa_v, s_lru_conv, s_lru_h, s_ffn_conv)
```

```python
import functools

import numpy as np
import jax
import jax.numpy as jnp
from jax import lax
from jax.experimental import pallas as pl
from jax.experimental.pallas import tpu as pltpu

D_MODEL = 1024
D_LRU = 512
LRU_BLOCKS = 8
LRU_BLOCK = 64
LRU_CONV_W = 4
LRU_C = 8.0
N_Q_HEADS = 8
N_KV_HEADS = 2
HEAD_DIM = 64
D_ATTN = 512
D_KV = 128
WINDOW = 128
D_IN = 1792
MEM_LEN = 256
N_MEM_HEADS = 4
MEM_HEAD_DIM = 128
D_MEM = 512
D_FF = 3072
FFN_CONV_W = 3
EPS = 1e-6
NEG_INF = -1e30

F32 = jnp.float32
BF16 = jnp.bfloat16

SLOPES = [float(2.0 ** (-8.0 * (i + 1) / N_Q_HEADS)) for i in range(N_Q_HEADS)]
ATTN_SCALE = HEAD_DIM ** -0.5
MEM_SCALE = MEM_HEAD_DIM ** -0.5
LOG2E = 1.4426950408889634
F32_TINY = 1.1754944e-38

TM = 1024
HR = 512
SEG = HR // 8
TF = 512
FF_CHUNK = 512
S_NB = 16
S_ROWS = 4 * S_NB
C_NB = 16
C_ROWS = 4 * C_NB
VMEM_LIMIT = 56 * 1024 * 1024


def _dot(a, b):
    return jnp.dot(a, b, preferred_element_type=F32)


def _dot_nt(a, b):
    return lax.dot_general(a, b, (((1,), (1,)), ((), ())), preferred_element_type=F32)


def _rmsnorm(x, g):
    ms = jnp.mean(x * x, axis=-1, keepdims=True)
    return x * lax.rsqrt(ms + EPS) * g


def _gelu(x):
    c = 0.7978845608028654
    return x * (0.5 * (1.0 + jnp.tanh(c * (x + 0.044715 * (x * x * x)))))


def _sigmoid(x):
    return 1.0 / (1.0 + jnp.exp(-x))


def _softplus(x):
    return jnp.maximum(x, 0.0) + jnp.log1p(jnp.exp(-jnp.abs(x)))


def _lru_gates(xc, wax_ref, ba, bx, lam):
    xcb = xc.astype(BF16)
    pa, px = [], []
    for gi in range(2):
        pre = _dot(xcb[:, gi * 256:(gi + 1) * 256], wax_ref[gi])
        pa.append(pre[:, :256])
        px.append(pre[:, 256:])
    r = _sigmoid(jnp.concatenate(pa, axis=1) + ba)
    i = _sigmoid(jnp.concatenate(px, axis=1) + bx)
    log_a = (-LRU_C * _softplus(-lam)) * r
    a = jnp.exp(log_a)
    om = -jnp.tanh(log_a) * (a * a + 1.0)
    b = (om * lax.rsqrt(jnp.maximum(om, F32_TINY))) * (i * xc)
    return a, b


def _head_variants(t):
    lo = lax.broadcasted_iota(jnp.int32, t.shape, 1) < HEAD_DIM
    tr = pltpu.roll(t, HEAD_DIM, 1)
    z = jnp.zeros_like(t)
    x0 = jnp.where(lo, t, z)
    y0 = jnp.where(lo, z, tr)
    x1 = jnp.where(lo, tr, z)
    y1 = jnp.where(lo, z, t)
    return [v.astype(BF16) for v in (x0, y0, x1, y1)]


def _mem_kv_kernel(mem_ref, g_ref, wk_ref, wv_ref, mk_ref, mv_ref, mkb_ref, mvb_ref):
    n = _rmsnorm(mem_ref[0], g_ref[...]).astype(BF16)
    mk = _dot(n, wk_ref[...])
    mv = _dot(n, wv_ref[...])
    mk_ref[0] = mk
    mv_ref[0] = mv
    mkb_ref[0] = mk.astype(BF16)
    mvb_ref[0] = mv.astype(BF16)


def _prompt_mixer_kernel(sinks_ref, x_ref, g_ref, win_ref, wconv_ref, bconv_ref, wax_ref, ba_ref, bx_ref,
                         lam_ref, wout_ref,
                         x1_ref, pk_ref, pv_ref, pconv_ref, ph_ref,
                         proj_s, pin_s, pout_s, xtail_s, a_s, b_s, hl_s, h_s, hc_s, kv_s, ymix_s):
    t = pl.program_id(1)

    @pl.when(t == 0)
    def _():
        xtail_s[...] = jnp.zeros((24, D_LRU), F32)
        hc_s[...] = jnp.zeros((8, D_LRU), F32)
        kv_s[:, 0:WINDOW, :] = jnp.zeros((8, WINDOW, D_KV), BF16)

    def sec(i):
        return slice(i * HR, (i + 1) * HR)

    def in_proj(i):
        n1 = _rmsnorm(x_ref[0, sec(i), :], g_ref[...]).astype(BF16)
        proj_s[sec(i), :] = _dot(n1, win_ref[...])

    def seg_rows(j):
        return pl.ds(SEG * (j % 8) + j // 8, 8, stride=8)

    row8 = lax.broadcasted_iota(jnp.int32, (8, D_LRU), 0)

    def lru_gates(i):
        r0 = i * HR
        for j in range(HR // 8):
            for l in range(D_LRU // 128):
                pin_s[l, seg_rows(j), :] = proj_s[r0 + 8 * j:r0 + 8 * j + 8, l * 128:(l + 1) * 128]
        xs = jnp.concatenate([pin_s[l] for l in range(D_LRU // 128)], axis=1)
        tail = xs[HR - 24:, :]
        prev = xtail_s[...]
        xtail_s[...] = tail
        heads = [jnp.where(row8 == 0, pltpu.roll(prev[8 * q:8 * q + 8, :], 1, 0),
                           pltpu.roll(tail[8 * q:8 * q + 8, :], 1, 0)) for q in range(3)]
        sh1 = jnp.concatenate(heads[2:] + [xs[:HR - 8, :]], axis=0)
        sh2 = jnp.concatenate(heads[1:] + [xs[:HR - 16, :]], axis=0)
        sh3 = jnp.concatenate(heads + [xs[:HR - 24, :]], axis=0)
        wc = wconv_ref[...]
        xc = bconv_ref[...] + wc[0:1] * sh3 + wc[1:2] * sh2 + wc[2:3] * sh1 + wc[3:4] * xs
        a, b = _lru_gates(xc, wax_ref, ba_ref[...], bx_ref[...], lam_ref[...])
        a_s[sec(i), :] = a
        b_s[sec(i), :] = b

    def lru_scan(i, hcar):
        r0 = i * HR
        hl = b_s[r0:r0 + 8, :]
        ac = a_s[r0:r0 + 8, :]
        hl_s[0:8, :] = hl
        for g in range(1, SEG):
            av = a_s[r0 + 8 * g:r0 + 8 * g + 8, :]
            hl = av * hl + b_s[r0 + 8 * g:r0 + 8 * g + 8, :]
            ac = av * ac
            hl_s[8 * g:8 * g + 8, :] = hl
            a_s[r0 + 8 * g:r0 + 8 * g + 8, :] = ac
        hin = hcar
        for s in range(8):
            hend = hl + ac * hin
            if s < 7:
                hin = jnp.where(row8 == s + 1, pltpu.roll(hend, 1, 0), hin)
        hcar = jnp.broadcast_to(hend[7:8, :], (8, D_LRU))
        for g in range(SEG):
            hg = hl_s[8 * g:8 * g + 8, :] + a_s[r0 + 8 * g:r0 + 8 * g + 8, :] * hin
            for l in range(D_LRU // 128):
                pout_s[l, 8 * g:8 * g + 8, :] = hg[:, l * 128:(l + 1) * 128]
        for j in range(HR // 8):
            h_s[r0 + 8 * j:r0 + 8 * j + 8, :] = jnp.concatenate(
                [pout_s[l, seg_rows(j), :] for l in range(D_LRU // 128)], axis=1)
        gate = proj_s[sec(i), D_LRU:2 * D_LRU]
        ymix_s[sec(i), 0:D_LRU] = (h_s[sec(i), :] * _gelu(gate)).astype(BF16)
        return hcar

    def kv_prep(i):
        k = proj_s[sec(i), 1536:1664]
        v = proj_s[sec(i), 1664:1792]
        for n, arr in enumerate(_head_variants(k) + _head_variants(v)):
            kv_s[n, WINDOW + i * HR:WINDOW + (i + 1) * HR, :] = arr

    qi = lax.broadcasted_iota(jnp.int32, (WINDOW, 2 * WINDOW), 0)
    kj = lax.broadcasted_iota(jnp.int32, (WINDOW, 2 * WINDOW), 1)
    dist = qi + WINDOW - kj
    valid = (dist >= 0) & (dist < WINDOW)
    valid0 = valid & (kj >= jnp.where(t == 0, WINDOW, 0))
    distf = dist.astype(F32)
    bias = [jnp.where(valid, -SLOPES[h] * distf, NEG_INF) for h in range(N_Q_HEADS)]
    bias0 = [jnp.where(valid0, -SLOPES[h] * distf, NEG_INF) for h in range(N_Q_HEADS)]
    lane_lo = lax.broadcasted_iota(jnp.int32, (WINDOW, 2 * HEAD_DIM), 1) < HEAD_DIM

    r512 = lax.broadcasted_iota(jnp.int32, (4 * WINDOW, 2 * HEAD_DIM), 0)
    l512 = lax.broadcasted_iota(jnp.int32, (4 * WINDOW, 2 * HEAD_DIM), 1)
    ones_cols = jnp.where((r512 < 2 * WINDOW) == (l512 < HEAD_DIM), 1.0, 0.0).astype(BF16)
    n_iter = (HR // WINDOW) * 4

    def qk(n):
        j, c = divmod(n, 4)
        hk = c // 2
        rows = slice(j * WINDOW, (j + 1) * WINDOW)
        win = slice(j * WINDOW, (j + 2) * WINDOW)
        qc = (proj_s[rows, 1024 + c * 128:1024 + (c + 1) * 128] * ATTN_SCALE).astype(BF16)
        kcat = jnp.concatenate([kv_s[2 * hk, win, :], kv_s[2 * hk + 1, win, :]], axis=0)
        return _dot_nt(qc, kcat)

    def softmax_pv(n, s):
        j, c = divmod(n, 4)
        hk = c // 2
        rows = slice(j * WINDOW, (j + 1) * WINDOW)
        win = slice(j * WINDOW, (j + 2) * WINDOW)
        vcat = jnp.concatenate([kv_s[4 + 2 * hk, win, :], kv_s[5 + 2 * hk, win, :]], axis=0)
        vaug = jnp.concatenate([vcat, ones_cols], axis=1)
        ps, es = [], []
        for half in range(2):
            h = 2 * c + half
            sink = sinks_ref[h]
            sh = s[:, half * 256:(half + 1) * 256] + (bias0[h] if j == 0 else bias[h])
            m = jnp.maximum(jnp.max(sh, axis=-1, keepdims=True), sink)
            ps.append(jnp.exp(sh - m).astype(BF16))
            es.append(jnp.exp(sink - m))
        oa = _dot(jnp.concatenate(ps, axis=1), vaug)
        den = oa[:, 128:256] + jnp.where(lane_lo, es[0], es[1])
        ymix_s[rows, D_LRU + c * 128:D_LRU + (c + 1) * 128] = (oa[:, 0:128] * (1.0 / den)).astype(BF16)

    def attention(i):
        depth = 2
        base = i * n_iter
        pend = [qk(base + n) for n in range(depth)]
        for n in range(n_iter):
            s = pend.pop(0)
            if n + depth < n_iter:
                pend.append(qk(base + n + depth))
            softmax_pv(base + n, s)

    def out_proj(i):
        x1_ref[0, sec(i), :] = x_ref[0, sec(i), :] + _dot(ymix_s[sec(i), :], wout_ref[...])

    n_sec = TM // HR
    h = hc_s[...]
    in_proj(0)
    lru_gates(0)
    for i in range(n_sec):
        if i + 1 < n_sec:
            in_proj(i + 1)
        h = lru_scan(i, h)
        kv_prep(i)
        attention(i)
        if i + 1 < n_sec:
            lru_gates(i + 1)
        out_proj(i)

    hc_s[...] = h
    ph_ref[0] = h
    pconv_ref[0] = proj_s[TM - 8:, 0:D_LRU]
    pk_ref[0] = proj_s[TM - WINDOW:, 1536:1664]
    pv_ref[0] = proj_s[TM - WINDOW:, 1664:1792]
    kv_s[:, 0:WINDOW, :] = kv_s[:, TM:TM + WINDOW, :]


def _ffn_chunks(n3, acc, gprev_fn, wg_ref, wu_ref, wfc_ref, bfc_ref, wd_ref, on_gate):
    nc = D_FF // FF_CHUNK

    def up(c):
        cs = slice(c * FF_CHUNK, (c + 1) * FF_CHUNK)
        return _dot(n3, wg_ref[:, cs]), _dot(n3, wu_ref[:, cs])

    nxt = up(0)
    for c in range(nc):
        cs = slice(c * FF_CHUNK, (c + 1) * FF_CHUNK)
        g, u = nxt
        if c + 1 < nc:
            nxt = up(c + 1)
        g2, g1 = gprev_fn(c, g)
        on_gate(c, g)
        wfc = wfc_ref[:, cs]
        conv = bfc_ref[:, cs] + wfc[0:1] * g2 + wfc[1:2] * g1 + wfc[2:3] * g
        hmid = (_gelu(conv) * u).astype(BF16)
        acc = acc + _dot(hmid, wd_ref[cs, :])
    return acc


def _prompt_ffn_kernel(x1_ref, mk_ref, mv_ref, gc_ref, wq_ref, wo_ref, gf_ref, wg_ref, wu_ref, wfc_ref, bfc_ref,
                       wd_ref, gfin_ref,
                       y_ref, pffn_ref,
                       oc_s, gbuf_s, gcar_s):
    t = pl.program_id(1)

    @pl.when(t == 0)
    def _():
        gcar_s[...] = jnp.zeros((8, D_FF), F32)

    x1 = x1_ref[0]
    qc = _dot(_rmsnorm(x1, gc_ref[...]).astype(BF16), wq_ref[...]).astype(BF16)
    hsl = [slice(h * MEM_HEAD_DIM, (h + 1) * MEM_HEAD_DIM) for h in range(N_MEM_HEADS)]
    ss = [_dot_nt(qc[:, hs], mk_ref[0, :, hs]) for hs in hsl]
    for h, hs in enumerate(hsl):
        s = ss[h]
        m = jnp.max(s, axis=-1, keepdims=True)
        p = jnp.exp2((s - m) * (MEM_SCALE * LOG2E))
        l = jnp.sum(p, axis=-1, keepdims=True)
        o = _dot(p.astype(BF16), mv_ref[0, :, hs]) * (1.0 / l)
        oc_s[:, hs] = o.astype(BF16)
    x2 = x1 + _dot(oc_s[...], wo_ref[...])
    n3 = _rmsnorm(x2, gf_ref[...]).astype(BF16)

    def gprev(c, g):
        cs = slice(c * FF_CHUNK, (c + 1) * FF_CHUNK)
        buf = gbuf_s.at[c % 2]
        buf[0:8, :] = gcar_s[:, cs]
        buf[8:TF + 8, :] = g
        return buf[6:6 + TF, :], buf[7:7 + TF, :]

    def on_gate(c, g):
        cs = slice(c * FF_CHUNK, (c + 1) * FF_CHUNK)
        tail = g[TF - 8:, :]
        gcar_s[:, cs] = tail
        pffn_ref[0, :, cs] = tail

    acc = _ffn_chunks(n3, x2, gprev, wg_ref, wu_ref, wfc_ref, bfc_ref, wd_ref, on_gate)
    y_ref[0] = _rmsnorm(acc, gfin_ref[...])


def _sample_mixer_kernel(sinks_ref, x_ref, prev4_ref, h0_ref, ck_ref, cv_ref, g_ref, win_ref, wconv_ref, bconv_ref,
                         wax_ref, ba_ref, bx_ref, lam_ref, wout_ref,
                         x1_ref, sk_ref, sv_ref, xr_ref, h_ref,
                         q_s, k_s, v_s, kall_s, vall_s, yatt_s):
    R = S_ROWS

    @pl.when(pl.program_id(0) == 0)
    def _():
        kall_s[...] = jnp.zeros(kall_s.shape, F32)
        vall_s[...] = jnp.zeros(vall_s.shape, F32)

    x = x_ref[...]
    n1 = _rmsnorm(x, g_ref[...]).astype(BF16)
    proj = _dot(n1, win_ref[...])
    xr = proj[:, 0:D_LRU]
    gate = proj[:, D_LRU:2 * D_LRU]
    q_s[...] = proj[:, 1024:1536] * ATTN_SCALE
    k_s[...] = proj[:, 1536:1664]
    v_s[...] = proj[:, 1664:1792]
    xr_ref[...] = xr

    tmod = lax.broadcasted_iota(jnp.int32, (R, D_LRU), 0) & 3
    prev4 = prev4_ref[...]
    xs1 = jnp.where(tmod >= 1, pltpu.roll(xr, 1, 0), pltpu.roll(prev4, R - 2, 0))
    xs2 = jnp.where(tmod >= 2, pltpu.roll(xr, 2, 0), pltpu.roll(prev4, R - 1, 0))
    xs3 = jnp.where(tmod >= 3, pltpu.roll(xr, 3, 0), prev4)
    wc = wconv_ref[...]
    xc = bconv_ref[...] + wc[0:1] * xs3 + wc[1:2] * xs2 + wc[2:3] * xs1 + wc[3:4] * xr

    a, b = _lru_gates(xc, wax_ref, ba_ref[...], bx_ref[...], lam_ref[...])
    for s in (1, 2):
        m = tmod >= s
        a_sh = pltpu.roll(a, s, 0)
        b_sh = pltpu.roll(b, s, 0)
        b = jnp.where(m, a * b_sh + b, b)
        a = jnp.where(m, a * a_sh, a)
    h = a * h0_ref[...] + b
    h_ref[...] = h
    y_lru = h * _gelu(gate)

    r16 = lax.broadcasted_iota(jnp.int32, (16, 512), 0)
    c16 = lax.broadcasted_iota(jnp.int32, (16, 512), 1)
    dist = (r16 & 3) + WINDOW - (c16 & 255)
    valid = (dist >= 0) & (dist < WINDOW)
    distf = dist.astype(F32)
    first8 = r16 < 8
    lowhalf = c16 < 256
    biases, sinkcols = [], []
    for hk in range(N_KV_HEADS):
        hd = [[2 * (2 * hk + c) + half for half in range(2)] for c in range(2)]
        slope = jnp.where(first8,
                          jnp.where(lowhalf, SLOPES[hd[0][0]], SLOPES[hd[0][1]]),
                          jnp.where(lowhalf, SLOPES[hd[1][0]], SLOPES[hd[1][1]]))
        biases.append(jnp.where(valid, -slope * distf, NEG_INF))
        sinkcols.append([jnp.where(first8[:, 0:1], sinks_ref[hd[0][half]], sinks_ref[hd[1][half]])
                         for half in range(2)])
    bat0_16 = (r16 & 7) < 4
    r8 = lax.broadcasted_iota(jnp.int32, (8, D_KV), 0)
    lane_lo = lax.broadcasted_iota(jnp.int32, (16, D_KV), 1) < HEAD_DIM

    def pair_scores(p):
        r0 = p * 8
        q8 = q_s[r0:r0 + 8, :]
        k8 = k_s[r0:r0 + 8, :]
        v8 = v_s[r0:r0 + 8, :]
        kvar, vvar = [], []
        for bb in range(2):
            bidx = 2 * p + bb
            slot = 2 * (p % 2) + bb
            if bb == 0:
                kn, vn = k8, v8
            else:
                kn, vn = pltpu.roll(k8, 4, 0), pltpu.roll(v8, 4, 0)
            kall_s[slot, 0:WINDOW, :] = ck_ref[bidx]
            vall_s[slot, 0:WINDOW, :] = cv_ref[bidx]
            kall_s[slot, WINDOW:WINDOW + 8, :] = jnp.where(r8 < 4, kn, 0.0)
            vall_s[slot, WINDOW:WINDOW + 8, :] = jnp.where(r8 < 4, vn, 0.0)
            sk_ref[bidx] = kall_s[slot, pl.ds(4, WINDOW), :]
            sv_ref[bidx] = vall_s[slot, pl.ds(4, WINDOW), :]
            kvar.append(_head_variants(kall_s[slot]))
            vvar.append(_head_variants(vall_s[slot]))
        ss = []
        for hk in range(N_KV_HEADS):
            c0 = 2 * hk
            lhs = jnp.concatenate([q8[:, c0 * 128:(c0 + 1) * 128], q8[:, (c0 + 1) * 128:(c0 + 2) * 128]],
                                  axis=0).astype(BF16)
            sb = [_dot_nt(lhs, jnp.concatenate([kvar[bb][2 * hk], kvar[bb][2 * hk + 1]], axis=0))
                  for bb in range(2)]
            ss.append(jnp.where(bat0_16, sb[0], sb[1]) + biases[hk])
        return ss, vvar

    def pair_finish(p, ss, vvar):
        r0 = p * 8
        for hk in range(N_KV_HEADS):
            c0 = 2 * hk
            s = ss[hk]
            ps, invs = [], []
            for half in range(2):
                sh = s[:, half * 256:(half + 1) * 256]
                sink = sinkcols[hk][half]
                m = jnp.maximum(jnp.max(sh, axis=-1, keepdims=True), sink)
                pe = jnp.exp(sh - m)
                l = jnp.sum(pe, axis=-1, keepdims=True) + jnp.exp(sink - m)
                ps.append(pe.astype(BF16))
                invs.append(1.0 / l)
            pcat = jnp.concatenate(ps, axis=1)
            ob = [_dot(pcat, jnp.concatenate([vvar[bb][2 * hk], vvar[bb][2 * hk + 1]], axis=0))
                  for bb in range(2)]
            o = jnp.where(bat0_16[:, 0:D_KV], ob[0], ob[1]) * jnp.where(lane_lo, invs[0], invs[1])
            yatt_s[r0:r0 + 8, c0 * 128:(c0 + 1) * 128] = o[0:8]
            yatt_s[r0:r0 + 8, (c0 + 1) * 128:(c0 + 2) * 128] = o[8:16]

    n_pairs = S_NB // 2
    cur = pair_scores(0)
    for p in range(n_pairs):
        nxt = pair_scores(p + 1) if p + 1 < n_pairs else None
        pair_finish(p, *cur)
        cur = nxt

    ymix = jnp.concatenate([y_lru, yatt_s[...]], axis=1).astype(BF16)
    x1_ref[...] = x + _dot(ymix, wout_ref[...])


def _load_mem_heads(ref, b):
    return jnp.concatenate([ref[b, pl.ds(h, MEM_LEN, stride=N_MEM_HEADS), :] for h in range(N_MEM_HEADS)],
                           axis=1).astype(BF16)


def _sample_cross_kernel(x1_ref, mk_ref, mv_ref, gc_ref, wq_ref, wo_ref, x2_ref, q_s, oc_s):
    x1 = x1_ref[...]
    q_s[...] = _dot(_rmsnorm(x1, gc_ref[...]).astype(BF16), wq_ref[...])

    lane_head = lax.broadcasted_iota(jnp.int32, (8, D_MEM), 1) // MEM_HEAD_DIM
    bat0 = (lax.broadcasted_iota(jnp.int32, (32, D_MEM), 0) & 7) < 4
    bat0_s = (lax.broadcasted_iota(jnp.int32, (32, MEM_LEN), 0) & 7) < 4

    def scores(p):
        q8 = q_s[p * 8:(p + 1) * 8, :]
        lhs = jnp.concatenate([jnp.where(lane_head == h, q8, 0.0) for h in range(N_MEM_HEADS)],
                              axis=0).astype(BF16)
        sb = [_dot_nt(lhs, _load_mem_heads(mk_ref, 2 * p + bb)) for bb in range(2)]
        return jnp.where(bat0_s, sb[0], sb[1]) * MEM_SCALE

    n_pairs = C_NB // 2
    s_next = scores(0)
    for p in range(n_pairs):
        s = s_next
        if p + 1 < n_pairs:
            s_next = scores(p + 1)
        m = jnp.max(s, axis=-1, keepdims=True)
        pe = jnp.exp(s - m)
        l = jnp.sum(pe, axis=-1, keepdims=True)
        pb = pe.astype(BF16)
        ob = [_dot(pb, _load_mem_heads(mv_ref, 2 * p + bb)) for bb in range(2)]
        o = jnp.where(bat0, ob[0], ob[1]) * (1.0 / l)
        out = jnp.where(lane_head == 0, o[0:8], 0.0)
        for h in range(1, N_MEM_HEADS):
            out = out + jnp.where(lane_head == h, o[h * 8:(h + 1) * 8], 0.0)
        oc_s[p * 8:(p + 1) * 8, :] = out

    x2_ref[...] = x1 + _dot(oc_s[...].astype(BF16), wo_ref[...])


def _sample_ffn_kernel(x2_ref, prev4_ref, gf_ref, wg_ref, wu_ref, wfc_ref, bfc_ref, wd_ref, gfin_ref,
                       y_ref, gate_ref):
    R = x2_ref.shape[0]
    x2 = x2_ref[...]
    n3 = _rmsnorm(x2, gf_ref[...]).astype(BF16)
    tmod = lax.broadcasted_iota(jnp.int32, (R, FF_CHUNK), 0) & 3

    def gprev(c, g):
        cs = slice(c * FF_CHUNK, (c + 1) * FF_CHUNK)
        prev4 = prev4_ref[:, cs]
        g1 = jnp.where(tmod >= 1, pltpu.roll(g, 1, 0), pltpu.roll(prev4, R - 1, 0))
        g2 = jnp.where(tmod >= 2, pltpu.roll(g, 2, 0), prev4)
        return g2, g1

    def on_gate(c, g):
        gate_ref[:, c * FF_CHUNK:(c + 1) * FF_CHUNK] = g

    acc = _ffn_chunks(n3, x2, gprev, wg_ref, wu_ref, wfc_ref, bfc_ref, wd_ref, on_gate)
    y_ref[...] = _rmsnorm(acc, gfin_ref[...])


def _const_spec(shape, grid_rank):
    zeros = (0,) * len(shape)
    if grid_rank == 1:
        return pl.BlockSpec(shape, lambda i: zeros, pipeline_mode=pl.Buffered(1))
    return pl.BlockSpec(shape, lambda i, j: zeros, pipeline_mode=pl.Buffered(1))


def _block_diag4(w):
    eye = jnp.eye(4, dtype=w.dtype)
    return (w[:, :, None, :] * eye[:, None, :, None]).reshape(4 * LRU_BLOCK, 4 * LRU_BLOCK)


def _params(sem):
    return pltpu.CompilerParams(dimension_semantics=sem, vmem_limit_bytes=VMEM_LIMIT)


def kernel(x_prompt, x_sample, cache_swa_k, cache_swa_v, cache_mem_k, cache_mem_v, state_lru_conv, state_lru_h, state_ffn_conv, mem_prompt, g_mix, w_in, w_lru_conv, b_lru_conv, w_lru_a, b_lru_a, w_lru_x, b_lru_x, lru_lambda, attn_sinks, w_out, g_cross, g_mem, w_mem_q, w_mem_k, w_mem_v, w_mem_o, g_ffn, w_ffn_gate, w_ffn_up, w_ffn_conv, b_ffn_conv, w_ffn_down, g_final):
    B, T, _ = x_prompt.shape
    NB = x_sample.shape[0]
    NT = T // TM
    SR = NB * 4

    win = w_in[0].astype(BF16)
    wout = w_out[0].astype(BF16)
    wq = w_mem_q[0].astype(BF16)
    wk = w_mem_k[0].astype(BF16)
    wv = w_mem_v[0].astype(BF16)
    wo = w_mem_o[0].astype(BF16)
    wg = w_ffn_gate[0].astype(BF16)
    wu = w_ffn_up[0].astype(BF16)
    wd = w_ffn_down[0].astype(BF16)
    wax = jnp.stack([
        jnp.concatenate([_block_diag4(w_lru_a[0, 4 * gi:4 * gi + 4]), _block_diag4(w_lru_x[0, 4 * gi:4 * gi + 4])],
                        axis=1) for gi in range(2)]).astype(BF16)
    gmix, gcross, gmem, gffn = g_mix, g_cross, g_mem, g_ffn
    gfin = g_final.reshape(1, D_MODEL)
    wconv, bconv = w_lru_conv[0], b_lru_conv
    ba, bx, lam = b_lru_a, b_lru_x, lru_lambda
    wfc, bfc = w_ffn_conv[0], b_ffn_conv
    sinks = attn_sinks[0]
    smem = pl.BlockSpec(memory_space=pltpu.SMEM)

    mk, mv, mkb, mvb = pl.pallas_call(
        _mem_kv_kernel,
        grid=(B,),
        in_specs=[pl.BlockSpec((1, MEM_LEN, D_MODEL), lambda b: (b, 0, 0)),
                  _const_spec((1, D_MODEL), 1), _const_spec((D_MODEL, D_MEM), 1), _const_spec((D_MODEL, D_MEM), 1)],
        out_specs=[pl.BlockSpec((1, MEM_LEN, D_MEM), lambda b: (b, 0, 0))] * 4,
        out_shape=[jax.ShapeDtypeStruct((B, MEM_LEN, D_MEM), F32)] * 2
        + [jax.ShapeDtypeStruct((B, MEM_LEN, D_MEM), BF16)] * 2,
        compiler_params=_params(("arbitrary",)),
        name="mem_kv",
    )(mem_prompt, gmem, wk, wv)

    x1p, pk, pv, pconv8, ph8 = pl.pallas_call(
        _prompt_mixer_kernel,
        grid=(B, NT),
        in_specs=[smem,
                  pl.BlockSpec((1, TM, D_MODEL), lambda b, t: (b, t, 0)),
                  _const_spec((1, D_MODEL), 2), _const_spec((D_MODEL, D_IN), 2),
                  _const_spec((LRU_CONV_W, D_LRU), 2), _const_spec((1, D_LRU), 2),
                  _const_spec((2, 256, 512), 2), _const_spec((1, D_LRU), 2), _const_spec((1, D_LRU), 2),
                  _const_spec((1, D_LRU), 2), _const_spec((D_MODEL, D_MODEL), 2)],
        out_specs=[pl.BlockSpec((1, TM, D_MODEL), lambda b, t: (b, t, 0)),
                   pl.BlockSpec((1, WINDOW, D_KV), lambda b, t: (b, 0, 0)),
                   pl.BlockSpec((1, WINDOW, D_KV), lambda b, t: (b, 0, 0)),
                   pl.BlockSpec((1, 8, D_LRU), lambda b, t: (b, 0, 0)),
                   pl.BlockSpec((1, 8, D_LRU), lambda b, t: (b, 0, 0))],
        out_shape=[jax.ShapeDtypeStruct((B, T, D_MODEL), F32),
                   jax.ShapeDtypeStruct((B, WINDOW, D_KV), F32),
                   jax.ShapeDtypeStruct((B, WINDOW, D_KV), F32),
                   jax.ShapeDtypeStruct((B, 8, D_LRU), F32),
                   jax.ShapeDtypeStruct((B, 8, D_LRU), F32)],
        scratch_shapes=[pltpu.VMEM((TM, D_IN), F32),
                        pltpu.VMEM((D_LRU // 128, HR, 128), F32),
                        pltpu.VMEM((D_LRU // 128, HR, 128), F32),
                        pltpu.VMEM((24, D_LRU), F32),
                        pltpu.VMEM((TM, D_LRU), F32),
                        pltpu.VMEM((TM, D_LRU), F32),
                        pltpu.VMEM((HR, D_LRU), F32),
                        pltpu.VMEM((TM, D_LRU), F32),
                        pltpu.VMEM((8, D_LRU), F32),
                        pltpu.VMEM((8, TM + WINDOW, D_KV), BF16),
                        pltpu.VMEM((TM, D_MODEL), BF16)],
        compiler_params=_params(("arbitrary", "arbitrary")),
        name="prompt_mixer",
    )(sinks, x_prompt, gmix, win, wconv, bconv, wax, ba, bx, lam, wout)

    y_prompt, pffn8 = pl.pallas_call(
        _prompt_ffn_kernel,
        grid=(B, T // TF),
        in_specs=[pl.BlockSpec((1, TF, D_MODEL), lambda b, t: (b, t, 0)),
                  pl.BlockSpec((1, MEM_LEN, D_MEM), lambda b, t: (b, 0, 0)),
                  pl.BlockSpec((1, MEM_LEN, D_MEM), lambda b, t: (b, 0, 0)),
                  _const_spec((1, D_MODEL), 2), _const_spec((D_MODEL, D_MEM), 2), _const_spec((D_MEM, D_MODEL), 2),
                  _const_spec((1, D_MODEL), 2), _const_spec((D_MODEL, D_FF), 2), _const_spec((D_MODEL, D_FF), 2),
                  _const_spec((FFN_CONV_W, D_FF), 2), _const_spec((1, D_FF), 2), _const_spec((D_FF, D_MODEL), 2),
                  _const_spec((1, D_MODEL), 2)],
        out_specs=[pl.BlockSpec((1, TF, D_MODEL), lambda b, t: (b, t, 0)),
                   pl.BlockSpec((1, 8, D_FF), lambda b, t: (b, 0, 0))],
        out_shape=[jax.ShapeDtypeStruct((B, T, D_MODEL), F32),
                   jax.ShapeDtypeStruct((B, 8, D_FF), F32)],
        scratch_shapes=[pltpu.VMEM((TF, D_MEM), BF16),
                        pltpu.VMEM((2, TF + 8, FF_CHUNK), F32),
                        pltpu.VMEM((8, D_FF), F32)],
        compiler_params=_params(("arbitrary", "arbitrary")),
        name="prompt_ffn",
    )(x1p, mkb, mvb, gcross, wq, wo, gffn, wg, wu, wfc, bfc, wd, gfin)

    xs = x_sample.reshape(SR, D_MODEL)
    conv_prev4 = jnp.pad(state_lru_conv[0], ((0, 0), (0, 1), (0, 0))).reshape(SR, D_LRU)
    h0rep = jnp.repeat(state_lru_h[0], 4, axis=0)
    ck = cache_swa_k[0].reshape(NB, WINDOW, D_KV)
    cv = cache_swa_v[0].reshape(NB, WINDOW, D_KV)
    row_spec = lambda w: pl.BlockSpec((S_ROWS, w), lambda i: (i, 0))
    cache_spec = pl.BlockSpec((S_NB, WINDOW, D_KV), lambda i: (i, 0, 0))
    x1s, sk, sv, xr_s, h_s = pl.pallas_call(
        _sample_mixer_kernel,
        grid=(NB // S_NB,),
        in_specs=[smem, row_spec(D_MODEL), row_spec(D_LRU), row_spec(D_LRU), cache_spec, cache_spec,
                  _const_spec((1, D_MODEL), 1), _const_spec((D_MODEL, D_IN), 1),
                  _const_spec((LRU_CONV_W, D_LRU), 1), _const_spec((1, D_LRU), 1),
                  _const_spec((2, 256, 512), 1), _const_spec((1, D_LRU), 1), _const_spec((1, D_LRU), 1),
                  _const_spec((1, D_LRU), 1), _const_spec((D_MODEL, D_MODEL), 1)],
        out_specs=[row_spec(D_MODEL), cache_spec, cache_spec, row_spec(D_LRU), row_spec(D_LRU)],
        out_shape=[jax.ShapeDtypeStruct((SR, D_MODEL), F32),
                   jax.ShapeDtypeStruct((NB, WINDOW, D_KV), F32),
                   jax.ShapeDtypeStruct((NB, WINDOW, D_KV), F32),
                   jax.ShapeDtypeStruct((SR, D_LRU), F32),
                   jax.ShapeDtypeStruct((SR, D_LRU), F32)],
        scratch_shapes=[pltpu.VMEM((S_ROWS, D_ATTN), F32),
                        pltpu.VMEM((S_ROWS, D_KV), F32),
                        pltpu.VMEM((S_ROWS, D_KV), F32),
                        pltpu.VMEM((4, 2 * WINDOW, D_KV), F32),
                        pltpu.VMEM((4, 2 * WINDOW, D_KV), F32),
                        pltpu.VMEM((S_ROWS, D_ATTN), F32)],
        compiler_params=_params(("arbitrary",)),
        name="sample_mixer",
    )(sinks, xs, conv_prev4, h0rep, ck, cv, gmix, win, wconv, bconv, wax, ba, bx, lam, wout)

    cmk = cache_mem_k.reshape(NB, MEM_LEN * N_MEM_HEADS, MEM_HEAD_DIM)
    cmv = cache_mem_v.reshape(NB, MEM_LEN * N_MEM_HEADS, MEM_HEAD_DIM)
    crow = pl.BlockSpec((C_ROWS, D_MODEL), lambda i: (i, 0))
    cmem = pl.BlockSpec((C_NB, MEM_LEN * N_MEM_HEADS, MEM_HEAD_DIM), lambda i: (i, 0, 0))
    x2s = pl.pallas_call(
        _sample_cross_kernel,
        grid=(NB // C_NB,),
        in_specs=[crow, cmem, cmem, _const_spec((1, D_MODEL), 1), _const_spec((D_MODEL, D_MEM), 1),
                  _const_spec((D_MEM, D_MODEL), 1)],
        out_specs=crow,
        out_shape=jax.ShapeDtypeStruct((SR, D_MODEL), F32),
        scratch_shapes=[pltpu.VMEM((C_ROWS, D_MEM), F32), pltpu.VMEM((C_ROWS, D_MEM), F32)],
        compiler_params=_params(("arbitrary",)),
        name="sample_cross",
    )(x1s, cmk, cmv, gcross, wq, wo)

    ffn_prev4 = jnp.pad(state_ffn_conv[0], ((0, 0), (0, 2), (0, 0))).reshape(SR, D_FF)
    y_s, gate_s = pl.pallas_call(
        _sample_ffn_kernel,
        grid=(1,),
        in_specs=[_const_spec((SR, D_MODEL), 1), _const_spec((SR, D_FF), 1),
                  _const_spec((1, D_MODEL), 1), _const_spec((D_MODEL, D_FF), 1), _const_spec((D_MODEL, D_FF), 1),
                  _const_spec((FFN_CONV_W, D_FF), 1), _const_spec((1, D_FF), 1), _const_spec((D_FF, D_MODEL), 1),
                  _const_spec((1, D_MODEL), 1)],
        out_specs=[pl.BlockSpec((SR, D_MODEL), lambda i: (0, 0)), pl.BlockSpec((SR, D_FF), lambda i: (0, 0))],
        out_shape=[jax.ShapeDtypeStruct((SR, D_MODEL), F32), jax.ShapeDtypeStruct((SR, D_FF), F32)],
        compiler_params=_params(("arbitrary",)),
        name="sample_ffn",
    )(x2s, ffn_prev4, gffn, wg, wu, wfc, bfc, wd, gfin)

    p_swa_k = pk.reshape(1, B, WINDOW, N_KV_HEADS, HEAD_DIM)
    p_swa_v = pv.reshape(1, B, WINDOW, N_KV_HEADS, HEAD_DIM)
    p_mem_k = mk.reshape(1, B, MEM_LEN, N_MEM_HEADS, MEM_HEAD_DIM)
    p_mem_v = mv.reshape(1, B, MEM_LEN, N_MEM_HEADS, MEM_HEAD_DIM)
    p_lru_conv = pconv8[None, :, 8 - (LRU_CONV_W - 1):, :]
    p_lru_h = ph8[None, :, 0, :]
    p_ffn_conv = pffn8[None, :, 8 - (FFN_CONV_W - 1):, :]
    y_sample = y_s.reshape(NB, 4, D_MODEL)
    s_swa_k = sk.reshape(1, NB, WINDOW, N_KV_HEADS, HEAD_DIM)
    s_swa_v = sv.reshape(1, NB, WINDOW, N_KV_HEADS, HEAD_DIM)
    s_lru_conv = xr_s.reshape(NB, 4, D_LRU)[None, :, 1:, :]
    s_lru_h = h_s.reshape(NB, 4, D_LRU)[None, :, 3, :]
    s_ffn_conv = gate_s.reshape(NB, 4, D_FF)[None, :, 2:, :]
    return (y_prompt, y_sample, p_swa_k, p_swa_v, p_mem_k, p_mem_v, p_lru_conv, p_lru_h, p_ffn_conv,
            s_swa_k, s_swa_v, s_lru_conv, s_lru_h, s_ffn_conv)
```

```python
import functools

import numpy as np
import jax
import jax.numpy as jnp
from jax import lax
from jax.experimental import pallas as pl
from jax.experimental.pallas import tpu as pltpu

D_MODEL = 1024
D_LRU = 512
LRU_BLOCKS = 8
LRU_BLOCK = 64
LRU_CONV_W = 4
LRU_C = 8.0
N_Q_HEADS = 8
N_KV_HEADS = 2
HEAD_DIM = 64
D_ATTN = 512
D_KV = 128
WINDOW = 128
D_IN = 1792
MEM_LEN = 256
N_MEM_HEADS = 4
MEM_HEAD_DIM = 128
D_MEM = 512
D_FF = 3072
FFN_CONV_W = 3
EPS = 1e-6
NEG_INF = -1e30

F32 = jnp.float32
BF16 = jnp.bfloat16

SLOPES = [float(2.0 ** (-8.0 * (i + 1) / N_Q_HEADS)) for i in range(N_Q_HEADS)]
ATTN_SCALE = HEAD_DIM ** -0.5
MEM_SCALE = MEM_HEAD_DIM ** -0.5
LOG2E = 1.4426950408889634
F32_TINY = 1.1754944e-38

TM = 1024
HR = 512
SEG = HR // 8
TF = 512
FF_CHUNK = 1024
S_NB = 16
S_ROWS = 4 * S_NB
C_NB = 16
C_ROWS = 4 * C_NB
VMEM_LIMIT = 56 * 1024 * 1024


def _dot(a, b):
    return jnp.dot(a, b, preferred_element_type=F32)


def _dot_nt(a, b):
    return lax.dot_general(a, b, (((1,), (1,)), ((), ())), preferred_element_type=F32)


def _rmsnorm(x, g):
    ms = jnp.mean(x * x, axis=-1, keepdims=True)
    return x * lax.rsqrt(ms + EPS) * g


def _gelu(x):
    c = 0.7978845608028654
    return x * (0.5 * (1.0 + jnp.tanh(c * (x + 0.044715 * (x * x * x)))))


def _sigmoid(x):
    return 1.0 / (1.0 + jnp.exp(-x))


def _softplus(x):
    return jnp.maximum(x, 0.0) + jnp.log1p(jnp.exp(-jnp.abs(x)))


def _lru_gates(xc, wax_ref, ba, bx, lam):
    xcb = xc.astype(BF16)
    pa, px = [], []
    for gi in range(2):
        pre = _dot(xcb[:, gi * 256:(gi + 1) * 256], wax_ref[gi])
        pa.append(pre[:, :256])
        px.append(pre[:, 256:])
    r = _sigmoid(jnp.concatenate(pa, axis=1) + ba)
    i = _sigmoid(jnp.concatenate(px, axis=1) + bx)
    log_a = (-LRU_C * _softplus(-lam)) * r
    a = jnp.exp(log_a)
    om = -jnp.tanh(log_a) * (a * a + 1.0)
    b = (om * lax.rsqrt(jnp.maximum(om, F32_TINY))) * (i * xc)
    return a, b


def _head_variants(t):
    lo = lax.broadcasted_iota(jnp.int32, t.shape, 1) < HEAD_DIM
    tr = pltpu.roll(t, HEAD_DIM, 1)
    z = jnp.zeros_like(t)
    x0 = jnp.where(lo, t, z)
    y0 = jnp.where(lo, z, tr)
    x1 = jnp.where(lo, tr, z)
    y1 = jnp.where(lo, z, t)
    return [v.astype(BF16) for v in (x0, y0, x1, y1)]


def _mem_kv_kernel(mem_ref, g_ref, wk_ref, wv_ref, mk_ref, mv_ref, mkb_ref, mvb_ref):
    n = _rmsnorm(mem_ref[0], g_ref[...]).astype(BF16)
    mk = _dot(n, wk_ref[...])
    mv = _dot(n, wv_ref[...])
    mk_ref[0] = mk
    mv_ref[0] = mv
    mkb_ref[0] = mk.astype(BF16)
    mvb_ref[0] = mv.astype(BF16)


def _prompt_mixer_kernel(sinks_ref, x_ref, g_ref, win_ref, wconv_ref, bconv_ref, wax_ref, ba_ref, bx_ref,
                         lam_ref, wout_ref,
                         x1_ref, pk_ref, pv_ref, pconv_ref, ph_ref,
                         proj_s, pin_s, pout_s, xtail_s, a_s, b_s, hl_s, h_s, hc_s, kv_s, ymix_s):
    t = pl.program_id(1)

    @pl.when(t == 0)
    def _():
        xtail_s[...] = jnp.zeros((24, D_LRU), F32)
        hc_s[...] = jnp.zeros((8, D_LRU), F32)
        kv_s[:, 0:WINDOW, :] = jnp.zeros((8, WINDOW, D_KV), BF16)

    def sec(i):
        return slice(i * HR, (i + 1) * HR)

    def in_proj(i):
        n1 = _rmsnorm(x_ref[0, sec(i), :], g_ref[...]).astype(BF16)
        proj_s[sec(i), :] = _dot(n1, win_ref[...])

    def seg_rows(j):
        return pl.ds(64 * (j % (SEG // 8)) + j // (SEG // 8), 8, stride=8)

    row8 = lax.broadcasted_iota(jnp.int32, (8, D_LRU), 0)

    def lru_gates(i):
        r0 = i * HR
        for j in range(HR // 8):
            for l in range(D_LRU // 128):
                pin_s[l, seg_rows(j), :] = proj_s[r0 + 8 * j:r0 + 8 * j + 8, l * 128:(l + 1) * 128]
        xs = jnp.concatenate([pin_s[l] for l in range(D_LRU // 128)], axis=1)
        tail = xs[HR - 24:, :]
        prev = xtail_s[...]
        xtail_s[...] = tail
        heads = [jnp.where(row8 == 0, pltpu.roll(prev[8 * q:8 * q + 8, :], 1, 0),
                           pltpu.roll(tail[8 * q:8 * q + 8, :], 1, 0)) for q in range(3)]
        sh1 = jnp.concatenate(heads[2:] + [xs[:HR - 8, :]], axis=0)
        sh2 = jnp.concatenate(heads[1:] + [xs[:HR - 16, :]], axis=0)
        sh3 = jnp.concatenate(heads + [xs[:HR - 24, :]], axis=0)
        wc = wconv_ref[...]
        xc = bconv_ref[...] + wc[0:1] * sh3 + wc[1:2] * sh2 + wc[2:3] * sh1 + wc[3:4] * xs
        a, b = _lru_gates(xc, wax_ref, ba_ref[...], bx_ref[...], lam_ref[...])
        a_s[sec(i), :] = a
        b_s[sec(i), :] = b

    def lru_scan(i, hcar):
        r0 = i * HR
        hl = b_s[r0:r0 + 8, :]
        ac = a_s[r0:r0 + 8, :]
        hl_s[0:8, :] = hl
        for g in range(1, SEG):
            av = a_s[r0 + 8 * g:r0 + 8 * g + 8, :]
            hl = av * hl + b_s[r0 + 8 * g:r0 + 8 * g + 8, :]
            ac = av * ac
            hl_s[8 * g:8 * g + 8, :] = hl
            a_s[r0 + 8 * g:r0 + 8 * g + 8, :] = ac
        hin = hcar
        for s in range(8):
            hend = hl + ac * hin
            if s < 7:
                hin = jnp.where(row8 == s + 1, pltpu.roll(hend, 1, 0), hin)
        hcar = jnp.broadcast_to(hend[7:8, :], (8, D_LRU))
        for g in range(SEG):
            hg = hl_s[8 * g:8 * g + 8, :] + a_s[r0 + 8 * g:r0 + 8 * g + 8, :] * hin
            for l in range(D_LRU // 128):
                pout_s[l, 8 * g:8 * g + 8, :] = hg[:, l * 128:(l + 1) * 128]
        for j in range(HR // 8):
            h_s[r0 + 8 * j:r0 + 8 * j + 8, :] = jnp.concatenate(
                [pout_s[l, seg_rows(j), :] for l in range(D_LRU // 128)], axis=1)
        gate = proj_s[sec(i), D_LRU:2 * D_LRU]
        ymix_s[sec(i), 0:D_LRU] = (h_s[sec(i), :] * _gelu(gate)).astype(BF16)
        return hcar

    def kv_prep(i):
        k = proj_s[sec(i), 1536:1664]
        v = proj_s[sec(i), 1664:1792]
        for n, arr in enumerate(_head_variants(k) + _head_variants(v)):
            kv_s[n, WINDOW + i * HR:WINDOW + (i + 1) * HR, :] = arr

    qi = lax.broadcasted_iota(jnp.int32, (WINDOW, 2 * WINDOW), 0)
    kj = lax.broadcasted_iota(jnp.int32, (WINDOW, 2 * WINDOW), 1)
    dist = qi + WINDOW - kj
    valid = (dist >= 0) & (dist < WINDOW)
    valid0 = valid & (kj >= jnp.where(t == 0, WINDOW, 0))
    distf = dist.astype(F32)
    bias = [jnp.where(valid, -SLOPES[h] * distf, NEG_INF) for h in range(N_Q_HEADS)]
    bias0 = [jnp.where(valid0, -SLOPES[h] * distf, NEG_INF) for h in range(N_Q_HEADS)]
    lane_lo = lax.broadcasted_iota(jnp.int32, (WINDOW, 2 * HEAD_DIM), 1) < HEAD_DIM

    r512 = lax.broadcasted_iota(jnp.int32, (4 * WINDOW, 2 * HEAD_DIM), 0)
    l512 = lax.broadcasted_iota(jnp.int32, (4 * WINDOW, 2 * HEAD_DIM), 1)
    ones_cols = jnp.where((r512 < 2 * WINDOW) == (l512 < HEAD_DIM), 1.0, 0.0).astype(BF16)
    n_iter = (HR // WINDOW) * 4

    def qk(n):
        j, c = divmod(n, 4)
        hk = c // 2
        rows = slice(j * WINDOW, (j + 1) * WINDOW)
        win = slice(j * WINDOW, (j + 2) * WINDOW)
        qc = (proj_s[rows, 1024 + c * 128:1024 + (c + 1) * 128] * ATTN_SCALE).astype(BF16)
        kcat = jnp.concatenate([kv_s[2 * hk, win, :], kv_s[2 * hk + 1, win, :]], axis=0)
        return _dot_nt(qc, kcat)

    def softmax_pv(n, s):
        j, c = divmod(n, 4)
        hk = c // 2
        rows = slice(j * WINDOW, (j + 1) * WINDOW)
        win = slice(j * WINDOW, (j + 2) * WINDOW)
        vcat = jnp.concatenate([kv_s[4 + 2 * hk, win, :], kv_s[5 + 2 * hk, win, :]], axis=0)
        vaug = jnp.concatenate([vcat, ones_cols], axis=1)
        ps, es = [], []
        for half in range(2):
            h = 2 * c + half
            sink = sinks_ref[h]
            sh = s[:, half * 256:(half + 1) * 256] + (bias0[h] if j == 0 else bias[h])
            m = jnp.maximum(jnp.max(sh, axis=-1, keepdims=True), sink)
            ps.append(jnp.exp(sh - m).astype(BF16))
            es.append(jnp.exp(sink - m))
        oa = _dot(jnp.concatenate(ps, axis=1), vaug)
        den = oa[:, 128:256] + jnp.where(lane_lo, es[0], es[1])
        ymix_s[rows, D_LRU + c * 128:D_LRU + (c + 1) * 128] = (oa[:, 0:128] * (1.0 / den)).astype(BF16)

    def attention(i):
        depth = 2
        base = i * n_iter
        pend = [qk(base + n) for n in range(depth)]
        for n in range(n_iter):
            s = pend.pop(0)
            if n + depth < n_iter:
                pend.append(qk(base + n + depth))
            softmax_pv(base + n, s)

    def out_proj(i):
        x1_ref[0, sec(i), :] = x_ref[0, sec(i), :] + _dot(ymix_s[sec(i), :], wout_ref[...])

    n_sec = TM // HR
    h = hc_s[...]
    in_proj(0)
    lru_gates(0)
    for i in range(n_sec):
        if i + 1 < n_sec:
            in_proj(i + 1)
        h = lru_scan(i, h)
        kv_prep(i)
        attention(i)
        if i + 1 < n_sec:
            lru_gates(i + 1)
        out_proj(i)

    hc_s[...] = h
    ph_ref[0] = h
    pconv_ref[0] = proj_s[TM - 8:, 0:D_LRU]
    pk_ref[0] = proj_s[TM - WINDOW:, 1536:1664]
    pv_ref[0] = proj_s[TM - WINDOW:, 1664:1792]
    kv_s[:, 0:WINDOW, :] = kv_s[:, TM:TM + WINDOW, :]


def _ffn_chunks(n3, acc, gprev_fn, wg_ref, wu_ref, wfc_ref, bfc_ref, wd_ref, on_gate):
    nc = D_FF // FF_CHUNK

    def up(c):
        cs = slice(c * FF_CHUNK, (c + 1) * FF_CHUNK)
        return _dot(n3, wg_ref[:, cs]), _dot(n3, wu_ref[:, cs])

    nxt = up(0)
    for c in range(nc):
        cs = slice(c * FF_CHUNK, (c + 1) * FF_CHUNK)
        g, u = nxt
        if c + 1 < nc:
            nxt = up(c + 1)
        g2, g1 = gprev_fn(c, g)
        on_gate(c, g)
        wfc = wfc_ref[:, cs]
        conv = bfc_ref[:, cs] + wfc[0:1] * g2 + wfc[1:2] * g1 + wfc[2:3] * g
        hmid = (_gelu(conv) * u).astype(BF16)
        acc = acc + _dot(hmid, wd_ref[cs, :])
    return acc


def _prompt_ffn_kernel(x1_ref, mk_ref, mv_ref, gc_ref, wq_ref, wo_ref, gf_ref, wg_ref, wu_ref, wfc_ref, bfc_ref,
                       wd_ref, gfin_ref,
                       y_ref, pffn_ref,
                       oc_s, gbuf_s, gcar_s):
    t = pl.program_id(1)

    @pl.when(t == 0)
    def _():
        gcar_s[...] = jnp.zeros((8, D_FF), F32)

    x1 = x1_ref[0]
    qc = _dot(_rmsnorm(x1, gc_ref[...]).astype(BF16), wq_ref[...]).astype(BF16)
    hsl = [slice(h * MEM_HEAD_DIM, (h + 1) * MEM_HEAD_DIM) for h in range(N_MEM_HEADS)]
    ss = [_dot_nt(qc[:, hs], mk_ref[0, :, hs]) for hs in hsl]
    for h, hs in enumerate(hsl):
        s = ss[h]
        m = jnp.max(s, axis=-1, keepdims=True)
        p = jnp.exp2((s - m) * (MEM_SCALE * LOG2E))
        l = jnp.sum(p, axis=-1, keepdims=True)
        o = _dot(p.astype(BF16), mv_ref[0, :, hs]) * (1.0 / l)
        oc_s[:, hs] = o.astype(BF16)
    x2 = x1 + _dot(oc_s[...], wo_ref[...])
    n3 = _rmsnorm(x2, gf_ref[...]).astype(BF16)

    def gprev(c, g):
        cs = slice(c * FF_CHUNK, (c + 1) * FF_CHUNK)
        buf = gbuf_s.at[c % 2]
        buf[0:8, :] = gcar_s[:, cs]
        buf[8:TF + 8, :] = g
        return buf[6:6 + TF, :], buf[7:7 + TF, :]

    def on_gate(c, g):
        cs = slice(c * FF_CHUNK, (c + 1) * FF_CHUNK)
        tail = g[TF - 8:, :]
        gcar_s[:, cs] = tail
        pffn_ref[0, :, cs] = tail

    acc = _ffn_chunks(n3, x2, gprev, wg_ref, wu_ref, wfc_ref, bfc_ref, wd_ref, on_gate)
    y_ref[0] = _rmsnorm(acc, gfin_ref[...])


def _sample_mixer_kernel(sinks_ref, x_ref, prev4_ref, h0_ref, ck_ref, cv_ref, g_ref, win_ref, wconv_ref, bconv_ref,
                         wax_ref, ba_ref, bx_ref, lam_ref, wout_ref, wkt_ref, wvt_ref,
                         x1_ref, sk_ref, sv_ref, xr_ref, h_ref,
                         q_s, yatt_s):
    R = S_ROWS
    x = x_ref[...]
    n1 = _rmsnorm(x, g_ref[...]).astype(BF16)
    proj = _dot(n1, win_ref[...])
    xr = proj[:, 0:D_LRU]
    gate = proj[:, D_LRU:2 * D_LRU]
    q_s[...] = proj[:, 1024:1536] * ATTN_SCALE
    xr_ref[...] = xr

    tmod = lax.broadcasted_iota(jnp.int32, (R, D_LRU), 0) & 3
    prev4 = prev4_ref[...]
    xs1 = jnp.where(tmod >= 1, pltpu.roll(xr, 1, 0), pltpu.roll(prev4, R - 2, 0))
    xs2 = jnp.where(tmod >= 2, pltpu.roll(xr, 2, 0), pltpu.roll(prev4, R - 1, 0))
    xs3 = jnp.where(tmod >= 3, pltpu.roll(xr, 3, 0), prev4)
    wc = wconv_ref[...]
    xc = bconv_ref[...] + wc[0:1] * xs3 + wc[1:2] * xs2 + wc[2:3] * xs1 + wc[3:4] * xr

    a, b = _lru_gates(xc, wax_ref, ba_ref[...], bx_ref[...], lam_ref[...])
    for s in (1, 2):
        m = tmod >= s
        a_sh = pltpu.roll(a, s, 0)
        b_sh = pltpu.roll(b, s, 0)
        b = jnp.where(m, a * b_sh + b, b)
        a = jnp.where(m, a * a_sh, a)
    h = a * h0_ref[...] + b
    h_ref[...] = h
    y_lru = h * _gelu(gate)

    kt_new = _dot_nt(wkt_ref[...], n1)
    vt_new = _dot_nt(wvt_ref[...], n1)
    zpad = jnp.zeros((D_KV, WINDOW - R), F32)
    kt_pad = jnp.concatenate([kt_new, zpad], axis=1)
    vt_pad = jnp.concatenate([vt_new, zpad], axis=1)
    kt_pad_b = kt_pad.astype(BF16)
    vt_pad_b = vt_pad.astype(BF16)

    r64 = lax.broadcasted_iota(jnp.int32, (8 * N_Q_HEADS, WINDOW), 0)
    c64 = lax.broadcasted_iota(jnp.int32, (8 * N_Q_HEADS, WINDOW), 1)
    t64 = r64 & 3
    slope = jnp.zeros((8 * N_Q_HEADS, WINDOW), F32)
    sinkcol = jnp.zeros((8 * N_Q_HEADS, 1), F32)
    for hq in range(N_Q_HEADS):
        slope = jnp.where((r64 >> 3) == hq, SLOPES[hq], slope)
        sinkcol = jnp.where((r64[:, 0:1] >> 3) == hq, sinks_ref[hq], sinkcol)
    bias_c = jnp.where(c64 > t64, -slope * (t64 + WINDOW - c64).astype(F32), NEG_INF)
    bat0 = (r64 & 7) < 4
    lane_lo = lax.broadcasted_iota(jnp.int32, (8, D_KV), 1) < HEAD_DIM
    lane128 = lax.broadcasted_iota(jnp.int32, (D_KV, WINDOW), 1)

    def pair_scores(p):
        r0 = p * 8
        q8 = q_s[r0:r0 + 8, :]
        blocks = []
        for hq in range(N_Q_HEADS):
            chunk = q8[:, (hq // 2) * 128:(hq // 2 + 1) * 128]
            hk = hq // (N_Q_HEADS // N_KV_HEADS)
            src = chunk if (hq % 2) == hk else pltpu.roll(chunk, HEAD_DIM, 1)
            blocks.append(jnp.where(lane_lo == (hk == 0), src, 0.0))
        lhs = jnp.concatenate(blocks, axis=0).astype(BF16)
        sb = []
        for bb in range(2):
            bidx = 2 * p + bb
            kt = ck_ref[bidx]
            vt = cv_ref[bidx]
            sb.append(_dot(lhs, kt.astype(BF16)))
            shift = (WINDOW - 4 - 4 * bidx) % WINDOW
            sk_ref[bidx] = jnp.where(lane128 >= WINDOW - 4, pltpu.roll(kt_pad, shift, 1),
                                     pltpu.roll(kt, WINDOW - 4, 1))
            sv_ref[bidx] = jnp.where(lane128 >= WINDOW - 4, pltpu.roll(vt_pad, shift, 1),
                                     pltpu.roll(vt, WINDOW - 4, 1))
        s_c = jnp.where(bat0, sb[0], sb[1]) + bias_c
        same = (c64 >> 2) == (2 * p + ((r64 & 7) >> 2))
        dn = t64 - (c64 & 3)
        bias_n = jnp.where(same & (dn >= 0), -slope * dn.astype(F32), NEG_INF)
        s_n = _dot(lhs, kt_pad_b) + bias_n
        return s_c, s_n

    def pair_finish(p, s_c, s_n):
        r0 = p * 8
        m = jnp.maximum(jnp.maximum(jnp.max(s_c, axis=-1, keepdims=True), jnp.max(s_n, axis=-1, keepdims=True)),
                        sinkcol)
        pc = jnp.exp(s_c - m)
        pn = jnp.exp(s_n - m)
        l = jnp.sum(pc, axis=-1, keepdims=True) + jnp.sum(pn, axis=-1, keepdims=True) + jnp.exp(sinkcol - m)
        pcb = pc.astype(BF16)
        ob = [_dot_nt(pcb, cv_ref[2 * p + bb].astype(BF16)) for bb in range(2)]
        o = (jnp.where(bat0, ob[0], ob[1]) + _dot_nt(pn.astype(BF16), vt_pad_b)) * (1.0 / l)
        for c in range(N_Q_HEADS // 2):
            hk = c // 2
            ev = o[16 * c:16 * c + 8, :]
            od = o[16 * c + 8:16 * c + 16, :]
            if hk == 0:
                od = pltpu.roll(od, HEAD_DIM, 1)
            else:
                ev = pltpu.roll(ev, HEAD_DIM, 1)
            yatt_s[r0:r0 + 8, c * 128:(c + 1) * 128] = jnp.where(lane_lo, ev, od)

    n_pairs = S_NB // 2
    cur = pair_scores(0)
    for p in range(n_pairs):
        nxt = pair_scores(p + 1) if p + 1 < n_pairs else None
        pair_finish(p, *cur)
        cur = nxt

    ymix = jnp.concatenate([y_lru, yatt_s[...]], axis=1).astype(BF16)
    x1_ref[...] = x + _dot(ymix, wout_ref[...])


def _load_mem_heads(ref, b):
    return jnp.concatenate([ref[b, pl.ds(h, MEM_LEN, stride=N_MEM_HEADS), :] for h in range(N_MEM_HEADS)],
                           axis=1).astype(BF16)


def _sample_cross_kernel(x1_ref, mk_ref, mv_ref, gc_ref, wq_ref, wo_ref, x2_ref, q_s, oc_s):
    x1 = x1_ref[...]
    q_s[...] = _dot(_rmsnorm(x1, gc_ref[...]).astype(BF16), wq_ref[...])

    lane_head = lax.broadcasted_iota(jnp.int32, (8, D_MEM), 1) // MEM_HEAD_DIM
    bat0 = (lax.broadcasted_iota(jnp.int32, (32, D_MEM), 0) & 7) < 4
    bat0_s = (lax.broadcasted_iota(jnp.int32, (32, MEM_LEN), 0) & 7) < 4

    def scores(p):
        q8 = q_s[p * 8:(p + 1) * 8, :]
        lhs = jnp.concatenate([jnp.where(lane_head == h, q8, 0.0) for h in range(N_MEM_HEADS)],
                              axis=0).astype(BF16)
        sb = [_dot_nt(lhs, _load_mem_heads(mk_ref, 2 * p + bb)) for bb in range(2)]
        return jnp.where(bat0_s, sb[0], sb[1]) * MEM_SCALE

    n_pairs = C_NB // 2
    s_next = scores(0)
    for p in range(n_pairs):
        s = s_next
        if p + 1 < n_pairs:
            s_next = scores(p + 1)
        m = jnp.max(s, axis=-1, keepdims=True)
        pe = jnp.exp(s - m)
        l = jnp.sum(pe, axis=-1, keepdims=True)
        pb = pe.astype(BF16)
        ob = [_dot(pb, _load_mem_heads(mv_ref, 2 * p + bb)) for bb in range(2)]
        o = jnp.where(bat0, ob[0], ob[1]) * (1.0 / l)
        out = jnp.where(lane_head == 0, o[0:8], 0.0)
        for h in range(1, N_MEM_HEADS):
            out = out + jnp.where(lane_head == h, o[h * 8:(h + 1) * 8], 0.0)
        oc_s[p * 8:(p + 1) * 8, :] = out

    x2_ref[...] = x1 + _dot(oc_s[...].astype(BF16), wo_ref[...])


def _sample_ffn_kernel(x2_ref, prev_ref, gf_ref, wg_ref, wu_ref, wfc_ref, bfc_ref, wd_ref, gfin_ref,
                       y_ref, gate_ref, slab_s, p1_s, p2_s):
    R = x2_ref.shape[0]
    nb = R // 4
    nl = FF_CHUNK // 128
    x2 = x2_ref[...]
    n3 = _rmsnorm(x2, gf_ref[...]).astype(BF16)
    tmod = lax.broadcasted_iota(jnp.int32, (R, FF_CHUNK), 0) & 3
    p1_s[...] = jnp.zeros(p1_s.shape, F32)
    p2_s[...] = jnp.zeros(p2_s.shape, F32)

    def gprev(c, g):
        for l in range(nl):
            cols = slice(c * FF_CHUNK + l * 128, c * FF_CHUNK + (l + 1) * 128)
            s0 = prev_ref[0, :, cols]
            s1 = prev_ref[1, :, cols]
            p1_s[l, pl.ds(0, nb, stride=4), :] = s1
            p2_s[l, pl.ds(0, nb, stride=4), :] = s0
            p2_s[l, pl.ds(1, nb, stride=4), :] = s1
        p1 = jnp.concatenate([p1_s[l] for l in range(nl)], axis=1)
        p2 = jnp.concatenate([p2_s[l] for l in range(nl)], axis=1)
        g1 = jnp.where(tmod >= 1, pltpu.roll(g, 1, 0), p1)
        g2 = jnp.where(tmod >= 2, pltpu.roll(g, 2, 0), p2)
        return g2, g1

    def on_gate(c, g):
        for l in range(nl):
            slab_s[l] = g[:, l * 128:(l + 1) * 128]
        for tt in range(2):
            gate_ref[tt, :, c * FF_CHUNK:(c + 1) * FF_CHUNK] = jnp.concatenate(
                [slab_s[l, pl.ds(2 + tt, nb, stride=4), :] for l in range(nl)], axis=1)

    acc = _ffn_chunks(n3, x2, gprev, wg_ref, wu_ref, wfc_ref, bfc_ref, wd_ref, on_gate)
    y_ref[...] = _rmsnorm(acc, gfin_ref[...])


def _const_spec(shape, grid_rank):
    zeros = (0,) * len(shape)
    if grid_rank == 1:
        return pl.BlockSpec(shape, lambda i: zeros, pipeline_mode=pl.Buffered(1))
    return pl.BlockSpec(shape, lambda i, j: zeros, pipeline_mode=pl.Buffered(1))


def _block_diag4(w):
    eye = jnp.eye(4, dtype=w.dtype)
    return (w[:, :, None, :] * eye[:, None, :, None]).reshape(4 * LRU_BLOCK, 4 * LRU_BLOCK)


def _params(sem):
    return pltpu.CompilerParams(dimension_semantics=sem, vmem_limit_bytes=VMEM_LIMIT)


def kernel(x_prompt, x_sample, cache_swa_k, cache_swa_v, cache_mem_k, cache_mem_v, state_lru_conv, state_lru_h, state_ffn_conv, mem_prompt, g_mix, w_in, w_lru_conv, b_lru_conv, w_lru_a, b_lru_a, w_lru_x, b_lru_x, lru_lambda, attn_sinks, w_out, g_cross, g_mem, w_mem_q, w_mem_k, w_mem_v, w_mem_o, g_ffn, w_ffn_gate, w_ffn_up, w_ffn_conv, b_ffn_conv, w_ffn_down, g_final):
    B, T, _ = x_prompt.shape
    NB = x_sample.shape[0]
    NT = T // TM
    SR = NB * 4

    win = w_in[0].astype(BF16)
    wout = w_out[0].astype(BF16)
    wq = w_mem_q[0].astype(BF16)
    wk = w_mem_k[0].astype(BF16)
    wv = w_mem_v[0].astype(BF16)
    wo = w_mem_o[0].astype(BF16)
    wg = w_ffn_gate[0].astype(BF16)
    wu = w_ffn_up[0].astype(BF16)
    wd = w_ffn_down[0].astype(BF16)
    wax = jnp.stack([
        jnp.concatenate([_block_diag4(w_lru_a[0, 4 * gi:4 * gi + 4]), _block_diag4(w_lru_x[0, 4 * gi:4 * gi + 4])],
                        axis=1) for gi in range(2)]).astype(BF16)
    gmix, gcross, gmem, gffn = g_mix, g_cross, g_mem, g_ffn
    gfin = g_final.reshape(1, D_MODEL)
    wconv, bconv = w_lru_conv[0], b_lru_conv
    ba, bx, lam = b_lru_a, b_lru_x, lru_lambda
    wfc, bfc = w_ffn_conv[0], b_ffn_conv
    sinks = attn_sinks[0]
    smem = pl.BlockSpec(memory_space=pltpu.SMEM)

    mk, mv, mkb, mvb = pl.pallas_call(
        _mem_kv_kernel,
        grid=(B,),
        in_specs=[pl.BlockSpec((1, MEM_LEN, D_MODEL), lambda b: (b, 0, 0)),
                  _const_spec((1, D_MODEL), 1), _const_spec((D_MODEL, D_MEM), 1), _const_spec((D_MODEL, D_MEM), 1)],
        out_specs=[pl.BlockSpec((1, MEM_LEN, D_MEM), lambda b: (b, 0, 0))] * 4,
        out_shape=[jax.ShapeDtypeStruct((B, MEM_LEN, D_MEM), F32)] * 2
        + [jax.ShapeDtypeStruct((B, MEM_LEN, D_MEM), BF16)] * 2,
        compiler_params=_params(("arbitrary",)),
        name="mem_kv",
    )(mem_prompt, gmem, wk, wv)

    x1p, pk, pv, pconv8, ph8 = pl.pallas_call(
        _prompt_mixer_kernel,
        grid=(B, NT),
        in_specs=[smem,
                  pl.BlockSpec((1, TM, D_MODEL), lambda b, t: (b, t, 0)),
                  _const_spec((1, D_MODEL), 2), _const_spec((D_MODEL, D_IN), 2),
                  _const_spec((LRU_CONV_W, D_LRU), 2), _const_spec((1, D_LRU), 2),
                  _const_spec((2, 256, 512), 2), _const_spec((1, D_LRU), 2), _const_spec((1, D_LRU), 2),
                  _const_spec((1, D_LRU), 2), _const_spec((D_MODEL, D_MODEL), 2)],
        out_specs=[pl.BlockSpec((1, TM, D_MODEL), lambda b, t: (b, t, 0)),
                   pl.BlockSpec((1, WINDOW, D_KV), lambda b, t: (b, 0, 0)),
                   pl.BlockSpec((1, WINDOW, D_KV), lambda b, t: (b, 0, 0)),
                   pl.BlockSpec((1, 8, D_LRU), lambda b, t: (b, 0, 0)),
                   pl.BlockSpec((1, 8, D_LRU), lambda b, t: (b, 0, 0))],
        out_shape=[jax.ShapeDtypeStruct((B, T, D_MODEL), F32),
                   jax.ShapeDtypeStruct((B, WINDOW, D_KV), F32),
                   jax.ShapeDtypeStruct((B, WINDOW, D_KV), F32),
                   jax.ShapeDtypeStruct((B, 8, D_LRU), F32),
                   jax.ShapeDtypeStruct((B, 8, D_LRU), F32)],
        scratch_shapes=[pltpu.VMEM((TM, D_IN), F32),
                        pltpu.VMEM((D_LRU // 128, HR, 128), F32),
                        pltpu.VMEM((D_LRU // 128, HR, 128), F32),
                        pltpu.VMEM((24, D_LRU), F32),
                        pltpu.VMEM((TM, D_LRU), F32),
                        pltpu.VMEM((TM, D_LRU), F32),
                        pltpu.VMEM((HR, D_LRU), F32),
                        pltpu.VMEM((TM, D_LRU), F32),
                        pltpu.VMEM((8, D_LRU), F32),
                        pltpu.VMEM((8, TM + WINDOW, D_KV), BF16),
                        pltpu.VMEM((TM, D_MODEL), BF16)],
        compiler_params=_params(("arbitrary", "arbitrary")),
        name="prompt_mixer",
    )(sinks, x_prompt, gmix, win, wconv, bconv, wax, ba, bx, lam, wout)

    y_prompt, pffn8 = pl.pallas_call(
        _prompt_ffn_kernel,
        grid=(B, T // TF),
        in_specs=[pl.BlockSpec((1, TF, D_MODEL), lambda b, t: (b, t, 0)),
                  pl.BlockSpec((1, MEM_LEN, D_MEM), lambda b, t: (b, 0, 0)),
                  pl.BlockSpec((1, MEM_LEN, D_MEM), lambda b, t: (b, 0, 0)),
                  _const_spec((1, D_MODEL), 2), _const_spec((D_MODEL, D_MEM), 2), _const_spec((D_MEM, D_MODEL), 2),
                  _const_spec((1, D_MODEL), 2), _const_spec((D_MODEL, D_FF), 2), _const_spec((D_MODEL, D_FF), 2),
                  _const_spec((FFN_CONV_W, D_FF), 2), _const_spec((1, D_FF), 2), _const_spec((D_FF, D_MODEL), 2),
                  _const_spec((1, D_MODEL), 2)],
        out_specs=[pl.BlockSpec((1, TF, D_MODEL), lambda b, t: (b, t, 0)),
                   pl.BlockSpec((1, 8, D_FF), lambda b, t: (b, 0, 0))],
        out_shape=[jax.ShapeDtypeStruct((B, T, D_MODEL), F32),
                   jax.ShapeDtypeStruct((B, 8, D_FF), F32)],
        scratch_shapes=[pltpu.VMEM((TF, D_MEM), BF16),
                        pltpu.VMEM((2, TF + 8, FF_CHUNK), F32),
                        pltpu.VMEM((8, D_FF), F32)],
        compiler_params=_params(("arbitrary", "arbitrary")),
        name="prompt_ffn",
    )(x1p, mkb, mvb, gcross, wq, wo, gffn, wg, wu, wfc, bfc, wd, gfin)

    xs = x_sample.reshape(SR, D_MODEL)
    conv_prev4 = jnp.pad(state_lru_conv[0], ((0, 0), (0, 1), (0, 0))).reshape(SR, D_LRU)
    h0rep = jnp.repeat(state_lru_h[0], 4, axis=0)
    ck = jnp.transpose(cache_swa_k[0], (0, 2, 3, 1)).reshape(NB, D_KV, WINDOW)
    cv = jnp.transpose(cache_swa_v[0], (0, 2, 3, 1)).reshape(NB, D_KV, WINDOW)
    wkt = jnp.transpose(w_in[0][:, 1536:1664]).astype(BF16)
    wvt = jnp.transpose(w_in[0][:, 1664:1792]).astype(BF16)
    row_spec = lambda w: pl.BlockSpec((S_ROWS, w), lambda i: (i, 0))
    cache_spec = pl.BlockSpec((S_NB, D_KV, WINDOW), lambda i: (i, 0, 0))
    x1s, sk, sv, xr_s, h_s = pl.pallas_call(
        _sample_mixer_kernel,
        grid=(NB // S_NB,),
        in_specs=[smem, row_spec(D_MODEL), row_spec(D_LRU), row_spec(D_LRU), cache_spec, cache_spec,
                  _const_spec((1, D_MODEL), 1), _const_spec((D_MODEL, D_IN), 1),
                  _const_spec((LRU_CONV_W, D_LRU), 1), _const_spec((1, D_LRU), 1),
                  _const_spec((2, 256, 512), 1), _const_spec((1, D_LRU), 1), _const_spec((1, D_LRU), 1),
                  _const_spec((1, D_LRU), 1), _const_spec((D_MODEL, D_MODEL), 1),
                  _const_spec((D_KV, D_MODEL), 1), _const_spec((D_KV, D_MODEL), 1)],
        out_specs=[row_spec(D_MODEL), cache_spec, cache_spec, row_spec(D_LRU), row_spec(D_LRU)],
        out_shape=[jax.ShapeDtypeStruct((SR, D_MODEL), F32),
                   jax.ShapeDtypeStruct((NB, D_KV, WINDOW), F32),
                   jax.ShapeDtypeStruct((NB, D_KV, WINDOW), F32),
                   jax.ShapeDtypeStruct((SR, D_LRU), F32),
                   jax.ShapeDtypeStruct((SR, D_LRU), F32)],
        scratch_shapes=[pltpu.VMEM((S_ROWS, D_ATTN), F32),
                        pltpu.VMEM((S_ROWS, D_ATTN), F32)],
        compiler_params=_params(("arbitrary",)),
        name="sample_mixer",
    )(sinks, xs, conv_prev4, h0rep, ck, cv, gmix, win, wconv, bconv, wax, ba, bx, lam, wout, wkt, wvt)

    cmk = cache_mem_k.reshape(NB, MEM_LEN * N_MEM_HEADS, MEM_HEAD_DIM)
    cmv = cache_mem_v.reshape(NB, MEM_LEN * N_MEM_HEADS, MEM_HEAD_DIM)
    crow = pl.BlockSpec((C_ROWS, D_MODEL), lambda i: (i, 0))
    cmem = pl.BlockSpec((C_NB, MEM_LEN * N_MEM_HEADS, MEM_HEAD_DIM), lambda i: (i, 0, 0))
    x2s = pl.pallas_call(
        _sample_cross_kernel,
        grid=(NB // C_NB,),
        in_specs=[crow, cmem, cmem, _const_spec((1, D_MODEL), 1), _const_spec((D_MODEL, D_MEM), 1),
                  _const_spec((D_MEM, D_MODEL), 1)],
        out_specs=crow,
        out_shape=jax.ShapeDtypeStruct((SR, D_MODEL), F32),
        scratch_shapes=[pltpu.VMEM((C_ROWS, D_MEM), F32), pltpu.VMEM((C_ROWS, D_MEM), F32)],
        compiler_params=_params(("arbitrary",)),
        name="sample_cross",
    )(x1s, cmk, cmv, gcross, wq, wo)

    ffn_prev_tm = jnp.transpose(state_ffn_conv[0], (1, 0, 2))
    slab = pltpu.VMEM((FF_CHUNK // 128, SR, 128), F32)
    y_s, gate_tm = pl.pallas_call(
        _sample_ffn_kernel,
        grid=(1,),
        in_specs=[_const_spec((SR, D_MODEL), 1), _const_spec((FFN_CONV_W - 1, NB, D_FF), 1),
                  _const_spec((1, D_MODEL), 1), _const_spec((D_MODEL, D_FF), 1), _const_spec((D_MODEL, D_FF), 1),
                  _const_spec((FFN_CONV_W, D_FF), 1), _const_spec((1, D_FF), 1), _const_spec((D_FF, D_MODEL), 1),
                  _const_spec((1, D_MODEL), 1)],
        out_specs=[pl.BlockSpec((SR, D_MODEL), lambda i: (0, 0)),
                   pl.BlockSpec((FFN_CONV_W - 1, NB, D_FF), lambda i: (0, 0, 0))],
        out_shape=[jax.ShapeDtypeStruct((SR, D_MODEL), F32),
                   jax.ShapeDtypeStruct((FFN_CONV_W - 1, NB, D_FF), F32)],
        scratch_shapes=[slab, slab, slab],
        compiler_params=_params(("arbitrary",)),
        name="sample_ffn",
    )(x2s, ffn_prev_tm, gffn, wg, wu, wfc, bfc, wd, gfin)

    p_swa_k = pk.reshape(1, B, WINDOW, N_KV_HEADS, HEAD_DIM)
    p_swa_v = pv.reshape(1, B, WINDOW, N_KV_HEADS, HEAD_DIM)
    p_mem_k = mk.reshape(1, B, MEM_LEN, N_MEM_HEADS, MEM_HEAD_DIM)
    p_mem_v = mv.reshape(1, B, MEM_LEN, N_MEM_HEADS, MEM_HEAD_DIM)
    p_lru_conv = pconv8[None, :, 8 - (LRU_CONV_W - 1):, :]
    p_lru_h = ph8[None, :, 0, :]
    p_ffn_conv = pffn8[None, :, 8 - (FFN_CONV_W - 1):, :]
    y_sample = y_s.reshape(NB, 4, D_MODEL)
    s_swa_k = jnp.transpose(sk.reshape(NB, N_KV_HEADS, HEAD_DIM, WINDOW), (0, 3, 1, 2))[None]
    s_swa_v = jnp.transpose(sv.reshape(NB, N_KV_HEADS, HEAD_DIM, WINDOW), (0, 3, 1, 2))[None]
    s_lru_conv = xr_s.reshape(NB, 4, D_LRU)[None, :, 1:, :]
    s_lru_h = h_s.reshape(NB, 4, D_LRU)[None, :, 3, :]
    s_ffn_conv = jnp.transpose(gate_tm, (1, 0, 2))[None]
    return (y_prompt, y_sample, p_swa_k, p_swa_v, p_mem_k, p_mem_v, p_lru_conv, p_lru_h, p_ffn_conv,
            s_swa_k, s_swa_v, s_lru_conv, s_lru_h, s_ffn_conv)
```

```python
import functools

import numpy as np
import jax
import jax.numpy as jnp
from jax import lax
from jax.experimental import pallas as pl
from jax.experimental.pallas import tpu as pltpu

D_MODEL = 1024
D_LRU = 512
LRU_BLOCKS = 8
LRU_BLOCK = 64
LRU_CONV_W = 4
LRU_C = 8.0
N_Q_HEADS = 8
N_KV_HEADS = 2
HEAD_DIM = 64
D_ATTN = 512
D_KV = 128
WINDOW = 128
D_IN = 1792
MEM_LEN = 256
N_MEM_HEADS = 4
MEM_HEAD_DIM = 128
D_MEM = 512
D_FF = 3072
FFN_CONV_W = 3
EPS = 1e-6
NEG_INF = -1e30

F32 = jnp.float32
BF16 = jnp.bfloat16

SLOPES = [float(2.0 ** (-8.0 * (i + 1) / N_Q_HEADS)) for i in range(N_Q_HEADS)]
ATTN_SCALE = HEAD_DIM ** -0.5
MEM_SCALE = MEM_HEAD_DIM ** -0.5
LOG2E = 1.4426950408889634
F32_TINY = 1.1754944e-38

TM = 1024
HR = 512
SEG = HR // 8
TF = 512
FF_CHUNK = 1024
S_NB = 16
S_ROWS = 4 * S_NB
C_NB = 16
C_ROWS = 4 * C_NB
VMEM_LIMIT = 56 * 1024 * 1024


def _dot(a, b):
    return jnp.dot(a, b, preferred_element_type=F32)


def _dot_nt(a, b):
    return lax.dot_general(a, b, (((1,), (1,)), ((), ())), preferred_element_type=F32)


def _rmsnorm(x, g):
    ms = jnp.mean(x * x, axis=-1, keepdims=True)
    return x * lax.rsqrt(ms + EPS) * g


def _gelu(x):
    c = 0.7978845608028654
    return x * (0.5 * (1.0 + jnp.tanh(c * (x + 0.044715 * (x * x * x)))))


def _sigmoid(x):
    return 1.0 / (1.0 + jnp.exp(-x))


def _softplus(x):
    return jnp.maximum(x, 0.0) + jnp.log1p(jnp.exp(-jnp.abs(x)))


def _lru_gates(xc, wax_ref, ba, bx, lam):
    xcb = xc.astype(BF16)
    pa, px = [], []
    for gi in range(2):
        pre = _dot(xcb[:, gi * 256:(gi + 1) * 256], wax_ref[gi])
        pa.append(pre[:, :256])
        px.append(pre[:, 256:])
    r = _sigmoid(jnp.concatenate(pa, axis=1) + ba)
    i = _sigmoid(jnp.concatenate(px, axis=1) + bx)
    log_a = (-LRU_C * _softplus(-lam)) * r
    a = jnp.exp(log_a)
    om = -jnp.tanh(log_a) * (a * a + 1.0)
    b = (om * lax.rsqrt(jnp.maximum(om, F32_TINY))) * (i * xc)
    return a, b


def _head_variants(t):
    lo = lax.broadcasted_iota(jnp.int32, t.shape, 1) < HEAD_DIM
    tr = pltpu.roll(t, HEAD_DIM, 1)
    z = jnp.zeros_like(t)
    x0 = jnp.where(lo, t, z)
    y0 = jnp.where(lo, z, tr)
    x1 = jnp.where(lo, tr, z)
    y1 = jnp.where(lo, z, t)
    return [v.astype(BF16) for v in (x0, y0, x1, y1)]


def _mem_kv_kernel(mem_ref, g_ref, wk_ref, wv_ref, mk_ref, mv_ref, mkb_ref, mvb_ref):
    n = _rmsnorm(mem_ref[0], g_ref[...]).astype(BF16)
    mk = _dot(n, wk_ref[...])
    mv = _dot(n, wv_ref[...])
    mk_ref[0] = mk
    mv_ref[0] = mv
    mkb_ref[0] = mk.astype(BF16)
    mvb_ref[0] = mv.astype(BF16)


def _prompt_mixer_kernel(sinks_ref, x_ref, g_ref, win_ref, wconv_ref, bconv_ref, wax_ref, ba_ref, bx_ref,
                         lam_ref, wout_ref,
                         x1_ref, pk_ref, pv_ref, pconv_ref, ph_ref,
                         proj_s, pin_s, pout_s, xtail_s, a_s, b_s, hl_s, h_s, hc_s, kv_s, ymix_s):
    t = pl.program_id(1)

    @pl.when(t == 0)
    def _():
        xtail_s[...] = jnp.zeros((24, D_LRU), F32)
        hc_s[...] = jnp.zeros((8, D_LRU), F32)
        kv_s[:, 0:WINDOW, :] = jnp.zeros((8, WINDOW, D_KV), BF16)

    def sec(i):
        return slice(i * HR, (i + 1) * HR)

    def in_proj(i):
        n1 = _rmsnorm(x_ref[0, sec(i), :], g_ref[...]).astype(BF16)
        proj_s[sec(i), :] = _dot(n1, win_ref[...])

    def seg_rows(j):
        return pl.ds(64 * (j % (SEG // 8)) + j // (SEG // 8), 8, stride=8)

    row8 = lax.broadcasted_iota(jnp.int32, (8, D_LRU), 0)

    def lru_gates(i):
        r0 = i * HR
        for j in range(HR // 8):
            for l in range(D_LRU // 128):
                pin_s[l, seg_rows(j), :] = proj_s[r0 + 8 * j:r0 + 8 * j + 8, l * 128:(l + 1) * 128]
        xs = jnp.concatenate([pin_s[l] for l in range(D_LRU // 128)], axis=1)
        tail = xs[HR - 24:, :]
        prev = xtail_s[...]
        xtail_s[...] = tail
        heads = [jnp.where(row8 == 0, pltpu.roll(prev[8 * q:8 * q + 8, :], 1, 0),
                           pltpu.roll(tail[8 * q:8 * q + 8, :], 1, 0)) for q in range(3)]
        sh1 = jnp.concatenate(heads[2:] + [xs[:HR - 8, :]], axis=0)
        sh2 = jnp.concatenate(heads[1:] + [xs[:HR - 16, :]], axis=0)
        sh3 = jnp.concatenate(heads + [xs[:HR - 24, :]], axis=0)
        wc = wconv_ref[...]
        xc = bconv_ref[...] + wc[0:1] * sh3 + wc[1:2] * sh2 + wc[2:3] * sh1 + wc[3:4] * xs
        a, b = _lru_gates(xc, wax_ref, ba_ref[...], bx_ref[...], lam_ref[...])
        a_s[sec(i), :] = a
        b_s[sec(i), :] = b

    def lru_scan(i, hcar):
        r0 = i * HR
        hl = b_s[r0:r0 + 8, :]
        ac = a_s[r0:r0 + 8, :]
        hl_s[0:8, :] = hl
        for g in range(1, SEG):
            av = a_s[r0 + 8 * g:r0 + 8 * g + 8, :]
            hl = av * hl + b_s[r0 + 8 * g:r0 + 8 * g + 8, :]
            ac = av * ac
            hl_s[8 * g:8 * g + 8, :] = hl
            a_s[r0 + 8 * g:r0 + 8 * g + 8, :] = ac
        hin = hcar
        for s in range(8):
            hend = hl + ac * hin
            if s < 7:
                hin = jnp.where(row8 == s + 1, pltpu.roll(hend, 1, 0), hin)
        hcar = jnp.broadcast_to(hend[7:8, :], (8, D_LRU))
        for g in range(SEG):
            hg = hl_s[8 * g:8 * g + 8, :] + a_s[r0 + 8 * g:r0 + 8 * g + 8, :] * hin
            for l in range(D_LRU // 128):
                pout_s[l, 8 * g:8 * g + 8, :] = hg[:, l * 128:(l + 1) * 128]
        for j in range(HR // 8):
            h_s[r0 + 8 * j:r0 + 8 * j + 8, :] = jnp.concatenate(
                [pout_s[l, seg_rows(j), :] for l in range(D_LRU // 128)], axis=1)
        gate = proj_s[sec(i), D_LRU:2 * D_LRU]
        ymix_s[sec(i), 0:D_LRU] = (h_s[sec(i), :] * _gelu(gate)).astype(BF16)
        return hcar

    def kv_prep(i):
        k = proj_s[sec(i), 1536:1664]
        v = proj_s[sec(i), 1664:1792]
        for n, arr in enumerate(_head_variants(k) + _head_variants(v)):
            kv_s[n, WINDOW + i * HR:WINDOW + (i + 1) * HR, :] = arr

    qi = lax.broadcasted_iota(jnp.int32, (WINDOW, 2 * WINDOW), 0)
    kj = lax.broadcasted_iota(jnp.int32, (WINDOW, 2 * WINDOW), 1)
    dist = qi + WINDOW - kj
    valid = (dist >= 0) & (dist < WINDOW)
    valid0 = valid & (kj >= jnp.where(t == 0, WINDOW, 0))
    distf = dist.astype(F32)
    bias = [jnp.where(valid, -SLOPES[h] * distf, NEG_INF) for h in range(N_Q_HEADS)]
    bias0 = [jnp.where(valid0, -SLOPES[h] * distf, NEG_INF) for h in range(N_Q_HEADS)]
    lane_lo = lax.broadcasted_iota(jnp.int32, (WINDOW, 2 * HEAD_DIM), 1) < HEAD_DIM

    r512 = lax.broadcasted_iota(jnp.int32, (4 * WINDOW, 2 * HEAD_DIM), 0)
    l512 = lax.broadcasted_iota(jnp.int32, (4 * WINDOW, 2 * HEAD_DIM), 1)
    ones_cols = jnp.where((r512 < 2 * WINDOW) == (l512 < HEAD_DIM), 1.0, 0.0).astype(BF16)
    n_iter = (HR // WINDOW) * 4

    def qk(n):
        j, c = divmod(n, 4)
        hk = c // 2
        rows = slice(j * WINDOW, (j + 1) * WINDOW)
        win = slice(j * WINDOW, (j + 2) * WINDOW)
        qc = (proj_s[rows, 1024 + c * 128:1024 + (c + 1) * 128] * ATTN_SCALE).astype(BF16)
        kcat = jnp.concatenate([kv_s[2 * hk, win, :], kv_s[2 * hk + 1, win, :]], axis=0)
        return _dot_nt(qc, kcat)

    def softmax_pv(n, s):
        j, c = divmod(n, 4)
        hk = c // 2
        rows = slice(j * WINDOW, (j + 1) * WINDOW)
        win = slice(j * WINDOW, (j + 2) * WINDOW)
        vcat = jnp.concatenate([kv_s[4 + 2 * hk, win, :], kv_s[5 + 2 * hk, win, :]], axis=0)
        vaug = jnp.concatenate([vcat, ones_cols], axis=1)
        ps, es = [], []
        for half in range(2):
            h = 2 * c + half
            sink = sinks_ref[h]
            sh = s[:, half * 256:(half + 1) * 256] + (bias0[h] if j == 0 else bias[h])
            m = jnp.maximum(jnp.max(sh, axis=-1, keepdims=True), sink)
            ps.append(jnp.exp(sh - m).astype(BF16))
            es.append(jnp.exp(sink - m))
        oa = _dot(jnp.concatenate(ps, axis=1), vaug)
        den = oa[:, 128:256] + jnp.where(lane_lo, es[0], es[1])
        ymix_s[rows, D_LRU + c * 128:D_LRU + (c + 1) * 128] = (oa[:, 0:128] * (1.0 / den)).astype(BF16)

    def attention(i):
        depth = 2
        base = i * n_iter
        pend = [qk(base + n) for n in range(depth)]
        for n in range(n_iter):
            s = pend.pop(0)
            if n + depth < n_iter:
                pend.append(qk(base + n + depth))
            softmax_pv(base + n, s)

    def out_proj(i):
        x1_ref[0, sec(i), :] = x_ref[0, sec(i), :] + _dot(ymix_s[sec(i), :], wout_ref[...])

    n_sec = TM // HR
    h = hc_s[...]
    in_proj(0)
    lru_gates(0)
    for i in range(n_sec):
        if i + 1 < n_sec:
            in_proj(i + 1)
        h = lru_scan(i, h)
        kv_prep(i)
        attention(i)
        if i + 1 < n_sec:
            lru_gates(i + 1)
        out_proj(i)

    hc_s[...] = h
    ph_ref[0] = h
    pconv_ref[0] = proj_s[TM - 8:, 0:D_LRU]
    pk_ref[0] = proj_s[TM - WINDOW:, 1536:1664]
    pv_ref[0] = proj_s[TM - WINDOW:, 1664:1792]
    kv_s[:, 0:WINDOW, :] = kv_s[:, TM:TM + WINDOW, :]


def _ffn_chunks(n3, acc, gprev_fn, wg_ref, wu_ref, wfc_ref, bfc_ref, wd_ref, on_gate):
    nc = D_FF // FF_CHUNK

    def up(c):
        cs = slice(c * FF_CHUNK, (c + 1) * FF_CHUNK)
        return _dot(n3, wg_ref[:, cs]), _dot(n3, wu_ref[:, cs])

    nxt = up(0)
    for c in range(nc):
        cs = slice(c * FF_CHUNK, (c + 1) * FF_CHUNK)
        g, u = nxt
        if c + 1 < nc:
            nxt = up(c + 1)
        g2, g1 = gprev_fn(c, g)
        on_gate(c, g)
        wfc = wfc_ref[:, cs]
        conv = bfc_ref[:, cs] + wfc[0:1] * g2 + wfc[1:2] * g1 + wfc[2:3] * g
        hmid = (_gelu(conv) * u).astype(BF16)
        acc = acc + _dot(hmid, wd_ref[cs, :])
    return acc


def _prompt_ffn_kernel(x1_ref, mk_ref, mv_ref, gc_ref, wq_ref, wo_ref, gf_ref, wg_ref, wu_ref, wfc_ref, bfc_ref,
                       wd_ref, gfin_ref,
                       y_ref, pffn_ref,
                       oc_s, gbuf_s, gcar_s):
    t = pl.program_id(1)

    @pl.when(t == 0)
    def _():
        gcar_s[...] = jnp.zeros((8, D_FF), F32)

    x1 = x1_ref[0]
    qc = _dot(_rmsnorm(x1, gc_ref[...]).astype(BF16), wq_ref[...]).astype(BF16)
    hsl = [slice(h * MEM_HEAD_DIM, (h + 1) * MEM_HEAD_DIM) for h in range(N_MEM_HEADS)]
    ss = [_dot_nt(qc[:, hs], mk_ref[0, :, hs]) for hs in hsl]
    for h, hs in enumerate(hsl):
        s = ss[h]
        m = jnp.max(s, axis=-1, keepdims=True)
        p = jnp.exp2((s - m) * (MEM_SCALE * LOG2E))
        l = jnp.sum(p, axis=-1, keepdims=True)
        o = _dot(p.astype(BF16), mv_ref[0, :, hs]) * (1.0 / l)
        oc_s[:, hs] = o.astype(BF16)
    x2 = x1 + _dot(oc_s[...], wo_ref[...])
    n3 = _rmsnorm(x2, gf_ref[...]).astype(BF16)

    def gprev(c, g):
        cs = slice(c * FF_CHUNK, (c + 1) * FF_CHUNK)
        gbuf_s[0:8, :] = gcar_s[:, cs]
        gbuf_s[8:TF + 8, :] = g
        return gbuf_s[6:6 + TF, :], gbuf_s[7:7 + TF, :]

    def on_gate(c, g):
        cs = slice(c * FF_CHUNK, (c + 1) * FF_CHUNK)
        tail = g[TF - 8:, :]
        gcar_s[:, cs] = tail
        pffn_ref[0, :, cs] = tail

    acc = _ffn_chunks(n3, x2, gprev, wg_ref, wu_ref, wfc_ref, bfc_ref, wd_ref, on_gate)
    y_ref[0] = _rmsnorm(acc, gfin_ref[...])


def _sample_mixer_kernel(sinks_ref, x_ref, prev4_ref, h0_ref, ck_ref, cv_ref, g_ref, win_ref, wconv_ref, bconv_ref,
                         wax_ref, ba_ref, bx_ref, lam_ref, wout_ref, wkt_ref, wvt_ref,
                         x1_ref, sk_ref, sv_ref, xr_ref, h_ref,
                         q_s, yatt_s):
    R = S_ROWS
    x = x_ref[...]
    n1 = _rmsnorm(x, g_ref[...]).astype(BF16)
    proj = _dot(n1, win_ref[...])
    xr = proj[:, 0:D_LRU]
    gate = proj[:, D_LRU:2 * D_LRU]
    q_s[...] = proj[:, 1024:1536] * ATTN_SCALE
    xr_ref[...] = xr

    tmod = lax.broadcasted_iota(jnp.int32, (R, D_LRU), 0) & 3
    prev4 = prev4_ref[...]
    xs1 = jnp.where(tmod >= 1, pltpu.roll(xr, 1, 0), pltpu.roll(prev4, R - 2, 0))
    xs2 = jnp.where(tmod >= 2, pltpu.roll(xr, 2, 0), pltpu.roll(prev4, R - 1, 0))
    xs3 = jnp.where(tmod >= 3, pltpu.roll(xr, 3, 0), prev4)
    wc = wconv_ref[...]
    xc = bconv_ref[...] + wc[0:1] * xs3 + wc[1:2] * xs2 + wc[2:3] * xs1 + wc[3:4] * xr

    a, b = _lru_gates(xc, wax_ref, ba_ref[...], bx_ref[...], lam_ref[...])
    for s in (1, 2):
        m = tmod >= s
        a_sh = pltpu.roll(a, s, 0)
        b_sh = pltpu.roll(b, s, 0)
        b = jnp.where(m, a * b_sh + b, b)
        a = jnp.where(m, a * a_sh, a)
    h = a * h0_ref[...] + b
    h_ref[...] = h
    y_lru = h * _gelu(gate)

    kt_new = _dot_nt(wkt_ref[...], n1)
    vt_new = _dot_nt(wvt_ref[...], n1)
    zpad = jnp.zeros((D_KV, WINDOW - R), F32)
    kt_pad = jnp.concatenate([kt_new, zpad], axis=1)
    vt_pad = jnp.concatenate([vt_new, zpad], axis=1)
    kt_pad_b = kt_pad.astype(BF16)
    vt_pad_b = vt_pad.astype(BF16)

    r64 = lax.broadcasted_iota(jnp.int32, (8 * N_Q_HEADS, WINDOW), 0)
    c64 = lax.broadcasted_iota(jnp.int32, (8 * N_Q_HEADS, WINDOW), 1)
    t64 = r64 & 3
    slope = jnp.zeros((8 * N_Q_HEADS, WINDOW), F32)
    sinkcol = jnp.zeros((8 * N_Q_HEADS, 1), F32)
    for hq in range(N_Q_HEADS):
        slope = jnp.where((r64 >> 3) == hq, SLOPES[hq], slope)
        sinkcol = jnp.where((r64[:, 0:1] >> 3) == hq, sinks_ref[hq], sinkcol)
    bias_c = jnp.where(c64 > t64, -slope * (t64 + WINDOW - c64).astype(F32), NEG_INF)
    bat0 = (r64 & 7) < 4
    lane_lo = lax.broadcasted_iota(jnp.int32, (8, D_KV), 1) < HEAD_DIM
    lane128 = lax.broadcasted_iota(jnp.int32, (D_KV, WINDOW), 1)

    def pair_scores(p):
        r0 = p * 8
        q8 = q_s[r0:r0 + 8, :]
        blocks = []
        for hq in range(N_Q_HEADS):
            chunk = q8[:, (hq // 2) * 128:(hq // 2 + 1) * 128]
            hk = hq // (N_Q_HEADS // N_KV_HEADS)
            src = chunk if (hq % 2) == hk else pltpu.roll(chunk, HEAD_DIM, 1)
            blocks.append(jnp.where(lane_lo == (hk == 0), src, 0.0))
        lhs = jnp.concatenate(blocks, axis=0).astype(BF16)
        sb = []
        for bb in range(2):
            bidx = 2 * p + bb
            kt = ck_ref[bidx]
            vt = cv_ref[bidx]
            sb.append(_dot(lhs, kt.astype(BF16)))
            shift = (WINDOW - 4 - 4 * bidx) % WINDOW
            sk_ref[bidx] = jnp.where(lane128 >= WINDOW - 4, pltpu.roll(kt_pad, shift, 1),
                                     pltpu.roll(kt, WINDOW - 4, 1))
            sv_ref[bidx] = jnp.where(lane128 >= WINDOW - 4, pltpu.roll(vt_pad, shift, 1),
                                     pltpu.roll(vt, WINDOW - 4, 1))
        s_c = jnp.where(bat0, sb[0], sb[1]) + bias_c
        same = (c64 >> 2) == (2 * p + ((r64 & 7) >> 2))
        dn = t64 - (c64 & 3)
        bias_n = jnp.where(same & (dn >= 0), -slope * dn.astype(F32), NEG_INF)
        s_n = _dot(lhs, kt_pad_b) + bias_n
        return s_c, s_n

    def pair_finish(p, s_c, s_n):
        r0 = p * 8
        m = jnp.maximum(jnp.maximum(jnp.max(s_c, axis=-1, keepdims=True), jnp.max(s_n, axis=-1, keepdims=True)),
                        sinkcol)
        pc = jnp.exp(s_c - m)
        pn = jnp.exp(s_n - m)
        l = jnp.sum(pc, axis=-1, keepdims=True) + jnp.sum(pn, axis=-1, keepdims=True) + jnp.exp(sinkcol - m)
        pcb = pc.astype(BF16)
        ob = [_dot_nt(pcb, cv_ref[2 * p + bb].astype(BF16)) for bb in range(2)]
        o = (jnp.where(bat0, ob[0], ob[1]) + _dot_nt(pn.astype(BF16), vt_pad_b)) * (1.0 / l)
        for c in range(N_Q_HEADS // 2):
            hk = c // 2
            ev = o[16 * c:16 * c + 8, :]
            od = o[16 * c + 8:16 * c + 16, :]
            if hk == 0:
                od = pltpu.roll(od, HEAD_DIM, 1)
            else:
                ev = pltpu.roll(ev, HEAD_DIM, 1)
            yatt_s[r0:r0 + 8, c * 128:(c + 1) * 128] = jnp.where(lane_lo, ev, od)

    n_pairs = S_NB // 2
    cur = pair_scores(0)
    for p in range(n_pairs):
        nxt = pair_scores(p + 1) if p + 1 < n_pairs else None
        pair_finish(p, *cur)
        cur = nxt

    ymix = jnp.concatenate([y_lru, yatt_s[...]], axis=1).astype(BF16)
    x1_ref[...] = x + _dot(ymix, wout_ref[...])


def _load_mem_heads(ref, b):
    return jnp.concatenate([ref[b, pl.ds(h, MEM_LEN, stride=N_MEM_HEADS), :] for h in range(N_MEM_HEADS)],
                           axis=1).astype(BF16)


def _sample_cross_kernel(x1_ref, mk_ref, mv_ref, gc_ref, wq_ref, wo_ref, x2_ref, q_s, oc_s):
    x1 = x1_ref[...]
    q_s[...] = _dot(_rmsnorm(x1, gc_ref[...]).astype(BF16), wq_ref[...])

    lane_head = lax.broadcasted_iota(jnp.int32, (8, D_MEM), 1) // MEM_HEAD_DIM
    bat0 = (lax.broadcasted_iota(jnp.int32, (32, D_MEM), 0) & 7) < 4
    bat0_s = (lax.broadcasted_iota(jnp.int32, (32, MEM_LEN), 0) & 7) < 4

    def scores(p):
        q8 = q_s[p * 8:(p + 1) * 8, :]
        lhs = jnp.concatenate([jnp.where(lane_head == h, q8, 0.0) for h in range(N_MEM_HEADS)],
                              axis=0).astype(BF16)
        sb = [_dot_nt(lhs, _load_mem_heads(mk_ref, 2 * p + bb)) for bb in range(2)]
        return jnp.where(bat0_s, sb[0], sb[1]) * MEM_SCALE

    n_pairs = C_NB // 2
    s_next = scores(0)
    for p in range(n_pairs):
        s = s_next
        if p + 1 < n_pairs:
            s_next = scores(p + 1)
        m = jnp.max(s, axis=-1, keepdims=True)
        pe = jnp.exp(s - m)
        l = jnp.sum(pe, axis=-1, keepdims=True)
        pb = pe.astype(BF16)
        ob = [_dot(pb, _load_mem_heads(mv_ref, 2 * p + bb)) for bb in range(2)]
        o = jnp.where(bat0, ob[0], ob[1]) * (1.0 / l)
        out = jnp.where(lane_head == 0, o[0:8], 0.0)
        for h in range(1, N_MEM_HEADS):
            out = out + jnp.where(lane_head == h, o[h * 8:(h + 1) * 8], 0.0)
        oc_s[p * 8:(p + 1) * 8, :] = out

    x2_ref[...] = x1 + _dot(oc_s[...].astype(BF16), wo_ref[...])


def _sample_ffn_kernel(x2_ref, prev_ref, gf_ref, wg_ref, wu_ref, wfc_ref, bfc_ref, wd_ref, gfin_ref,
                       y_ref, gate_ref, slab_s, p1_s, p2_s):
    R = x2_ref.shape[0]
    nb = R // 4
    nl = FF_CHUNK // 128
    x2 = x2_ref[...]
    n3 = _rmsnorm(x2, gf_ref[...]).astype(BF16)
    tmod = lax.broadcasted_iota(jnp.int32, (R, FF_CHUNK), 0) & 3
    p1_s[...] = jnp.zeros(p1_s.shape, F32)
    p2_s[...] = jnp.zeros(p2_s.shape, F32)

    def gprev(c, g):
        for l in range(nl):
            cols = slice(c * FF_CHUNK + l * 128, c * FF_CHUNK + (l + 1) * 128)
            s0 = prev_ref[0, :, cols]
            s1 = prev_ref[1, :, cols]
            p1_s[l, pl.ds(0, nb, stride=4), :] = s1
            p2_s[l, pl.ds(0, nb, stride=4), :] = s0
            p2_s[l, pl.ds(1, nb, stride=4), :] = s1
        p1 = jnp.concatenate([p1_s[l] for l in range(nl)], axis=1)
        p2 = jnp.concatenate([p2_s[l] for l in range(nl)], axis=1)
        g1 = jnp.where(tmod >= 1, pltpu.roll(g, 1, 0), p1)
        g2 = jnp.where(tmod >= 2, pltpu.roll(g, 2, 0), p2)
        return g2, g1

    def on_gate(c, g):
        for l in range(nl):
            slab_s[l] = g[:, l * 128:(l + 1) * 128]
        for tt in range(2):
            gate_ref[tt, :, c * FF_CHUNK:(c + 1) * FF_CHUNK] = jnp.concatenate(
                [slab_s[l, pl.ds(2 + tt, nb, stride=4), :] for l in range(nl)], axis=1)

    acc = _ffn_chunks(n3, x2, gprev, wg_ref, wu_ref, wfc_ref, bfc_ref, wd_ref, on_gate)
    y_ref[...] = _rmsnorm(acc, gfin_ref[...])


def _const_spec(shape, grid_rank):
    zeros = (0,) * len(shape)
    if grid_rank == 1:
        return pl.BlockSpec(shape, lambda i: zeros, pipeline_mode=pl.Buffered(1))
    return pl.BlockSpec(shape, lambda i, j: zeros, pipeline_mode=pl.Buffered(1))


def _block_diag4(w):
    eye = jnp.eye(4, dtype=w.dtype)
    return (w[:, :, None, :] * eye[:, None, :, None]).reshape(4 * LRU_BLOCK, 4 * LRU_BLOCK)


def _params(sem):
    return pltpu.CompilerParams(dimension_semantics=sem, vmem_limit_bytes=VMEM_LIMIT)


def kernel(x_prompt, x_sample, cache_swa_k, cache_swa_v, cache_mem_k, cache_mem_v, state_lru_conv, state_lru_h, state_ffn_conv, mem_prompt, g_mix, w_in, w_lru_conv, b_lru_conv, w_lru_a, b_lru_a, w_lru_x, b_lru_x, lru_lambda, attn_sinks, w_out, g_cross, g_mem, w_mem_q, w_mem_k, w_mem_v, w_mem_o, g_ffn, w_ffn_gate, w_ffn_up, w_ffn_conv, b_ffn_conv, w_ffn_down, g_final):
    B, T, _ = x_prompt.shape
    NB = x_sample.shape[0]
    NT = T // TM
    SR = NB * 4

    win = w_in[0].astype(BF16)
    wout = w_out[0].astype(BF16)
    wq = w_mem_q[0].astype(BF16)
    wk = w_mem_k[0].astype(BF16)
    wv = w_mem_v[0].astype(BF16)
    wo = w_mem_o[0].astype(BF16)
    wg = w_ffn_gate[0].astype(BF16)
    wu = w_ffn_up[0].astype(BF16)
    wd = w_ffn_down[0].astype(BF16)
    wax = jnp.stack([
        jnp.concatenate([_block_diag4(w_lru_a[0, 4 * gi:4 * gi + 4]), _block_diag4(w_lru_x[0, 4 * gi:4 * gi + 4])],
                        axis=1) for gi in range(2)]).astype(BF16)
    gmix, gcross, gmem, gffn = g_mix, g_cross, g_mem, g_ffn
    gfin = g_final.reshape(1, D_MODEL)
    wconv, bconv = w_lru_conv[0], b_lru_conv
    ba, bx, lam = b_lru_a, b_lru_x, lru_lambda
    wfc, bfc = w_ffn_conv[0], b_ffn_conv
    sinks = attn_sinks[0]
    smem = pl.BlockSpec(memory_space=pltpu.SMEM)

    mk, mv, mkb, mvb = pl.pallas_call(
        _mem_kv_kernel,
        grid=(B,),
        in_specs=[pl.BlockSpec((1, MEM_LEN, D_MODEL), lambda b: (b, 0, 0)),
                  _const_spec((1, D_MODEL), 1), _const_spec((D_MODEL, D_MEM), 1), _const_spec((D_MODEL, D_MEM), 1)],
        out_specs=[pl.BlockSpec((1, MEM_LEN, D_MEM), lambda b: (b, 0, 0))] * 4,
        out_shape=[jax.ShapeDtypeStruct((B, MEM_LEN, D_MEM), F32)] * 2
        + [jax.ShapeDtypeStruct((B, MEM_LEN, D_MEM), BF16)] * 2,
        compiler_params=_params(("arbitrary",)),
        name="mem_kv",
    )(mem_prompt, gmem, wk, wv)

    x1p, pk, pv, pconv8, ph8 = pl.pallas_call(
        _prompt_mixer_kernel,
        grid=(B, NT),
        in_specs=[smem,
                  pl.BlockSpec((1, TM, D_MODEL), lambda b, t: (b, t, 0)),
                  _const_spec((1, D_MODEL), 2), _const_spec((D_MODEL, D_IN), 2),
                  _const_spec((LRU_CONV_W, D_LRU), 2), _const_spec((1, D_LRU), 2),
                  _const_spec((2, 256, 512), 2), _const_spec((1, D_LRU), 2), _const_spec((1, D_LRU), 2),
                  _const_spec((1, D_LRU), 2), _const_spec((D_MODEL, D_MODEL), 2)],
        out_specs=[pl.BlockSpec((1, TM, D_MODEL), lambda b, t: (b, t, 0)),
                   pl.BlockSpec((1, WINDOW, D_KV), lambda b, t: (b, 0, 0)),
                   pl.BlockSpec((1, WINDOW, D_KV), lambda b, t: (b, 0, 0)),
                   pl.BlockSpec((1, 8, D_LRU), lambda b, t: (b, 0, 0)),
                   pl.BlockSpec((1, 8, D_LRU), lambda b, t: (b, 0, 0))],
        out_shape=[jax.ShapeDtypeStruct((B, T, D_MODEL), F32),
                   jax.ShapeDtypeStruct((B, WINDOW, D_KV), F32),
                   jax.ShapeDtypeStruct((B, WINDOW, D_KV), F32),
                   jax.ShapeDtypeStruct((B, 8, D_LRU), F32),
                   jax.ShapeDtypeStruct((B, 8, D_LRU), F32)],
        scratch_shapes=[pltpu.VMEM((TM, D_IN), F32),
                        pltpu.VMEM((D_LRU // 128, HR, 128), F32),
                        pltpu.VMEM((D_LRU // 128, HR, 128), F32),
                        pltpu.VMEM((24, D_LRU), F32),
                        pltpu.VMEM((TM, D_LRU), F32),
                        pltpu.VMEM((TM, D_LRU), F32),
                        pltpu.VMEM((HR, D_LRU), F32),
                        pltpu.VMEM((TM, D_LRU), F32),
                        pltpu.VMEM((8, D_LRU), F32),
                        pltpu.VMEM((8, TM + WINDOW, D_KV), BF16),
                        pltpu.VMEM((TM, D_MODEL), BF16)],
        compiler_params=_params(("arbitrary", "arbitrary")),
        name="prompt_mixer",
    )(sinks, x_prompt, gmix, win, wconv, bconv, wax, ba, bx, lam, wout)

    y_prompt, pffn8 = pl.pallas_call(
        _prompt_ffn_kernel,
        grid=(B, T // TF),
        in_specs=[pl.BlockSpec((1, TF, D_MODEL), lambda b, t: (b, t, 0)),
                  pl.BlockSpec((1, MEM_LEN, D_MEM), lambda b, t: (b, 0, 0)),
                  pl.BlockSpec((1, MEM_LEN, D_MEM), lambda b, t: (b, 0, 0)),
                  _const_spec((1, D_MODEL), 2), _const_spec((D_MODEL, D_MEM), 2), _const_spec((D_MEM, D_MODEL), 2),
                  _const_spec((1, D_MODEL), 2), _const_spec((D_MODEL, D_FF), 2), _const_spec((D_MODEL, D_FF), 2),
                  _const_spec((FFN_CONV_W, D_FF), 2), _const_spec((1, D_FF), 2), _const_spec((D_FF, D_MODEL), 2),
                  _const_spec((1, D_MODEL), 2)],
        out_specs=[pl.BlockSpec((1, TF, D_MODEL), lambda b, t: (b, t, 0)),
                   pl.BlockSpec((1, 8, D_FF), lambda b, t: (b, 0, 0))],
        out_shape=[jax.ShapeDtypeStruct((B, T, D_MODEL), F32),
                   jax.ShapeDtypeStruct((B, 8, D_FF), F32)],
        scratch_shapes=[pltpu.VMEM((TF, D_MEM), BF16),
                        pltpu.VMEM((TF + 8, FF_CHUNK), F32),
                        pltpu.VMEM((8, D_FF), F32)],
        compiler_params=_params(("arbitrary", "arbitrary")),
        name="prompt_ffn",
    )(x1p, mkb, mvb, gcross, wq, wo, gffn, wg, wu, wfc, bfc, wd, gfin)

    xs = x_sample.reshape(SR, D_MODEL)
    conv_prev4 = jnp.pad(state_lru_conv[0], ((0, 0), (0, 1), (0, 0))).reshape(SR, D_LRU)
    h0rep = jnp.repeat(state_lru_h[0], 4, axis=0)
    ck = jnp.transpose(cache_swa_k[0], (0, 2, 3, 1)).reshape(NB, D_KV, WINDOW)
    cv = jnp.transpose(cache_swa_v[0], (0, 2, 3, 1)).reshape(NB, D_KV, WINDOW)
    wkt = jnp.transpose(w_in[0][:, 1536:1664]).astype(BF16)
    wvt = jnp.transpose(w_in[0][:, 1664:1792]).astype(BF16)
    row_spec = lambda w: pl.BlockSpec((S_ROWS, w), lambda i: (i, 0))
    cache_spec = pl.BlockSpec((S_NB, D_KV, WINDOW), lambda i: (i, 0, 0))
    x1s, sk, sv, xr_s, h_s = pl.pallas_call(
        _sample_mixer_kernel,
        grid=(NB // S_NB,),
        in_specs=[smem, row_spec(D_MODEL), row_spec(D_LRU), row_spec(D_LRU), cache_spec, cache_spec,
                  _const_spec((1, D_MODEL), 1), _const_spec((D_MODEL, D_IN), 1),
                  _const_spec((LRU_CONV_W, D_LRU), 1), _const_spec((1, D_LRU), 1),
                  _const_spec((2, 256, 512), 1), _const_spec((1, D_LRU), 1), _const_spec((1, D_LRU), 1),
                  _const_spec((1, D_LRU), 1), _const_spec((D_MODEL, D_MODEL), 1),
                  _const_spec((D_KV, D_MODEL), 1), _const_spec((D_KV, D_MODEL), 1)],
        out_specs=[row_spec(D_MODEL), cache_spec, cache_spec, row_spec(D_LRU), row_spec(D_LRU)],
        out_shape=[jax.ShapeDtypeStruct((SR, D_MODEL), F32),
                   jax.ShapeDtypeStruct((NB, D_KV, WINDOW), F32),
                   jax.ShapeDtypeStruct((NB, D_KV, WINDOW), F32),
                   jax.ShapeDtypeStruct((SR, D_LRU), F32),
                   jax.ShapeDtypeStruct((SR, D_LRU), F32)],
        scratch_shapes=[pltpu.VMEM((S_ROWS, D_ATTN), F32),
                        pltpu.VMEM((S_ROWS, D_ATTN), F32)],
        compiler_params=_params(("arbitrary",)),
        name="sample_mixer",
    )(sinks, xs, conv_prev4, h0rep, ck, cv, gmix, win, wconv, bconv, wax, ba, bx, lam, wout, wkt, wvt)

    cmk = cache_mem_k.reshape(NB, MEM_LEN * N_MEM_HEADS, MEM_HEAD_DIM)
    cmv = cache_mem_v.reshape(NB, MEM_LEN * N_MEM_HEADS, MEM_HEAD_DIM)
    crow = pl.BlockSpec((C_ROWS, D_MODEL), lambda i: (i, 0))
    cmem = pl.BlockSpec((C_NB, MEM_LEN * N_MEM_HEADS, MEM_HEAD_DIM), lambda i: (i, 0, 0))
    x2s = pl.pallas_call(
        _sample_cross_kernel,
        grid=(NB // C_NB,),
        in_specs=[crow, cmem, cmem, _const_spec((1, D_MODEL), 1), _const_spec((D_MODEL, D_MEM), 1),
                  _const_spec((D_MEM, D_MODEL), 1)],
        out_specs=crow,
        out_shape=jax.ShapeDtypeStruct((SR, D_MODEL), F32),
        scratch_shapes=[pltpu.VMEM((C_ROWS, D_MEM), F32), pltpu.VMEM((C_ROWS, D_MEM), F32)],
        compiler_params=_params(("arbitrary",)),
        name="sample_cross",
    )(x1s, cmk, cmv, gcross, wq, wo)

    ffn_prev_tm = jnp.transpose(state_ffn_conv[0], (1, 0, 2))
    slab = pltpu.VMEM((FF_CHUNK // 128, SR, 128), F32)
    y_s, gate_tm = pl.pallas_call(
        _sample_ffn_kernel,
        grid=(1,),
        in_specs=[_const_spec((SR, D_MODEL), 1), _const_spec((FFN_CONV_W - 1, NB, D_FF), 1),
                  _const_spec((1, D_MODEL), 1), _const_spec((D_MODEL, D_FF), 1), _const_spec((D_MODEL, D_FF), 1),
                  _const_spec((FFN_CONV_W, D_FF), 1), _const_spec((1, D_FF), 1), _const_spec((D_FF, D_MODEL), 1),
                  _const_spec((1, D_MODEL), 1)],
        out_specs=[pl.BlockSpec((SR, D_MODEL), lambda i: (0, 0)),
                   pl.BlockSpec((FFN_CONV_W - 1, NB, D_FF), lambda i: (0, 0, 0))],
        out_shape=[jax.ShapeDtypeStruct((SR, D_MODEL), F32),
                   jax.ShapeDtypeStruct((FFN_CONV_W - 1, NB, D_FF), F32)],
        scratch_shapes=[slab, slab, slab],
        compiler_params=_params(("arbitrary",)),
        name="sample_ffn",
    )(x2s, ffn_prev_tm, gffn, wg, wu, wfc, bfc, wd, gfin)

    p_swa_k = pk.reshape(1, B, WINDOW, N_KV_HEADS, HEAD_DIM)
    p_swa_v = pv.reshape(1, B, WINDOW, N_KV_HEADS, HEAD_DIM)
    p_mem_k = mk.reshape(1, B, MEM_LEN, N_MEM_HEADS, MEM_HEAD_DIM)
    p_mem_v = mv.reshape(1, B, MEM_LEN, N_MEM_HEADS, MEM_HEAD_DIM)
    p_lru_conv = pconv8[None, :, 8 - (LRU_CONV_W - 1):, :]
    p_lru_h = ph8[None, :, 0, :]
    p_ffn_conv = pffn8[None, :, 8 - (FFN_CONV_W - 1):, :]
    y_sample = y_s.reshape(NB, 4, D_MODEL)
    s_swa_k = jnp.transpose(sk.reshape(NB, N_KV_HEADS, HEAD_DIM, WINDOW), (0, 3, 1, 2))[None]
    s_swa_v = jnp.transpose(sv.reshape(NB, N_KV_HEADS, HEAD_DIM, WINDOW), (0, 3, 1, 2))[None]
    s_lru_conv = xr_s.reshape(NB, 4, D_LRU)[None, :, 1:, :]
    s_lru_h = h_s.reshape(NB, 4, D_LRU)[None, :, 3, :]
    s_ffn_conv = jnp.transpose(gate_tm, (1, 0, 2))[None]
    return (y_prompt, y_sample, p_swa_k, p_swa_v, p_mem_k, p_mem_v, p_lru_conv, p_lru_h, p_ffn_conv,
            s_swa_k, s_swa_v, s_lru_conv, s_lru_h, s_ffn_conv)
```

```python
import functools

import numpy as np
import jax
import jax.numpy as jnp
from jax import lax
from jax.experimental import pallas as pl
from jax.experimental.pallas import tpu as pltpu

D_MODEL = 1024
D_LRU = 512
LRU_BLOCKS = 8
LRU_BLOCK = 64
LRU_CONV_W = 4
LRU_C = 8.0
N_Q_HEADS = 8
N_KV_HEADS = 2
HEAD_DIM = 64
D_ATTN = 512
D_KV = 128
WINDOW = 128
D_IN = 1792
MEM_LEN = 256
N_MEM_HEADS = 4
MEM_HEAD_DIM = 128
D_MEM = 512
D_FF = 3072
FFN_CONV_W = 3
EPS = 1e-6
NEG_INF = -1e30

F32 = jnp.float32
BF16 = jnp.bfloat16

SLOPES = [float(2.0 ** (-8.0 * (i + 1) / N_Q_HEADS)) for i in range(N_Q_HEADS)]
ATTN_SCALE = HEAD_DIM ** -0.5
MEM_SCALE = MEM_HEAD_DIM ** -0.5
LOG2E = 1.4426950408889634
F32_TINY = 1.1754944e-38

TM = 1024
HR = 512
SEG = HR // 8
TF = 512
FF_CHUNK = 1024
S_NB = 16
S_ROWS = 4 * S_NB
C_NB = 4
C_ROWS = 4 * C_NB
VMEM_LIMIT = 56 * 1024 * 1024


def _dot(a, b):
    return jnp.dot(a, b, preferred_element_type=F32)


def _dot_nt(a, b):
    return lax.dot_general(a, b, (((1,), (1,)), ((), ())), preferred_element_type=F32)


def _rmsnorm(x, g):
    ms = jnp.mean(x * x, axis=-1, keepdims=True)
    return x * lax.rsqrt(ms + EPS) * g


def _gelu(x):
    c = 0.7978845608028654
    return x * (0.5 * (1.0 + jnp.tanh(c * (x + 0.044715 * (x * x * x)))))


def _sigmoid(x):
    return 1.0 / (1.0 + jnp.exp(-x))


def _softplus(x):
    return jnp.maximum(x, 0.0) + jnp.log1p(jnp.exp(-jnp.abs(x)))


def _lru_gates(xc, wax_ref, ba, bx, lam):
    xcb = xc.astype(BF16)
    pa, px = [], []
    for gi in range(2):
        pre = _dot(xcb[:, gi * 256:(gi + 1) * 256], wax_ref[gi])
        pa.append(pre[:, :256])
        px.append(pre[:, 256:])
    r = _sigmoid(jnp.concatenate(pa, axis=1) + ba)
    i = _sigmoid(jnp.concatenate(px, axis=1) + bx)
    log_a = (-LRU_C * _softplus(-lam)) * r
    a = jnp.exp(log_a)
    om = -jnp.tanh(log_a) * (a * a + 1.0)
    b = (om * lax.rsqrt(jnp.maximum(om, F32_TINY))) * (i * xc)
    return a, b


def _head_variants(t):
    lo = lax.broadcasted_iota(jnp.int32, t.shape, 1) < HEAD_DIM
    tr = pltpu.roll(t, HEAD_DIM, 1)
    z = jnp.zeros_like(t)
    x0 = jnp.where(lo, t, z)
    y0 = jnp.where(lo, z, tr)
    x1 = jnp.where(lo, tr, z)
    y1 = jnp.where(lo, z, t)
    return [v.astype(BF16) for v in (x0, y0, x1, y1)]


def _mem_kv_kernel(mem_ref, g_ref, wk_ref, wv_ref, mk_ref, mv_ref, mkb_ref, mvb_ref):
    n = _rmsnorm(mem_ref[0], g_ref[...]).astype(BF16)
    mk = _dot(n, wk_ref[...])
    mv = _dot(n, wv_ref[...])
    mk_ref[0] = mk
    mv_ref[0] = mv
    mkb_ref[0] = mk.astype(BF16)
    mvb_ref[0] = mv.astype(BF16)


def _prompt_mixer_kernel(sinks_ref, x_ref, g_ref, win_ref, wconv_ref, bconv_ref, wax_ref, ba_ref, bx_ref,
                         lam_ref, wout_ref,
                         x1_ref, pk_ref, pv_ref, pconv_ref, ph_ref,
                         proj_s, pin_s, pout_s, xtail_s, a_s, b_s, hl_s, h_s, hc_s, kv_s, ymix_s):
    t = pl.program_id(1)

    @pl.when(t == 0)
    def _():
        xtail_s[...] = jnp.zeros((24, D_LRU), F32)
        hc_s[...] = jnp.zeros((8, D_LRU), F32)
        kv_s[:, 0:WINDOW, :] = jnp.zeros((8, WINDOW, D_KV), BF16)

    def sec(i):
        return slice(i * HR, (i + 1) * HR)

    def in_proj(i):
        n1 = _rmsnorm(x_ref[0, sec(i), :], g_ref[...]).astype(BF16)
        proj_s[sec(i), :] = _dot(n1, win_ref[...])

    def seg_rows(j):
        return pl.ds(64 * (j % (SEG // 8)) + j // (SEG // 8), 8, stride=8)

    row8 = lax.broadcasted_iota(jnp.int32, (8, D_LRU), 0)

    def lru_gates(i):
        r0 = i * HR
        for j in range(HR // 8):
            for l in range(D_LRU // 128):
                pin_s[l, seg_rows(j), :] = proj_s[r0 + 8 * j:r0 + 8 * j + 8, l * 128:(l + 1) * 128]
        xs = jnp.concatenate([pin_s[l] for l in range(D_LRU // 128)], axis=1)
        tail = xs[HR - 24:, :]
        prev = xtail_s[...]
        xtail_s[...] = tail
        heads = [jnp.where(row8 == 0, pltpu.roll(prev[8 * q:8 * q + 8, :], 1, 0),
                           pltpu.roll(tail[8 * q:8 * q + 8, :], 1, 0)) for q in range(3)]
        sh1 = jnp.concatenate(heads[2:] + [xs[:HR - 8, :]], axis=0)
        sh2 = jnp.concatenate(heads[1:] + [xs[:HR - 16, :]], axis=0)
        sh3 = jnp.concatenate(heads + [xs[:HR - 24, :]], axis=0)
        wc = wconv_ref[...]
        xc = bconv_ref[...] + wc[0:1] * sh3 + wc[1:2] * sh2 + wc[2:3] * sh1 + wc[3:4] * xs
        a, b = _lru_gates(xc, wax_ref, ba_ref[...], bx_ref[...], lam_ref[...])
        a_s[sec(i), :] = a
        b_s[sec(i), :] = b

    def lru_scan(i, hcar):
        r0 = i * HR
        hl = b_s[r0:r0 + 8, :]
        ac = a_s[r0:r0 + 8, :]
        hl_s[0:8, :] = hl
        for g in range(1, SEG):
            av = a_s[r0 + 8 * g:r0 + 8 * g + 8, :]
            hl = av * hl + b_s[r0 + 8 * g:r0 + 8 * g + 8, :]
            ac = av * ac
            hl_s[8 * g:8 * g + 8, :] = hl
            a_s[r0 + 8 * g:r0 + 8 * g + 8, :] = ac
        hin = hcar
        for s in range(8):
            hend = hl + ac * hin
            if s < 7:
                hin = jnp.where(row8 == s + 1, pltpu.roll(hend, 1, 0), hin)
        hcar = jnp.broadcast_to(hend[7:8, :], (8, D_LRU))
        for g in range(SEG):
            hg = hl_s[8 * g:8 * g + 8, :] + a_s[r0 + 8 * g:r0 + 8 * g + 8, :] * hin
            for l in range(D_LRU // 128):
                pout_s[l, 8 * g:8 * g + 8, :] = hg[:, l * 128:(l + 1) * 128]
        for j in range(HR // 8):
            h_s[r0 + 8 * j:r0 + 8 * j + 8, :] = jnp.concatenate(
                [pout_s[l, seg_rows(j), :] for l in range(D_LRU // 128)], axis=1)
        gate = proj_s[sec(i), D_LRU:2 * D_LRU]
        ymix_s[sec(i), 0:D_LRU] = (h_s[sec(i), :] * _gelu(gate)).astype(BF16)
        return hcar

    def kv_prep(i):
        k = proj_s[sec(i), 1536:1664]
        v = proj_s[sec(i), 1664:1792]
        for n, arr in enumerate(_head_variants(k) + _head_variants(v)):
            kv_s[n, WINDOW + i * HR:WINDOW + (i + 1) * HR, :] = arr

    qi = lax.broadcasted_iota(jnp.int32, (WINDOW, 2 * WINDOW), 0)
    kj = lax.broadcasted_iota(jnp.int32, (WINDOW, 2 * WINDOW), 1)
    dist = qi + WINDOW - kj
    valid = (dist >= 0) & (dist < WINDOW)
    valid0 = valid & (kj >= jnp.where(t == 0, WINDOW, 0))
    distf = dist.astype(F32)
    bias = [jnp.where(valid, -SLOPES[h] * distf, NEG_INF) for h in range(N_Q_HEADS)]
    bias0 = [jnp.where(valid0, -SLOPES[h] * distf, NEG_INF) for h in range(N_Q_HEADS)]
    lane_lo = lax.broadcasted_iota(jnp.int32, (WINDOW, 2 * HEAD_DIM), 1) < HEAD_DIM

    r512 = lax.broadcasted_iota(jnp.int32, (4 * WINDOW, 2 * HEAD_DIM), 0)
    l512 = lax.broadcasted_iota(jnp.int32, (4 * WINDOW, 2 * HEAD_DIM), 1)
    ones_cols = jnp.where((r512 < 2 * WINDOW) == (l512 < HEAD_DIM), 1.0, 0.0).astype(BF16)
    n_iter = (HR // WINDOW) * 4

    def qk(n):
        j, c = divmod(n, 4)
        hk = c // 2
        rows = slice(j * WINDOW, (j + 1) * WINDOW)
        win = slice(j * WINDOW, (j + 2) * WINDOW)
        qc = (proj_s[rows, 1024 + c * 128:1024 + (c + 1) * 128] * ATTN_SCALE).astype(BF16)
        kcat = jnp.concatenate([kv_s[2 * hk, win, :], kv_s[2 * hk + 1, win, :]], axis=0)
        return _dot_nt(qc, kcat)

    def softmax_pv(n, s):
        j, c = divmod(n, 4)
        hk = c // 2
        rows = slice(j * WINDOW, (j + 1) * WINDOW)
        win = slice(j * WINDOW, (j + 2) * WINDOW)
        vcat = jnp.concatenate([kv_s[4 + 2 * hk, win, :], kv_s[5 + 2 * hk, win, :]], axis=0)
        vaug = jnp.concatenate([vcat, ones_cols], axis=1)
        ps, es = [], []
        for half in range(2):
            h = 2 * c + half
            sink = sinks_ref[h]
            sh = s[:, half * 256:(half + 1) * 256] + (bias0[h] if j == 0 else bias[h])
            m = jnp.maximum(jnp.max(sh, axis=-1, keepdims=True), sink)
            ps.append(jnp.exp(sh - m).astype(BF16))
            es.append(jnp.exp(sink - m))
        oa = _dot(jnp.concatenate(ps, axis=1), vaug)
        den = oa[:, 128:256] + jnp.where(lane_lo, es[0], es[1])
        ymix_s[rows, D_LRU + c * 128:D_LRU + (c + 1) * 128] = (oa[:, 0:128] * (1.0 / den)).astype(BF16)

    def attention(i):
        depth = 2
        base = i * n_iter
        pend = [qk(base + n) for n in range(depth)]
        for n in range(n_iter):
            s = pend.pop(0)
            if n + depth < n_iter:
                pend.append(qk(base + n + depth))
            softmax_pv(base + n, s)

    def out_proj(i):
        x1_ref[0, sec(i), :] = x_ref[0, sec(i), :] + _dot(ymix_s[sec(i), :], wout_ref[...])

    n_sec = TM // HR
    h = hc_s[...]
    in_proj(0)
    lru_gates(0)
    for i in range(n_sec):
        if i + 1 < n_sec:
            in_proj(i + 1)
        h = lru_scan(i, h)
        kv_prep(i)
        attention(i)
        if i + 1 < n_sec:
            lru_gates(i + 1)
        out_proj(i)

    hc_s[...] = h
    ph_ref[0] = h
    pconv_ref[0] = proj_s[TM - 8:, 0:D_LRU]
    pk_ref[0] = proj_s[TM - WINDOW:, 1536:1664]
    pv_ref[0] = proj_s[TM - WINDOW:, 1664:1792]
    kv_s[:, 0:WINDOW, :] = kv_s[:, TM:TM + WINDOW, :]


def _ffn_chunks(n3, acc, gprev_fn, wg_ref, wu_ref, wfc_ref, bfc_ref, wd_ref, on_gate):
    nc = D_FF // FF_CHUNK

    def up(c):
        cs = slice(c * FF_CHUNK, (c + 1) * FF_CHUNK)
        return _dot(n3, wg_ref[:, cs]), _dot(n3, wu_ref[:, cs])

    nxt = up(0)
    for c in range(nc):
        cs = slice(c * FF_CHUNK, (c + 1) * FF_CHUNK)
        g, u = nxt
        if c + 1 < nc:
            nxt = up(c + 1)
        g2, g1 = gprev_fn(c, g)
        on_gate(c, g)
        wfc = wfc_ref[:, cs]
        conv = bfc_ref[:, cs] + wfc[0:1] * g2 + wfc[1:2] * g1 + wfc[2:3] * g
        hmid = (_gelu(conv) * u).astype(BF16)
        acc = acc + _dot(hmid, wd_ref[cs, :])
    return acc


def _prompt_ffn_kernel(x1_ref, mk_ref, mv_ref, xs1_ref, cmk_ref, cmv_ref, gc_ref, wq_ref, wo_ref, gf_ref,
                       wg_ref, wu_ref, wfc_ref, bfc_ref, wd_ref, gfin_ref,
                       y_ref, pffn_ref, xs2_ref,
                       oc_s, gbuf_s, gcar_s, sq_s, soc_s):
    t = pl.program_id(1)

    @pl.when(t == 0)
    def _():
        gcar_s[...] = jnp.zeros((8, D_FF), F32)

    for stage in _sample_cross_stages(xs1_ref, cmk_ref, cmv_ref, gc_ref, wq_ref, wo_ref, xs2_ref, sq_s, soc_s):
        stage()

    x1 = x1_ref[0]
    qc = _dot(_rmsnorm(x1, gc_ref[...]).astype(BF16), wq_ref[...]).astype(BF16)
    hsl = [slice(h * MEM_HEAD_DIM, (h + 1) * MEM_HEAD_DIM) for h in range(N_MEM_HEADS)]
    ss = [_dot_nt(qc[:, hs], mk_ref[0, :, hs]) for hs in hsl]
    for h, hs in enumerate(hsl):
        s = ss[h]
        m = jnp.max(s, axis=-1, keepdims=True)
        p = jnp.exp2((s - m) * (MEM_SCALE * LOG2E))
        l = jnp.sum(p, axis=-1, keepdims=True)
        o = _dot(p.astype(BF16), mv_ref[0, :, hs]) * (1.0 / l)
        oc_s[:, hs] = o.astype(BF16)
    x2 = x1 + _dot(oc_s[...], wo_ref[...])
    n3 = _rmsnorm(x2, gf_ref[...]).astype(BF16)

    def gprev(c, g):
        cs = slice(c * FF_CHUNK, (c + 1) * FF_CHUNK)
        gbuf_s[0:8, :] = gcar_s[:, cs]
        gbuf_s[8:TF + 8, :] = g
        return gbuf_s[6:6 + TF, :], gbuf_s[7:7 + TF, :]

    def on_gate(c, g):
        cs = slice(c * FF_CHUNK, (c + 1) * FF_CHUNK)
        tail = g[TF - 8:, :]
        gcar_s[:, cs] = tail
        pffn_ref[0, :, cs] = tail

    acc = _ffn_chunks(n3, x2, gprev, wg_ref, wu_ref, wfc_ref, bfc_ref, wd_ref, on_gate)
    y_ref[0] = _rmsnorm(acc, gfin_ref[...])


def _sample_mixer_kernel(sinks_ref, x_ref, prev4_ref, h0_ref, ck_ref, cv_ref, g_ref, win_ref, wconv_ref, bconv_ref,
                         wax_ref, ba_ref, bx_ref, lam_ref, wout_ref, wkt_ref, wvt_ref,
                         x1_ref, sk_ref, sv_ref, xr_ref, h_ref,
                         q_s, yatt_s):
    R = S_ROWS
    x = x_ref[...]
    n1 = _rmsnorm(x, g_ref[...]).astype(BF16)
    proj = _dot(n1, win_ref[...])
    xr = proj[:, 0:D_LRU]
    gate = proj[:, D_LRU:2 * D_LRU]
    q_s[...] = proj[:, 1024:1536] * ATTN_SCALE
    xr_ref[...] = xr

    tmod = lax.broadcasted_iota(jnp.int32, (R, D_LRU), 0) & 3
    prev4 = prev4_ref[...]
    xs1 = jnp.where(tmod >= 1, pltpu.roll(xr, 1, 0), pltpu.roll(prev4, R - 2, 0))
    xs2 = jnp.where(tmod >= 2, pltpu.roll(xr, 2, 0), pltpu.roll(prev4, R - 1, 0))
    xs3 = jnp.where(tmod >= 3, pltpu.roll(xr, 3, 0), prev4)
    wc = wconv_ref[...]
    xc = bconv_ref[...] + wc[0:1] * xs3 + wc[1:2] * xs2 + wc[2:3] * xs1 + wc[3:4] * xr

    a, b = _lru_gates(xc, wax_ref, ba_ref[...], bx_ref[...], lam_ref[...])
    for s in (1, 2):
        m = tmod >= s
        a_sh = pltpu.roll(a, s, 0)
        b_sh = pltpu.roll(b, s, 0)
        b = jnp.where(m, a * b_sh + b, b)
        a = jnp.where(m, a * a_sh, a)
    h = a * h0_ref[...] + b
    h_ref[...] = h
    y_lru = h * _gelu(gate)

    kt_new = _dot_nt(wkt_ref[...], n1)
    vt_new = _dot_nt(wvt_ref[...], n1)
    zpad = jnp.zeros((D_KV, WINDOW - R), F32)
    kt_pad = jnp.concatenate([kt_new, zpad], axis=1)
    vt_pad = jnp.concatenate([vt_new, zpad], axis=1)
    kt_pad_b = kt_pad.astype(BF16)
    vt_pad_b = vt_pad.astype(BF16)

    r64 = lax.broadcasted_iota(jnp.int32, (8 * N_Q_HEADS, WINDOW), 0)
    c64 = lax.broadcasted_iota(jnp.int32, (8 * N_Q_HEADS, WINDOW), 1)
    t64 = r64 & 3
    slope = jnp.zeros((8 * N_Q_HEADS, WINDOW), F32)
    sinkcol = jnp.zeros((8 * N_Q_HEADS, 1), F32)
    for hq in range(N_Q_HEADS):
        slope = jnp.where((r64 >> 3) == hq, SLOPES[hq], slope)
        sinkcol = jnp.where((r64[:, 0:1] >> 3) == hq, sinks_ref[hq], sinkcol)
    bias_c = jnp.where(c64 > t64, -slope * (t64 + WINDOW - c64).astype(F32), NEG_INF)
    bat0 = (r64 & 7) < 4
    lane_lo = lax.broadcasted_iota(jnp.int32, (8, D_KV), 1) < HEAD_DIM
    lane128 = lax.broadcasted_iota(jnp.int32, (D_KV, WINDOW), 1)

    def pair_scores(p):
        r0 = p * 8
        q8 = q_s[r0:r0 + 8, :]
        blocks = []
        for hq in range(N_Q_HEADS):
            chunk = q8[:, (hq // 2) * 128:(hq // 2 + 1) * 128]
            hk = hq // (N_Q_HEADS // N_KV_HEADS)
            src = chunk if (hq % 2) == hk else pltpu.roll(chunk, HEAD_DIM, 1)
            blocks.append(jnp.where(lane_lo == (hk == 0), src, 0.0))
        lhs = jnp.concatenate(blocks, axis=0).astype(BF16)
        sb = []
        for bb in range(2):
            bidx = 2 * p + bb
            kt = ck_ref[bidx]
            vt = cv_ref[bidx]
            sb.append(_dot(lhs, kt.astype(BF16)))
            shift = (WINDOW - 4 - 4 * bidx) % WINDOW
            sk_ref[bidx] = jnp.where(lane128 >= WINDOW - 4, pltpu.roll(kt_pad, shift, 1),
                                     pltpu.roll(kt, WINDOW - 4, 1))
            sv_ref[bidx] = jnp.where(lane128 >= WINDOW - 4, pltpu.roll(vt_pad, shift, 1),
                                     pltpu.roll(vt, WINDOW - 4, 1))
        s_c = jnp.where(bat0, sb[0], sb[1]) + bias_c
        same = (c64 >> 2) == (2 * p + ((r64 & 7) >> 2))
        dn = t64 - (c64 & 3)
        bias_n = jnp.where(same & (dn >= 0), -slope * dn.astype(F32), NEG_INF)
        s_n = _dot(lhs, kt_pad_b) + bias_n
        return s_c, s_n

    def pair_finish(p, s_c, s_n):
        r0 = p * 8
        m = jnp.maximum(jnp.maximum(jnp.max(s_c, axis=-1, keepdims=True), jnp.max(s_n, axis=-1, keepdims=True)),
                        sinkcol)
        pc = jnp.exp(s_c - m)
        pn = jnp.exp(s_n - m)
        l = jnp.sum(pc, axis=-1, keepdims=True) + jnp.sum(pn, axis=-1, keepdims=True) + jnp.exp(sinkcol - m)
        pcb = pc.astype(BF16)
        ob = [_dot_nt(pcb, cv_ref[2 * p + bb].astype(BF16)) for bb in range(2)]
        o = (jnp.where(bat0, ob[0], ob[1]) + _dot_nt(pn.astype(BF16), vt_pad_b)) * (1.0 / l)
        for c in range(N_Q_HEADS // 2):
            hk = c // 2
            ev = o[16 * c:16 * c + 8, :]
            od = o[16 * c + 8:16 * c + 16, :]
            if hk == 0:
                od = pltpu.roll(od, HEAD_DIM, 1)
            else:
                ev = pltpu.roll(ev, HEAD_DIM, 1)
            yatt_s[r0:r0 + 8, c * 128:(c + 1) * 128] = jnp.where(lane_lo, ev, od)

    n_pairs = S_NB // 2
    cur = pair_scores(0)
    for p in range(n_pairs):
        nxt = pair_scores(p + 1) if p + 1 < n_pairs else None
        pair_finish(p, *cur)
        cur = nxt

    ymix = jnp.concatenate([y_lru, yatt_s[...]], axis=1).astype(BF16)
    x1_ref[...] = x + _dot(ymix, wout_ref[...])


def _load_mem_heads(ref, b):
    return jnp.concatenate([ref[b, pl.ds(h, MEM_LEN, stride=N_MEM_HEADS), :] for h in range(N_MEM_HEADS)],
                           axis=1).astype(BF16)


def _sample_cross_stages(x1_ref, mk_ref, mv_ref, gc_ref, wq_ref, wo_ref, x2_ref, q_s, oc_s):
    lane_head = lax.broadcasted_iota(jnp.int32, (8, D_MEM), 1) // MEM_HEAD_DIM
    bat0 = (lax.broadcasted_iota(jnp.int32, (32, D_MEM), 0) & 7) < 4
    bat0_s = (lax.broadcasted_iota(jnp.int32, (32, MEM_LEN), 0) & 7) < 4

    def q_proj():
        q_s[...] = _dot(_rmsnorm(x1_ref[...], gc_ref[...]).astype(BF16), wq_ref[...])

    def scores(p):
        q8 = q_s[p * 8:(p + 1) * 8, :]
        lhs = jnp.concatenate([jnp.where(lane_head == h, q8, 0.0) for h in range(N_MEM_HEADS)],
                              axis=0).astype(BF16)
        sb = [_dot_nt(lhs, _load_mem_heads(mk_ref, 2 * p + bb)) for bb in range(2)]
        return jnp.where(bat0_s, sb[0], sb[1]) * MEM_SCALE

    def attend():
        ss = [scores(p) for p in range(C_NB // 2)]
        for p, s in enumerate(ss):
            finish(p, s)

    def finish(p, s):
        m = jnp.max(s, axis=-1, keepdims=True)
        pe = jnp.exp(s - m)
        l = jnp.sum(pe, axis=-1, keepdims=True)
        pb = pe.astype(BF16)
        ob = [_dot(pb, _load_mem_heads(mv_ref, 2 * p + bb)) for bb in range(2)]
        o = jnp.where(bat0, ob[0], ob[1]) * (1.0 / l)
        out = jnp.where(lane_head == 0, o[0:8], 0.0)
        for h in range(1, N_MEM_HEADS):
            out = out + jnp.where(lane_head == h, o[h * 8:(h + 1) * 8], 0.0)
        oc_s[p * 8:(p + 1) * 8, :] = out

    def out_proj():
        x2_ref[...] = x1_ref[...] + _dot(oc_s[...].astype(BF16), wo_ref[...])

    return [q_proj, attend, out_proj]


def _sample_ffn_kernel(x2_ref, prev_ref, gf_ref, wg_ref, wu_ref, wfc_ref, bfc_ref, wd_ref, gfin_ref,
                       y_ref, gate_ref, slab_s, p1_s, p2_s):
    R = x2_ref.shape[0]
    nb = R // 4
    nl = FF_CHUNK // 128
    x2 = x2_ref[...]
    n3 = _rmsnorm(x2, gf_ref[...]).astype(BF16)
    tmod = lax.broadcasted_iota(jnp.int32, (R, FF_CHUNK), 0) & 3
    p1_s[...] = jnp.zeros(p1_s.shape, F32)
    p2_s[...] = jnp.zeros(p2_s.shape, F32)

    def gprev(c, g):
        for l in range(nl):
            cols = slice(c * FF_CHUNK + l * 128, c * FF_CHUNK + (l + 1) * 128)
            s0 = prev_ref[0, :, cols]
            s1 = prev_ref[1, :, cols]
            p1_s[l, pl.ds(0, nb, stride=4), :] = s1
            p2_s[l, pl.ds(0, nb, stride=4), :] = s0
            p2_s[l, pl.ds(1, nb, stride=4), :] = s1
        p1 = jnp.concatenate([p1_s[l] for l in range(nl)], axis=1)
        p2 = jnp.concatenate([p2_s[l] for l in range(nl)], axis=1)
        g1 = jnp.where(tmod >= 1, pltpu.roll(g, 1, 0), p1)
        g2 = jnp.where(tmod >= 2, pltpu.roll(g, 2, 0), p2)
        return g2, g1

    def on_gate(c, g):
        for l in range(nl):
            slab_s[l] = g[:, l * 128:(l + 1) * 128]
        for tt in range(2):
            gate_ref[tt, :, c * FF_CHUNK:(c + 1) * FF_CHUNK] = jnp.concatenate(
                [slab_s[l, pl.ds(2 + tt, nb, stride=4), :] for l in range(nl)], axis=1)

    acc = _ffn_chunks(n3, x2, gprev, wg_ref, wu_ref, wfc_ref, bfc_ref, wd_ref, on_gate)
    y_ref[...] = _rmsnorm(acc, gfin_ref[...])


def _const_spec(shape, grid_rank):
    zeros = (0,) * len(shape)
    if grid_rank == 1:
        return pl.BlockSpec(shape, lambda i: zeros, pipeline_mode=pl.Buffered(1))
    return pl.BlockSpec(shape, lambda i, j: zeros, pipeline_mode=pl.Buffered(1))


def _block_diag4(w):
    eye = jnp.eye(4, dtype=w.dtype)
    return (w[:, :, None, :] * eye[:, None, :, None]).reshape(4 * LRU_BLOCK, 4 * LRU_BLOCK)


def _params(sem):
    return pltpu.CompilerParams(dimension_semantics=sem, vmem_limit_bytes=VMEM_LIMIT)


def kernel(x_prompt, x_sample, cache_swa_k, cache_swa_v, cache_mem_k, cache_mem_v, state_lru_conv, state_lru_h, state_ffn_conv, mem_prompt, g_mix, w_in, w_lru_conv, b_lru_conv, w_lru_a, b_lru_a, w_lru_x, b_lru_x, lru_lambda, attn_sinks, w_out, g_cross, g_mem, w_mem_q, w_mem_k, w_mem_v, w_mem_o, g_ffn, w_ffn_gate, w_ffn_up, w_ffn_conv, b_ffn_conv, w_ffn_down, g_final):
    B, T, _ = x_prompt.shape
    NB = x_sample.shape[0]
    NT = T // TM
    SR = NB * 4

    win = w_in[0].astype(BF16)
    wout = w_out[0].astype(BF16)
    wq = w_mem_q[0].astype(BF16)
    wk = w_mem_k[0].astype(BF16)
    wv = w_mem_v[0].astype(BF16)
    wo = w_mem_o[0].astype(BF16)
    wg = w_ffn_gate[0].astype(BF16)
    wu = w_ffn_up[0].astype(BF16)
    wd = w_ffn_down[0].astype(BF16)
    wax = jnp.stack([
        jnp.concatenate([_block_diag4(w_lru_a[0, 4 * gi:4 * gi + 4]), _block_diag4(w_lru_x[0, 4 * gi:4 * gi + 4])],
                        axis=1) for gi in range(2)]).astype(BF16)
    gmix, gcross, gmem, gffn = g_mix, g_cross, g_mem, g_ffn
    gfin = g_final.reshape(1, D_MODEL)
    wconv, bconv = w_lru_conv[0], b_lru_conv
    ba, bx, lam = b_lru_a, b_lru_x, lru_lambda
    wfc, bfc = w_ffn_conv[0], b_ffn_conv
    sinks = attn_sinks[0]
    smem = pl.BlockSpec(memory_space=pltpu.SMEM)

    mk, mv, mkb, mvb = pl.pallas_call(
        _mem_kv_kernel,
        grid=(B,),
        in_specs=[pl.BlockSpec((1, MEM_LEN, D_MODEL), lambda b: (b, 0, 0)),
                  _const_spec((1, D_MODEL), 1), _const_spec((D_MODEL, D_MEM), 1), _const_spec((D_MODEL, D_MEM), 1)],
        out_specs=[pl.BlockSpec((1, MEM_LEN, D_MEM), lambda b: (b, 0, 0))] * 4,
        out_shape=[jax.ShapeDtypeStruct((B, MEM_LEN, D_MEM), F32)] * 2
        + [jax.ShapeDtypeStruct((B, MEM_LEN, D_MEM), BF16)] * 2,
        compiler_params=_params(("arbitrary",)),
        name="mem_kv",
    )(mem_prompt, gmem, wk, wv)

    x1p, pk, pv, pconv8, ph8 = pl.pallas_call(
        _prompt_mixer_kernel,
        grid=(B, NT),
        in_specs=[smem,
                  pl.BlockSpec((1, TM, D_MODEL), lambda b, t: (b, t, 0)),
                  _const_spec((1, D_MODEL), 2), _const_spec((D_MODEL, D_IN), 2),
                  _const_spec((LRU_CONV_W, D_LRU), 2), _const_spec((1, D_LRU), 2),
                  _const_spec((2, 256, 512), 2), _const_spec((1, D_LRU), 2), _const_spec((1, D_LRU), 2),
                  _const_spec((1, D_LRU), 2), _const_spec((D_MODEL, D_MODEL), 2)],
        out_specs=[pl.BlockSpec((1, TM, D_MODEL), lambda b, t: (b, t, 0)),
                   pl.BlockSpec((1, WINDOW, D_KV), lambda b, t: (b, 0, 0)),
                   pl.BlockSpec((1, WINDOW, D_KV), lambda b, t: (b, 0, 0)),
                   pl.BlockSpec((1, 8, D_LRU), lambda b, t: (b, 0, 0)),
                   pl.BlockSpec((1, 8, D_LRU), lambda b, t: (b, 0, 0))],
        out_shape=[jax.ShapeDtypeStruct((B, T, D_MODEL), F32),
                   jax.ShapeDtypeStruct((B, WINDOW, D_KV), F32),
                   jax.ShapeDtypeStruct((B, WINDOW, D_KV), F32),
                   jax.ShapeDtypeStruct((B, 8, D_LRU), F32),
                   jax.ShapeDtypeStruct((B, 8, D_LRU), F32)],
        scratch_shapes=[pltpu.VMEM((TM, D_IN), F32),
                        pltpu.VMEM((D_LRU // 128, HR, 128), F32),
                        pltpu.VMEM((D_LRU // 128, HR, 128), F32),
                        pltpu.VMEM((24, D_LRU), F32),
                        pltpu.VMEM((TM, D_LRU), F32),
                        pltpu.VMEM((TM, D_LRU), F32),
                        pltpu.VMEM((HR, D_LRU), F32),
                        pltpu.VMEM((TM, D_LRU), F32),
                        pltpu.VMEM((8, D_LRU), F32),
                        pltpu.VMEM((8, TM + WINDOW, D_KV), BF16),
                        pltpu.VMEM((TM, D_MODEL), BF16)],
        compiler_params=_params(("arbitrary", "arbitrary")),
        name="prompt_mixer",
    )(sinks, x_prompt, gmix, win, wconv, bconv, wax, ba, bx, lam, wout)

    xs = x_sample.reshape(SR, D_MODEL)
    conv_prev4 = jnp.pad(state_lru_conv[0], ((0, 0), (0, 1), (0, 0))).reshape(SR, D_LRU)
    h0rep = jnp.repeat(state_lru_h[0], 4, axis=0)
    ck = jnp.transpose(cache_swa_k[0], (0, 2, 3, 1)).reshape(NB, D_KV, WINDOW)
    cv = jnp.transpose(cache_swa_v[0], (0, 2, 3, 1)).reshape(NB, D_KV, WINDOW)
    wkt = jnp.transpose(w_in[0][:, 1536:1664]).astype(BF16)
    wvt = jnp.transpose(w_in[0][:, 1664:1792]).astype(BF16)
    row_spec = lambda w: pl.BlockSpec((S_ROWS, w), lambda i: (i, 0))
    cache_spec = pl.BlockSpec((S_NB, D_KV, WINDOW), lambda i: (i, 0, 0))
    x1s, sk, sv, xr_s, h_s = pl.pallas_call(
        _sample_mixer_kernel,
        grid=(NB // S_NB,),
        in_specs=[smem, row_spec(D_MODEL), row_spec(D_LRU), row_spec(D_LRU), cache_spec, cache_spec,
                  _const_spec((1, D_MODEL), 1), _const_spec((D_MODEL, D_IN), 1),
                  _const_spec((LRU_CONV_W, D_LRU), 1), _const_spec((1, D_LRU), 1),
                  _const_spec((2, 256, 512), 1), _const_spec((1, D_LRU), 1), _const_spec((1, D_LRU), 1),
                  _const_spec((1, D_LRU), 1), _const_spec((D_MODEL, D_MODEL), 1),
                  _const_spec((D_KV, D_MODEL), 1), _const_spec((D_KV, D_MODEL), 1)],
        out_specs=[row_spec(D_MODEL), cache_spec, cache_spec, row_spec(D_LRU), row_spec(D_LRU)],
        out_shape=[jax.ShapeDtypeStruct((SR, D_MODEL), F32),
                   jax.ShapeDtypeStruct((NB, D_KV, WINDOW), F32),
                   jax.ShapeDtypeStruct((NB, D_KV, WINDOW), F32),
                   jax.ShapeDtypeStruct((SR, D_LRU), F32),
                   jax.ShapeDtypeStruct((SR, D_LRU), F32)],
        scratch_shapes=[pltpu.VMEM((S_ROWS, D_ATTN), F32),
                        pltpu.VMEM((S_ROWS, D_ATTN), F32)],
        compiler_params=_params(("arbitrary",)),
        name="sample_mixer",
    )(sinks, xs, conv_prev4, h0rep, ck, cv, gmix, win, wconv, bconv, wax, ba, bx, lam, wout, wkt, wvt)

    cmk = cache_mem_k.reshape(NB, MEM_LEN * N_MEM_HEADS, MEM_HEAD_DIM)
    cmv = cache_mem_v.reshape(NB, MEM_LEN * N_MEM_HEADS, MEM_HEAD_DIM)
    ntf = T // TF
    assert NB == C_NB * B * ntf
    crow = pl.BlockSpec((C_ROWS, D_MODEL), lambda b, t: (b * ntf + t, 0))
    cmem = pl.BlockSpec((C_NB, MEM_LEN * N_MEM_HEADS, MEM_HEAD_DIM), lambda b, t: (b * ntf + t, 0, 0))
    y_prompt, pffn8, x2s = pl.pallas_call(
        _prompt_ffn_kernel,
        grid=(B, ntf),
        in_specs=[pl.BlockSpec((1, TF, D_MODEL), lambda b, t: (b, t, 0)),
                  pl.BlockSpec((1, MEM_LEN, D_MEM), lambda b, t: (b, 0, 0)),
                  pl.BlockSpec((1, MEM_LEN, D_MEM), lambda b, t: (b, 0, 0)),
                  crow, cmem, cmem,
                  _const_spec((1, D_MODEL), 2), _const_spec((D_MODEL, D_MEM), 2), _const_spec((D_MEM, D_MODEL), 2),
                  _const_spec((1, D_MODEL), 2), _const_spec((D_MODEL, D_FF), 2), _const_spec((D_MODEL, D_FF), 2),
                  _const_spec((FFN_CONV_W, D_FF), 2), _const_spec((1, D_FF), 2), _const_spec((D_FF, D_MODEL), 2),
                  _const_spec((1, D_MODEL), 2)],
        out_specs=[pl.BlockSpec((1, TF, D_MODEL), lambda b, t: (b, t, 0)),
                   pl.BlockSpec((1, 8, D_FF), lambda b, t: (b, 0, 0)),
                   crow],
        out_shape=[jax.ShapeDtypeStruct((B, T, D_MODEL), F32),
                   jax.ShapeDtypeStruct((B, 8, D_FF), F32),
                   jax.ShapeDtypeStruct((SR, D_MODEL), F32)],
        scratch_shapes=[pltpu.VMEM((TF, D_MEM), BF16),
                        pltpu.VMEM((TF + 8, FF_CHUNK), F32),
                        pltpu.VMEM((8, D_FF), F32),
                        pltpu.VMEM((C_ROWS, D_MEM), F32),
                        pltpu.VMEM((C_ROWS, D_MEM), F32)],
        compiler_params=_params(("arbitrary", "arbitrary")),
        name="prompt_ffn",
    )(x1p, mkb, mvb, x1s, cmk, cmv, gcross, wq, wo, gffn, wg, wu, wfc, bfc, wd, gfin)

    ffn_prev_tm = jnp.transpose(state_ffn_conv[0], (1, 0, 2))
    slab = pltpu.VMEM((FF_CHUNK // 128, SR, 128), F32)
    y_s, gate_tm = pl.pallas_call(
        _sample_ffn_kernel,
        grid=(1,),
        in_specs=[_const_spec((SR, D_MODEL), 1), _const_spec((FFN_CONV_W - 1, NB, D_FF), 1),
                  _const_spec((1, D_MODEL), 1), _const_spec((D_MODEL, D_FF), 1), _const_spec((D_MODEL, D_FF), 1),
                  _const_spec((FFN_CONV_W, D_FF), 1), _const_spec((1, D_FF), 1), _const_spec((D_FF, D_MODEL), 1),
                  _const_spec((1, D_MODEL), 1)],
        out_specs=[pl.BlockSpec((SR, D_MODEL), lambda i: (0, 0)),
                   pl.BlockSpec((FFN_CONV_W - 1, NB, D_FF), lambda i: (0, 0, 0))],
        out_shape=[jax.ShapeDtypeStruct((SR, D_MODEL), F32),
                   jax.ShapeDtypeStruct((FFN_CONV_W - 1, NB, D_FF), F32)],
        scratch_shapes=[slab, slab, slab],
        compiler_params=_params(("arbitrary",)),
        name="sample_ffn",
    )(x2s, ffn_prev_tm, gffn, wg, wu, wfc, bfc, wd, gfin)

    p_swa_k = pk.reshape(1, B, WINDOW, N_KV_HEADS, HEAD_DIM)
    p_swa_v = pv.reshape(1, B, WINDOW, N_KV_HEADS, HEAD_DIM)
    p_mem_k = mk.reshape(1, B, MEM_LEN, N_MEM_HEADS, MEM_HEAD_DIM)
    p_mem_v = mv.reshape(1, B, MEM_LEN, N_MEM_HEADS, MEM_HEAD_DIM)
    p_lru_conv = pconv8[None, :, 8 - (LRU_CONV_W - 1):, :]
    p_lru_h = ph8[None, :, 0, :]
    p_ffn_conv = pffn8[None, :, 8 - (FFN_CONV_W - 1):, :]
    y_sample = y_s.reshape(NB, 4, D_MODEL)
    s_swa_k = jnp.transpose(sk.reshape(NB, N_KV_HEADS, HEAD_DIM, WINDOW), (0, 3, 1, 2))[None]
    s_swa_v = jnp.transpose(sv.reshape(NB, N_KV_HEADS, HEAD_DIM, WINDOW), (0, 3, 1, 2))[None]
    s_lru_conv = xr_s.reshape(NB, 4, D_LRU)[None, :, 1:, :]
    s_lru_h = h_s.reshape(NB, 4, D_LRU)[None, :, 3, :]
    s_ffn_conv = jnp.transpose(gate_tm, (1, 0, 2))[None]
    return (y_prompt, y_sample, p_swa_k, p_swa_v, p_mem_k, p_mem_v, p_lru_conv, p_lru_h, p_ffn_conv,
            s_swa_k, s_swa_v, s_lru_conv, s_lru_h, s_ffn_conv)
```

```python
import functools

import jax
import jax.numpy as jnp
from jax import lax
from jax.experimental import pallas as pl
from jax.experimental.pallas import tpu as pltpu

D_MODEL = 1024
D_LRU = 512
LRU_BLOCK = 64
LRU_CONV_W = 4
LRU_C = 8.0
N_Q_HEADS = 8
N_KV_HEADS = 2
HEAD_DIM = 64
D_ATTN = 512
D_KV = 128
WINDOW = 128
D_IN = 1792
MEM_LEN = 256
N_MEM_HEADS = 4
MEM_HEAD_DIM = 128
D_MEM = 512
D_FF = 3072
FFN_CONV_W = 3
EPS = 1e-6
NEG_INF = -1e30

F32 = jnp.float32
BF16 = jnp.bfloat16

SLOPES = [float(2.0 ** (-8.0 * (i + 1) / N_Q_HEADS)) for i in range(N_Q_HEADS)]
ATTN_SCALE = HEAD_DIM ** -0.5
MEM_SCALE = MEM_HEAD_DIM ** -0.5
LOG2E = 1.4426950408889634
F32_TINY = 1.1754944e-38

TM = 1024
HR = 512
SEG = HR // 8
TF = 512
FF_CHUNK = 1024
S_NB = 16
S_ROWS = 4 * S_NB
C_NB = 4
C_ROWS = 4 * C_NB
VMEM_LIMIT = 56 * 1024 * 1024


def _dot(a, b):
    return jnp.dot(a, b, preferred_element_type=F32)


def _dot_nt(a, b):
    return lax.dot_general(a, b, (((1,), (1,)), ((), ())), preferred_element_type=F32)


def _rmsnorm(x, g):
    ms = jnp.mean(x * x, axis=-1, keepdims=True)
    return x * lax.rsqrt(ms + EPS) * g


def _gelu(x):
    c = 0.7978845608028654
    return x * (0.5 * (1.0 + jnp.tanh(c * (x + 0.044715 * (x * x * x)))))


def _sigmoid(x):
    return 1.0 / (1.0 + jnp.exp(-x))


def _softplus(x):
    return jnp.maximum(x, 0.0) + jnp.log1p(jnp.exp(-jnp.abs(x)))


def _lru_gates(xc, wax_ref, ba, bx, lam):
    xcb = xc.astype(BF16)
    pa, px = [], []
    for gi in range(2):
        pre = _dot(xcb[:, gi * 256:(gi + 1) * 256], wax_ref[gi])
        pa.append(pre[:, :256])
        px.append(pre[:, 256:])
    r = _sigmoid(jnp.concatenate(pa, axis=1) + ba)
    i = _sigmoid(jnp.concatenate(px, axis=1) + bx)
    log_a = (-LRU_C * _softplus(-lam)) * r
    a = jnp.exp(log_a)
    om = -jnp.tanh(log_a) * (a * a + 1.0)
    b = (om * lax.rsqrt(jnp.maximum(om, F32_TINY))) * (i * xc)
    return a, b


def _head_variants(t):
    lo = lax.broadcasted_iota(jnp.int32, t.shape, 1) < HEAD_DIM
    tr = pltpu.roll(t, HEAD_DIM, 1)
    z = jnp.zeros_like(t)
    x0 = jnp.where(lo, t, z)
    y0 = jnp.where(lo, z, tr)
    x1 = jnp.where(lo, tr, z)
    y1 = jnp.where(lo, z, t)
    return [v.astype(BF16) for v in (x0, y0, x1, y1)]


def _mem_kv_kernel(mem_ref, g_ref, wk_ref, wv_ref, mk_ref, mv_ref, mkb_ref, mvb_ref):
    n = _rmsnorm(mem_ref[0], g_ref[...]).astype(BF16)
    mk = _dot(n, wk_ref[...])
    mv = _dot(n, wv_ref[...])
    mk_ref[0] = mk
    mv_ref[0] = mv
    mkb_ref[0] = mk.astype(BF16)
    mvb_ref[0] = mv.astype(BF16)


def _prompt_mixer_kernel(sinks_ref, x_ref, g_ref, win_ref, wconv_ref, bconv_ref, wax_ref, ba_ref, bx_ref,
                         lam_ref, wout_ref,
                         x1_ref, pk_ref, pv_ref, pconv_ref, ph_ref,
                         proj_s, pin_s, pout_s, xtail_s, a_s, b_s, hl_s, h_s, hc_s, kv_s, ymix_s):
    t = pl.program_id(1)

    @pl.when(t == 0)
    def _():
        xtail_s[...] = jnp.zeros((24, D_LRU), F32)
        hc_s[...] = jnp.zeros((8, D_LRU), F32)
        kv_s[:, 0:WINDOW, :] = jnp.zeros((8, WINDOW, D_KV), BF16)

    def sec(i):
        return slice(i * HR, (i + 1) * HR)

    def in_proj(i):
        n1 = _rmsnorm(x_ref[0, sec(i), :], g_ref[...]).astype(BF16)
        proj_s[sec(i), :] = _dot(n1, win_ref[...])

    def seg_rows(j):
        return pl.ds(64 * (j % (SEG // 8)) + j // (SEG // 8), 8, stride=8)

    row8 = lax.broadcasted_iota(jnp.int32, (8, D_LRU), 0)

    def lru_gates(i):
        r0 = i * HR
        for j in range(HR // 8):
            for l in range(D_LRU // 128):
                pin_s[l, seg_rows(j), :] = proj_s[r0 + 8 * j:r0 + 8 * j + 8, l * 128:(l + 1) * 128]
        xs = jnp.concatenate([pin_s[l] for l in range(D_LRU // 128)], axis=1)
        tail = xs[HR - 24:, :]
        prev = xtail_s[...]
        xtail_s[...] = tail
        heads = [jnp.where(row8 == 0, pltpu.roll(prev[8 * q:8 * q + 8, :], 1, 0),
                           pltpu.roll(tail[8 * q:8 * q + 8, :], 1, 0)) for q in range(3)]
        sh1 = jnp.concatenate(heads[2:] + [xs[:HR - 8, :]], axis=0)
        sh2 = jnp.concatenate(heads[1:] + [xs[:HR - 16, :]], axis=0)
        sh3 = jnp.concatenate(heads + [xs[:HR - 24, :]], axis=0)
        wc = wconv_ref[...]
        xc = bconv_ref[...] + wc[0:1] * sh3 + wc[1:2] * sh2 + wc[2:3] * sh1 + wc[3:4] * xs
        a, b = _lru_gates(xc, wax_ref, ba_ref[...], bx_ref[...], lam_ref[...])
        a_s[sec(i), :] = a
        b_s[sec(i), :] = b

    def lru_scan(i, hcar):
        r0 = i * HR
        hl = b_s[r0:r0 + 8, :]
        ac = a_s[r0:r0 + 8, :]
        hl_s[0:8, :] = hl
        for g in range(1, SEG):
            av = a_s[r0 + 8 * g:r0 + 8 * g + 8, :]
            hl = av * hl + b_s[r0 + 8 * g:r0 + 8 * g + 8, :]
            ac = av * ac
            hl_s[8 * g:8 * g + 8, :] = hl
            a_s[r0 + 8 * g:r0 + 8 * g + 8, :] = ac
        hin = hcar
        for s in range(8):
            hend = hl + ac * hin
            if s < 7:
                hin = jnp.where(row8 == s + 1, pltpu.roll(hend, 1, 0), hin)
        hcar = jnp.broadcast_to(hend[7:8, :], (8, D_LRU))
        for g in range(SEG):
            hg = hl_s[8 * g:8 * g + 8, :] + a_s[r0 + 8 * g:r0 + 8 * g + 8, :] * hin
            for l in range(D_LRU // 128):
                pout_s[l, 8 * g:8 * g + 8, :] = hg[:, l * 128:(l + 1) * 128]
        for j in range(HR // 8):
            h_s[r0 + 8 * j:r0 + 8 * j + 8, :] = jnp.concatenate(
                [pout_s[l, seg_rows(j), :] for l in range(D_LRU // 128)], axis=1)
        gate = proj_s[sec(i), D_LRU:2 * D_LRU]
        ymix_s[sec(i), 0:D_LRU] = (h_s[sec(i), :] * _gelu(gate)).astype(BF16)
        return hcar

    def kv_prep(i):
        k = proj_s[sec(i), 1536:1664]
        v = proj_s[sec(i), 1664:1792]
        for n, arr in enumerate(_head_variants(k) + _head_variants(v)):
            kv_s[n, WINDOW + i * HR:WINDOW + (i + 1) * HR, :] = arr

    qi = lax.broadcasted_iota(jnp.int32, (WINDOW, 2 * WINDOW), 0)
    kj = lax.broadcasted_iota(jnp.int32, (WINDOW, 2 * WINDOW), 1)
    dist = qi + WINDOW - kj
    valid = (dist >= 0) & (dist < WINDOW)
    valid0 = valid & (kj >= jnp.where(t == 0, WINDOW, 0))
    distf = dist.astype(F32)
    bias = [jnp.where(valid, -SLOPES[h] * distf, NEG_INF) for h in range(N_Q_HEADS)]
    bias0 = [jnp.where(valid0, -SLOPES[h] * distf, NEG_INF) for h in range(N_Q_HEADS)]
    lane_lo = lax.broadcasted_iota(jnp.int32, (WINDOW, 2 * HEAD_DIM), 1) < HEAD_DIM

    r512 = lax.broadcasted_iota(jnp.int32, (4 * WINDOW, 2 * HEAD_DIM), 0)
    l512 = lax.broadcasted_iota(jnp.int32, (4 * WINDOW, 2 * HEAD_DIM), 1)
    ones_cols = jnp.where((r512 < 2 * WINDOW) == (l512 < HEAD_DIM), 1.0, 0.0).astype(BF16)
    n_iter = (HR // WINDOW) * 4

    def qk(n):
        j, c = divmod(n, 4)
        hk = c // 2
        rows = slice(j * WINDOW, (j + 1) * WINDOW)
        win = slice(j * WINDOW, (j + 2) * WINDOW)
        qc = (proj_s[rows, 1024 + c * 128:1024 + (c + 1) * 128] * ATTN_SCALE).astype(BF16)
        kcat = jnp.concatenate([kv_s[2 * hk, win, :], kv_s[2 * hk + 1, win, :]], axis=0)
        return _dot_nt(qc, kcat)

    def softmax_pv(n, s):
        j, c = divmod(n, 4)
        hk = c // 2
        rows = slice(j * WINDOW, (j + 1) * WINDOW)
        win = slice(j * WINDOW, (j + 2) * WINDOW)
        vcat = jnp.concatenate([kv_s[4 + 2 * hk, win, :], kv_s[5 + 2 * hk, win, :]], axis=0)
        vaug = jnp.concatenate([vcat, ones_cols], axis=1)
        ps, es = [], []
        for half in range(2):
            h = 2 * c + half
            sink = sinks_ref[h]
            sh = s[:, half * 256:(half + 1) * 256] + (bias0[h] if j == 0 else bias[h])
            m = jnp.maximum(jnp.max(sh, axis=-1, keepdims=True), sink)
            ps.append(jnp.exp(sh - m).astype(BF16))
            es.append(jnp.exp(sink - m))
        oa = _dot(jnp.concatenate(ps, axis=1), vaug)
        den = oa[:, 128:256] + jnp.where(lane_lo, es[0], es[1])
        ymix_s[rows, D_LRU + c * 128:D_LRU + (c + 1) * 128] = (oa[:, 0:128] * (1.0 / den)).astype(BF16)

    def attention(i):
        depth = 2
        base = i * n_iter
        pend = [qk(base + n) for n in range(depth)]
        for n in range(n_iter):
            s = pend.pop(0)
            if n + depth < n_iter:
                pend.append(qk(base + n + depth))
            softmax_pv(base + n, s)

    def out_proj(i):
        x1_ref[0, sec(i), :] = x_ref[0, sec(i), :] + _dot(ymix_s[sec(i), :], wout_ref[...])

    n_sec = TM // HR
    h = hc_s[...]
    in_proj(0)
    lru_gates(0)
    for i in range(n_sec):
        if i + 1 < n_sec:
            in_proj(i + 1)
        h = lru_scan(i, h)
        kv_prep(i)
        attention(i)
        if i + 1 < n_sec:
            lru_gates(i + 1)
        out_proj(i)

    hc_s[...] = h
    ph_ref[0] = h
    pconv_ref[0] = proj_s[TM - 8:, 0:D_LRU]
    pk_ref[0] = proj_s[TM - WINDOW:, 1536:1664]
    pv_ref[0] = proj_s[TM - WINDOW:, 1664:1792]
    kv_s[:, 0:WINDOW, :] = kv_s[:, TM:TM + WINDOW, :]


def _ffn_chunks(n3, acc, gprev_fn, wg_ref, wu_ref, wfc_ref, bfc_ref, wd_ref, on_gate):
    nc = D_FF // FF_CHUNK

    def up(c):
        cs = slice(c * FF_CHUNK, (c + 1) * FF_CHUNK)
        return _dot(n3, wg_ref[:, cs]), _dot(n3, wu_ref[:, cs])

    nxt = up(0)
    for c in range(nc):
        cs = slice(c * FF_CHUNK, (c + 1) * FF_CHUNK)
        g, u = nxt
        if c + 1 < nc:
            nxt = up(c + 1)
        g2, g1 = gprev_fn(c, g)
        on_gate(c, g)
        wfc = wfc_ref[:, cs]
        conv = bfc_ref[:, cs] + wfc[0:1] * g2 + wfc[1:2] * g1 + wfc[2:3] * g
        hmid = (_gelu(conv) * u).astype(BF16)
        acc = acc + _dot(hmid, wd_ref[cs, :])
    return acc


def _prompt_ffn_kernel(x1_ref, mk_ref, mv_ref, xs1_ref, cmk_ref, cmv_ref, gc_ref, wq_ref, wo_ref, gf_ref,
                       wg_ref, wu_ref, wfc_ref, bfc_ref, wd_ref, gfin_ref,
                       y_ref, pffn_ref, xs2_ref,
                       oc_s, gbuf_s, gcar_s, sq_s, soc_s):
    t = pl.program_id(1)

    @pl.when(t == 0)
    def _():
        gcar_s[...] = jnp.zeros((8, D_FF), F32)

    _sample_cross(xs1_ref, cmk_ref, cmv_ref, gc_ref, wq_ref, wo_ref, xs2_ref, sq_s, soc_s)

    x1 = x1_ref[0]
    qc = _dot(_rmsnorm(x1, gc_ref[...]).astype(BF16), wq_ref[...]).astype(BF16)
    hsl = [slice(h * MEM_HEAD_DIM, (h + 1) * MEM_HEAD_DIM) for h in range(N_MEM_HEADS)]
    ss = [_dot_nt(qc[:, hs], mk_ref[0, :, hs]) for hs in hsl]
    for h, hs in enumerate(hsl):
        s = ss[h]
        m = jnp.max(s, axis=-1, keepdims=True)
        p = jnp.exp2((s - m) * (MEM_SCALE * LOG2E))
        l = jnp.sum(p, axis=-1, keepdims=True)
        o = _dot(p.astype(BF16), mv_ref[0, :, hs]) * (1.0 / l)
        oc_s[:, hs] = o.astype(BF16)
    x2 = x1 + _dot(oc_s[...], wo_ref[...])
    n3 = _rmsnorm(x2, gf_ref[...]).astype(BF16)

    def gprev(c, g):
        cs = slice(c * FF_CHUNK, (c + 1) * FF_CHUNK)
        gbuf_s[0:8, :] = gcar_s[:, cs]
        gbuf_s[8:TF + 8, :] = g
        return gbuf_s[6:6 + TF, :], gbuf_s[7:7 + TF, :]

    def on_gate(c, g):
        cs = slice(c * FF_CHUNK, (c + 1) * FF_CHUNK)
        tail = g[TF - 8:, :]
        gcar_s[:, cs] = tail
        pffn_ref[0, :, cs] = tail

    acc = _ffn_chunks(n3, x2, gprev, wg_ref, wu_ref, wfc_ref, bfc_ref, wd_ref, on_gate)
    y_ref[0] = _rmsnorm(acc, gfin_ref[...])


def _sample_mixer_kernel(sinks_ref, x_ref, prev4_ref, h0_ref, ck_ref, cv_ref, g_ref, win_ref, wconv_ref, bconv_ref,
                         wax_ref, ba_ref, bx_ref, lam_ref, wout_ref, wkt_ref, wvt_ref,
                         x1_ref, sk_ref, sv_ref, xr_ref, h_ref,
                         q_s, yatt_s):
    R = S_ROWS
    x = x_ref[...]
    n1 = _rmsnorm(x, g_ref[...]).astype(BF16)
    proj = _dot(n1, win_ref[...])
    xr = proj[:, 0:D_LRU]
    gate = proj[:, D_LRU:2 * D_LRU]
    q_s[...] = proj[:, 1024:1536] * ATTN_SCALE
    xr_ref[...] = xr

    tmod = lax.broadcasted_iota(jnp.int32, (R, D_LRU), 0) & 3
    prev4 = prev4_ref[...]
    xs1 = jnp.where(tmod >= 1, pltpu.roll(xr, 1, 0), pltpu.roll(prev4, R - 2, 0))
    xs2 = jnp.where(tmod >= 2, pltpu.roll(xr, 2, 0), pltpu.roll(prev4, R - 1, 0))
    xs3 = jnp.where(tmod >= 3, pltpu.roll(xr, 3, 0), prev4)
    wc = wconv_ref[...]
    xc = bconv_ref[...] + wc[0:1] * xs3 + wc[1:2] * xs2 + wc[2:3] * xs1 + wc[3:4] * xr

    a, b = _lru_gates(xc, wax_ref, ba_ref[...], bx_ref[...], lam_ref[...])
    for s in (1, 2):
        m = tmod >= s
        a_sh = pltpu.roll(a, s, 0)
        b_sh = pltpu.roll(b, s, 0)
        b = jnp.where(m, a * b_sh + b, b)
        a = jnp.where(m, a * a_sh, a)
    h = a * h0_ref[...] + b
    h_ref[...] = h
    y_lru = h * _gelu(gate)

    kt_new = _dot_nt(wkt_ref[...], n1)
    vt_new = _dot_nt(wvt_ref[...], n1)
    zpad = jnp.zeros((D_KV, WINDOW - R), F32)
    kt_pad = jnp.concatenate([kt_new, zpad], axis=1)
    vt_pad = jnp.concatenate([vt_new, zpad], axis=1)
    kt_pad_b = kt_pad.astype(BF16)
    vt_pad_b = vt_pad.astype(BF16)

    r64 = lax.broadcasted_iota(jnp.int32, (8 * N_Q_HEADS, WINDOW), 0)
    c64 = lax.broadcasted_iota(jnp.int32, (8 * N_Q_HEADS, WINDOW), 1)
    t64 = r64 & 3
    slope = jnp.zeros((8 * N_Q_HEADS, WINDOW), F32)
    sinkcol = jnp.zeros((8 * N_Q_HEADS, 1), F32)
    for hq in range(N_Q_HEADS):
        slope = jnp.where((r64 >> 3) == hq, SLOPES[hq], slope)
        sinkcol = jnp.where((r64[:, 0:1] >> 3) == hq, sinks_ref[hq], sinkcol)
    bias_c = jnp.where(c64 > t64, -slope * (t64 + WINDOW - c64).astype(F32), NEG_INF)
    bat0 = (r64 & 7) < 4
    lane_lo = lax.broadcasted_iota(jnp.int32, (8, D_KV), 1) < HEAD_DIM
    lane128 = lax.broadcasted_iota(jnp.int32, (D_KV, WINDOW), 1)

    def pair_scores(p):
        r0 = p * 8
        q8 = q_s[r0:r0 + 8, :]
        blocks = []
        for hq in range(N_Q_HEADS):
            chunk = q8[:, (hq // 2) * 128:(hq // 2 + 1) * 128]
            hk = hq // (N_Q_HEADS // N_KV_HEADS)
            src = chunk if (hq % 2) == hk else pltpu.roll(chunk, HEAD_DIM, 1)
            blocks.append(jnp.where(lane_lo == (hk == 0), src, 0.0))
        lhs = jnp.concatenate(blocks, axis=0).astype(BF16)
        sb = []
        for bb in range(2):
            bidx = 2 * p + bb
            kt = ck_ref[bidx]
            vt = cv_ref[bidx]
            sb.append(_dot(lhs, kt.astype(BF16)))
            shift = (WINDOW - 4 - 4 * bidx) % WINDOW
            sk_ref[bidx] = jnp.where(lane128 >= WINDOW - 4, pltpu.roll(kt_pad, shift, 1),
                                     pltpu.roll(kt, WINDOW - 4, 1))
            sv_ref[bidx] = jnp.where(lane128 >= WINDOW - 4, pltpu.roll(vt_pad, shift, 1),
                                     pltpu.roll(vt, WINDOW - 4, 1))
        s_c = jnp.where(bat0, sb[0], sb[1]) + bias_c
        same = (c64 >> 2) == (2 * p + ((r64 & 7) >> 2))
        dn = t64 - (c64 & 3)
        bias_n = jnp.where(same & (dn >= 0), -slope * dn.astype(F32), NEG_INF)
        s_n = _dot(lhs, kt_pad_b) + bias_n
        return s_c, s_n

    def pair_finish(p, s_c, s_n):
        r0 = p * 8
        m = jnp.maximum(jnp.maximum(jnp.max(s_c, axis=-1, keepdims=True), jnp.max(s_n, axis=-1, keepdims=True)),
                        sinkcol)
        pc = jnp.exp(s_c - m)
        pn = jnp.exp(s_n - m)
        l = jnp.sum(pc, axis=-1, keepdims=True) + jnp.sum(pn, axis=-1, keepdims=True) + jnp.exp(sinkcol - m)
        pcb = pc.astype(BF16)
        ob = [_dot_nt(pcb, cv_ref[2 * p + bb].astype(BF16)) for bb in range(2)]
        o = (jnp.where(bat0, ob[0], ob[1]) + _dot_nt(pn.astype(BF16), vt_pad_b)) * (1.0 / l)
        for c in range(N_Q_HEADS // 2):
            hk = c // 2
            ev = o[16 * c:16 * c + 8, :]
            od = o[16 * c + 8:16 * c + 16, :]
            if hk == 0:
                od = pltpu.roll(od, HEAD_DIM, 1)
            else:
                ev = pltpu.roll(ev, HEAD_DIM, 1)
            yatt_s[r0:r0 + 8, c * 128:(c + 1) * 128] = jnp.where(lane_lo, ev, od)

    n_pairs = S_NB // 2
    cur = pair_scores(0)
    for p in range(n_pairs):
        nxt = pair_scores(p + 1) if p + 1 < n_pairs else None
        pair_finish(p, *cur)
        cur = nxt

    ymix = jnp.concatenate([y_lru, yatt_s[...]], axis=1).astype(BF16)
    x1_ref[...] = x + _dot(ymix, wout_ref[...])


def _load_mem_heads(ref, b):
    return jnp.concatenate([ref[b, pl.ds(h, MEM_LEN, stride=N_MEM_HEADS), :] for h in range(N_MEM_HEADS)],
                           axis=1).astype(BF16)


def _sample_cross(x1_ref, mk_ref, mv_ref, gc_ref, wq_ref, wo_ref, x2_ref, q_s, oc_s):
    lane_head = lax.broadcasted_iota(jnp.int32, (8, D_MEM), 1) // MEM_HEAD_DIM
    bat0 = (lax.broadcasted_iota(jnp.int32, (32, D_MEM), 0) & 7) < 4
    bat0_s = (lax.broadcasted_iota(jnp.int32, (32, MEM_LEN), 0) & 7) < 4

    def q_proj():
        q_s[...] = _dot(_rmsnorm(x1_ref[...], gc_ref[...]).astype(BF16), wq_ref[...])

    def scores(p):
        q8 = q_s[p * 8:(p + 1) * 8, :]
        lhs = jnp.concatenate([jnp.where(lane_head == h, q8, 0.0) for h in range(N_MEM_HEADS)],
                              axis=0).astype(BF16)
        sb = [_dot_nt(lhs, _load_mem_heads(mk_ref, 2 * p + bb)) for bb in range(2)]
        return jnp.where(bat0_s, sb[0], sb[1]) * MEM_SCALE

    def attend():
        ss = [scores(p) for p in range(C_NB // 2)]
        for p, s in enumerate(ss):
            finish(p, s)

    def finish(p, s):
        m = jnp.max(s, axis=-1, keepdims=True)
        pe = jnp.exp(s - m)
        l = jnp.sum(pe, axis=-1, keepdims=True)
        pb = pe.astype(BF16)
        ob = [_dot(pb, _load_mem_heads(mv_ref, 2 * p + bb)) for bb in range(2)]
        o = jnp.where(bat0, ob[0], ob[1]) * (1.0 / l)
        out = jnp.where(lane_head == 0, o[0:8], 0.0)
        for h in range(1, N_MEM_HEADS):
            out = out + jnp.where(lane_head == h, o[h * 8:(h + 1) * 8], 0.0)
        oc_s[p * 8:(p + 1) * 8, :] = out

    def out_proj():
        x2_ref[...] = x1_ref[...] + _dot(oc_s[...].astype(BF16), wo_ref[...])

    q_proj()
    attend()
    out_proj()


def _sample_ffn_kernel(x2_ref, prev_ref, gf_ref, wg_ref, wu_ref, wfc_ref, bfc_ref, wd_ref, gfin_ref,
                       y_ref, gate_ref, slab_s, p1_s, p2_s):
    R = x2_ref.shape[0]
    nb = R // 4
    nl = FF_CHUNK // 128
    x2 = x2_ref[...]
    n3 = _rmsnorm(x2, gf_ref[...]).astype(BF16)
    tmod = lax.broadcasted_iota(jnp.int32, (R, FF_CHUNK), 0) & 3
    p1_s[...] = jnp.zeros(p1_s.shape, F32)
    p2_s[...] = jnp.zeros(p2_s.shape, F32)

    def gprev(c, g):
        for l in range(nl):
            cols = slice(c * FF_CHUNK + l * 128, c * FF_CHUNK + (l + 1) * 128)
            s0 = prev_ref[0, :, cols]
            s1 = prev_ref[1, :, cols]
            p1_s[l, pl.ds(0, nb, stride=4), :] = s1
            p2_s[l, pl.ds(0, nb, stride=4), :] = s0
            p2_s[l, pl.ds(1, nb, stride=4), :] = s1
        p1 = jnp.concatenate([p1_s[l] for l in range(nl)], axis=1)
        p2 = jnp.concatenate([p2_s[l] for l in range(nl)], axis=1)
        g1 = jnp.where(tmod >= 1, pltpu.roll(g, 1, 0), p1)
        g2 = jnp.where(tmod >= 2, pltpu.roll(g, 2, 0), p2)
        return g2, g1

    def on_gate(c, g):
        for l in range(nl):
            slab_s[l] = g[:, l * 128:(l + 1) * 128]
        for tt in range(2):
            gate_ref[tt, :, c * FF_CHUNK:(c + 1) * FF_CHUNK] = jnp.concatenate(
                [slab_s[l, pl.ds(2 + tt, nb, stride=4), :] for l in range(nl)], axis=1)

    acc = _ffn_chunks(n3, x2, gprev, wg_ref, wu_ref, wfc_ref, bfc_ref, wd_ref, on_gate)
    y_ref[...] = _rmsnorm(acc, gfin_ref[...])


def _const_spec(shape, grid_rank):
    zeros = (0,) * len(shape)
    if grid_rank == 1:
        return pl.BlockSpec(shape, lambda i: zeros, pipeline_mode=pl.Buffered(1))
    return pl.BlockSpec(shape, lambda i, j: zeros, pipeline_mode=pl.Buffered(1))


def _block_diag4(w):
    eye = jnp.eye(4, dtype=w.dtype)
    return (w[:, :, None, :] * eye[:, None, :, None]).reshape(4 * LRU_BLOCK, 4 * LRU_BLOCK)


def _params(sem):
    return pltpu.CompilerParams(dimension_semantics=sem, vmem_limit_bytes=VMEM_LIMIT)


def kernel(x_prompt, x_sample, cache_swa_k, cache_swa_v, cache_mem_k, cache_mem_v, state_lru_conv, state_lru_h, state_ffn_conv, mem_prompt, g_mix, w_in, w_lru_conv, b_lru_conv, w_lru_a, b_lru_a, w_lru_x, b_lru_x, lru_lambda, attn_sinks, w_out, g_cross, g_mem, w_mem_q, w_mem_k, w_mem_v, w_mem_o, g_ffn, w_ffn_gate, w_ffn_up, w_ffn_conv, b_ffn_conv, w_ffn_down, g_final):
    B, T, _ = x_prompt.shape
    NB = x_sample.shape[0]
    NT = T // TM
    SR = NB * 4

    win = w_in[0].astype(BF16)
    wout = w_out[0].astype(BF16)
    wq = w_mem_q[0].astype(BF16)
    wk = w_mem_k[0].astype(BF16)
    wv = w_mem_v[0].astype(BF16)
    wo = w_mem_o[0].astype(BF16)
    wg = w_ffn_gate[0].astype(BF16)
    wu = w_ffn_up[0].astype(BF16)
    wd = w_ffn_down[0].astype(BF16)
    wax = jnp.stack([
        jnp.concatenate([_block_diag4(w_lru_a[0, 4 * gi:4 * gi + 4]), _block_diag4(w_lru_x[0, 4 * gi:4 * gi + 4])],
                        axis=1) for gi in range(2)]).astype(BF16)
    gmix, gcross, gmem, gffn = g_mix, g_cross, g_mem, g_ffn
    gfin = g_final.reshape(1, D_MODEL)
    wconv, bconv = w_lru_conv[0], b_lru_conv
    ba, bx, lam = b_lru_a, b_lru_x, lru_lambda
    wfc, bfc = w_ffn_conv[0], b_ffn_conv
    sinks = attn_sinks[0]
    smem = pl.BlockSpec(memory_space=pltpu.SMEM)

    mk, mv, mkb, mvb = pl.pallas_call(
        _mem_kv_kernel,
        grid=(B,),
        in_specs=[pl.BlockSpec((1, MEM_LEN, D_MODEL), lambda b: (b, 0, 0)),
                  _const_spec((1, D_MODEL), 1), _const_spec((D_MODEL, D_MEM), 1), _const_spec((D_MODEL, D_MEM), 1)],
        out_specs=[pl.BlockSpec((1, MEM_LEN, D_MEM), lambda b: (b, 0, 0))] * 4,
        out_shape=[jax.ShapeDtypeStruct((B, MEM_LEN, D_MEM), F32)] * 2
        + [jax.ShapeDtypeStruct((B, MEM_LEN, D_MEM), BF16)] * 2,
        compiler_params=_params(("arbitrary",)),
        name="mem_kv",
    )(mem_prompt, gmem, wk, wv)

    x1p, pk, pv, pconv8, ph8 = pl.pallas_call(
        _prompt_mixer_kernel,
        grid=(B, NT),
        in_specs=[smem,
                  pl.BlockSpec((1, TM, D_MODEL), lambda b, t: (b, t, 0)),
                  _const_spec((1, D_MODEL), 2), _const_spec((D_MODEL, D_IN), 2),
                  _const_spec((LRU_CONV_W, D_LRU), 2), _const_spec((1, D_LRU), 2),
                  _const_spec((2, 256, 512), 2), _const_spec((1, D_LRU), 2), _const_spec((1, D_LRU), 2),
                  _const_spec((1, D_LRU), 2), _const_spec((D_MODEL, D_MODEL), 2)],
        out_specs=[pl.BlockSpec((1, TM, D_MODEL), lambda b, t: (b, t, 0)),
                   pl.BlockSpec((1, WINDOW, D_KV), lambda b, t: (b, 0, 0)),
                   pl.BlockSpec((1, WINDOW, D_KV), lambda b, t: (b, 0, 0)),
                   pl.BlockSpec((1, 8, D_LRU), lambda b, t: (b, 0, 0)),
                   pl.BlockSpec((1, 8, D_LRU), lambda b, t: (b, 0, 0))],
        out_shape=[jax.ShapeDtypeStruct((B, T, D_MODEL), F32),
                   jax.ShapeDtypeStruct((B, WINDOW, D_KV), F32),
                   jax.ShapeDtypeStruct((B, WINDOW, D_KV), F32),
                   jax.ShapeDtypeStruct((B, 8, D_LRU), F32),
                   jax.ShapeDtypeStruct((B, 8, D_LRU), F32)],
        scratch_shapes=[pltpu.VMEM((TM, D_IN), F32),
                        pltpu.VMEM((D_LRU // 128, HR, 128), F32),
                        pltpu.VMEM((D_LRU // 128, HR, 128), F32),
                        pltpu.VMEM((24, D_LRU), F32),
                        pltpu.VMEM((TM, D_LRU), F32),
                        pltpu.VMEM((TM, D_LRU), F32),
                        pltpu.VMEM((HR, D_LRU), F32),
                        pltpu.VMEM((TM, D_LRU), F32),
                        pltpu.VMEM((8, D_LRU), F32),
                        pltpu.VMEM((8, TM + WINDOW, D_KV), BF16),
                        pltpu.VMEM((TM, D_MODEL), BF16)],
        compiler_params=_params(("arbitrary", "arbitrary")),
        name="prompt_mixer",
    )(sinks, x_prompt, gmix, win, wconv, bconv, wax, ba, bx, lam, wout)

    xs = x_sample.reshape(SR, D_MODEL)
    conv_prev4 = jnp.pad(state_lru_conv[0], ((0, 0), (0, 1), (0, 0))).reshape(SR, D_LRU)
    h0rep = jnp.repeat(state_lru_h[0], 4, axis=0)
    ck = jnp.transpose(cache_swa_k[0], (0, 2, 3, 1)).reshape(NB, D_KV, WINDOW)
    cv = jnp.transpose(cache_swa_v[0], (0, 2, 3, 1)).reshape(NB, D_KV, WINDOW)
    wkt = jnp.transpose(w_in[0][:, 1536:1664]).astype(BF16)
    wvt = jnp.transpose(w_in[0][:, 1664:1792]).astype(BF16)
    row_spec = lambda w: pl.BlockSpec((S_ROWS, w), lambda i: (i, 0))
    cache_spec = pl.BlockSpec((S_NB, D_KV, WINDOW), lambda i: (i, 0, 0))
    x1s, sk, sv, xr_s, h_s = pl.pallas_call(
        _sample_mixer_kernel,
        grid=(NB // S_NB,),
        in_specs=[smem, row_spec(D_MODEL), row_spec(D_LRU), row_spec(D_LRU), cache_spec, cache_spec,
                  _const_spec((1, D_MODEL), 1), _const_spec((D_MODEL, D_IN), 1),
                  _const_spec((LRU_CONV_W, D_LRU), 1), _const_spec((1, D_LRU), 1),
                  _const_spec((2, 256, 512), 1), _const_spec((1, D_LRU), 1), _const_spec((1, D_LRU), 1),
                  _const_spec((1, D_LRU), 1), _const_spec((D_MODEL, D_MODEL), 1),
                  _const_spec((D_KV, D_MODEL), 1), _const_spec((D_KV, D_MODEL), 1)],
        out_specs=[row_spec(D_MODEL), cache_spec, cache_spec, row_spec(D_LRU), row_spec(D_LRU)],
        out_shape=[jax.ShapeDtypeStruct((SR, D_MODEL), F32),
                   jax.ShapeDtypeStruct((NB, D_KV, WINDOW), F32),
                   jax.ShapeDtypeStruct((NB, D_KV, WINDOW), F32),
                   jax.ShapeDtypeStruct((SR, D_LRU), F32),
                   jax.ShapeDtypeStruct((SR, D_LRU), F32)],
        scratch_shapes=[pltpu.VMEM((S_ROWS, D_ATTN), F32),
                        pltpu.VMEM((S_ROWS, D_ATTN), F32)],
        compiler_params=_params(("arbitrary",)),
        name="sample_mixer",
    )(sinks, xs, conv_prev4, h0rep, ck, cv, gmix, win, wconv, bconv, wax, ba, bx, lam, wout, wkt, wvt)

    cmk = cache_mem_k.reshape(NB, MEM_LEN * N_MEM_HEADS, MEM_HEAD_DIM)
    cmv = cache_mem_v.reshape(NB, MEM_LEN * N_MEM_HEADS, MEM_HEAD_DIM)
    ntf = T // TF
    assert NB == C_NB * B * ntf
    crow = pl.BlockSpec((C_ROWS, D_MODEL), lambda b, t: (b * ntf + t, 0))
    cmem = pl.BlockSpec((C_NB, MEM_LEN * N_MEM_HEADS, MEM_HEAD_DIM), lambda b, t: (b * ntf + t, 0, 0))
    y_prompt, pffn8, x2s = pl.pallas_call(
        _prompt_ffn_kernel,
        grid=(B, ntf),
        in_specs=[pl.BlockSpec((1, TF, D_MODEL), lambda b, t: (b, t, 0)),
                  pl.BlockSpec((1, MEM_LEN, D_MEM), lambda b, t: (b, 0, 0)),
                  pl.BlockSpec((1, MEM_LEN, D_MEM), lambda b, t: (b, 0, 0)),
                  crow, cmem, cmem,
                  _const_spec((1, D_MODEL), 2), _const_spec((D_MODEL, D_MEM), 2), _const_spec((D_MEM, D_MODEL), 2),
                  _const_spec((1, D_MODEL), 2), _const_spec((D_MODEL, D_FF), 2), _const_spec((D_MODEL, D_FF), 2),
                  _const_spec((FFN_CONV_W, D_FF), 2), _const_spec((1, D_FF), 2), _const_spec((D_FF, D_MODEL), 2),
                  _const_spec((1, D_MODEL), 2)],
        out_specs=[pl.BlockSpec((1, TF, D_MODEL), lambda b, t: (b, t, 0)),
                   pl.BlockSpec((1, 8, D_FF), lambda b, t: (b, 0, 0)),
                   crow],
        out_shape=[jax.ShapeDtypeStruct((B, T, D_MODEL), F32),
                   jax.ShapeDtypeStruct((B, 8, D_FF), F32),
                   jax.ShapeDtypeStruct((SR, D_MODEL), F32)],
        scratch_shapes=[pltpu.VMEM((TF, D_MEM), BF16),
                        pltpu.VMEM((TF + 8, FF_CHUNK), F32),
                        pltpu.VMEM((8, D_FF), F32),
                        pltpu.VMEM((C_ROWS, D_MEM), F32),
                        pltpu.VMEM((C_ROWS, D_MEM), F32)],
        compiler_params=_params(("arbitrary", "arbitrary")),
        name="prompt_ffn",
    )(x1p, mkb, mvb, x1s, cmk, cmv, gcross, wq, wo, gffn, wg, wu, wfc, bfc, wd, gfin)

    ffn_prev_tm = jnp.transpose(state_ffn_conv[0], (1, 0, 2))
    slab = pltpu.VMEM((FF_CHUNK // 128, SR, 128), F32)
    y_s, gate_tm = pl.pallas_call(
        _sample_ffn_kernel,
        grid=(1,),
        in_specs=[_const_spec((SR, D_MODEL), 1), _const_spec((FFN_CONV_W - 1, NB, D_FF), 1),
                  _const_spec((1, D_MODEL), 1), _const_spec((D_MODEL, D_FF), 1), _const_spec((D_MODEL, D_FF), 1),
                  _const_spec((FFN_CONV_W, D_FF), 1), _const_spec((1, D_FF), 1), _const_spec((D_FF, D_MODEL), 1),
                  _const_spec((1, D_MODEL), 1)],
        out_specs=[pl.BlockSpec((SR, D_MODEL), lambda i: (0, 0)),
                   pl.BlockSpec((FFN_CONV_W - 1, NB, D_FF), lambda i: (0, 0, 0))],
        out_shape=[jax.ShapeDtypeStruct((SR, D_MODEL), F32),
                   jax.ShapeDtypeStruct((FFN_CONV_W - 1, NB, D_FF), F32)],
        scratch_shapes=[slab, slab, slab],
        compiler_params=_params(("arbitrary",)),
        name="sample_ffn",
    )(x2s, ffn_prev_tm, gffn, wg, wu, wfc, bfc, wd, gfin)

    p_swa_k = pk.reshape(1, B, WINDOW, N_KV_HEADS, HEAD_DIM)
    p_swa_v = pv.reshape(1, B, WINDOW, N_KV_HEADS, HEAD_DIM)
    p_mem_k = mk.reshape(1, B, MEM_LEN, N_MEM_HEADS, MEM_HEAD_DIM)
    p_mem_v = mv.reshape(1, B, MEM_LEN, N_MEM_HEADS, MEM_HEAD_DIM)
    p_lru_conv = pconv8[None, :, 8 - (LRU_CONV_W - 1):, :]
    p_lru_h = ph8[None, :, 0, :]
    p_ffn_conv = pffn8[None, :, 8 - (FFN_CONV_W - 1):, :]
    y_sample = y_s.reshape(NB, 4, D_MODEL)
    s_swa_k = jnp.transpose(sk.reshape(NB, N_KV_HEADS, HEAD_DIM, WINDOW), (0, 3, 1, 2))[None]
    s_swa_v = jnp.transpose(sv.reshape(NB, N_KV_HEADS, HEAD_DIM, WINDOW), (0, 3, 1, 2))[None]
    s_lru_conv = xr_s.reshape(NB, 4, D_LRU)[None, :, 1:, :]
    s_lru_h = h_s.reshape(NB, 4, D_LRU)[None, :, 3, :]
    s_ffn_conv = jnp.transpose(gate_tm, (1, 0, 2))[None]
    return (y_prompt, y_sample, p_swa_k, p_swa_v, p_mem_k, p_mem_v, p_lru_conv, p_lru_h, p_ffn_conv,
            s_swa_k, s_swa_v, s_lru_conv, s_lru_h, s_ffn_conv)
```

```python
import functools

import jax
import jax.numpy as jnp
from jax import lax
from jax.experimental import pallas as pl
from jax.experimental.pallas import tpu as pltpu

D_MODEL = 1024
D_LRU = 512
LRU_BLOCK = 64
LRU_CONV_W = 4
LRU_C = 8.0
N_Q_HEADS = 8
N_KV_HEADS = 2
HEAD_DIM = 64
D_ATTN = 512
D_KV = 128
WINDOW = 128
D_IN = 1792
MEM_LEN = 256
N_MEM_HEADS = 4
MEM_HEAD_DIM = 128
D_MEM = 512
D_FF = 3072
FFN_CONV_W = 3
EPS = 1e-6
NEG_INF = -1e30

F32 = jnp.float32
BF16 = jnp.bfloat16

SLOPES = [float(2.0 ** (-8.0 * (i + 1) / N_Q_HEADS)) for i in range(N_Q_HEADS)]
ATTN_SCALE = HEAD_DIM ** -0.5
MEM_SCALE = MEM_HEAD_DIM ** -0.5
LOG2E = 1.4426950408889634
F32_TINY = 1.1754944e-38

TM = 1024
HR = 512
SEG = HR // 8
TF = 512
FF_CHUNK = 1024
S_NB = 32
S_ROWS = 4 * S_NB
C_NB = 4
C_ROWS = 4 * C_NB
VMEM_LIMIT = 56 * 1024 * 1024


def _dot(a, b):
    return jnp.dot(a, b, preferred_element_type=F32)


def _dot_nt(a, b):
    return lax.dot_general(a, b, (((1,), (1,)), ((), ())), preferred_element_type=F32)


def _rmsnorm(x, g):
    ms = jnp.mean(x * x, axis=-1, keepdims=True)
    return x * lax.rsqrt(ms + EPS) * g


def _gelu(x):
    c = 0.7978845608028654
    return x * (0.5 * (1.0 + jnp.tanh(c * (x + 0.044715 * (x * x * x)))))


def _sigmoid(x):
    return 1.0 / (1.0 + jnp.exp(-x))


def _softplus(x):
    return jnp.maximum(x, 0.0) + jnp.log1p(jnp.exp(-jnp.abs(x)))


def _lru_gates(xc, wax_ref, ba, bx, lam):
    xcb = xc.astype(BF16)
    pa, px = [], []
    for gi in range(2):
        pre = _dot(xcb[:, gi * 256:(gi + 1) * 256], wax_ref[gi])
        pa.append(pre[:, :256])
        px.append(pre[:, 256:])
    r = _sigmoid(jnp.concatenate(pa, axis=1) + ba)
    i = _sigmoid(jnp.concatenate(px, axis=1) + bx)
    log_a = (-LRU_C * _softplus(-lam)) * r
    a = jnp.exp(log_a)
    om = -jnp.tanh(log_a) * (a * a + 1.0)
    b = (om * lax.rsqrt(jnp.maximum(om, F32_TINY))) * (i * xc)
    return a, b


def _head_variants(t):
    lo = lax.broadcasted_iota(jnp.int32, t.shape, 1) < HEAD_DIM
    tr = pltpu.roll(t, HEAD_DIM, 1)
    z = jnp.zeros_like(t)
    x0 = jnp.where(lo, t, z)
    y0 = jnp.where(lo, z, tr)
    x1 = jnp.where(lo, tr, z)
    y1 = jnp.where(lo, z, t)
    return [v.astype(BF16) for v in (x0, y0, x1, y1)]


def _mem_kv_kernel(mem_ref, g_ref, wk_ref, wv_ref, mk_ref, mv_ref, mkb_ref, mvb_ref):
    n = _rmsnorm(mem_ref[0], g_ref[...]).astype(BF16)
    mk = _dot(n, wk_ref[...])
    mv = _dot(n, wv_ref[...])
    mk_ref[0] = mk
    mv_ref[0] = mv
    mkb_ref[0] = mk.astype(BF16)
    mvb_ref[0] = mv.astype(BF16)


def _prompt_mixer_kernel(sinks_ref, x_ref, g_ref, win_ref, wconv_ref, bconv_ref, wax_ref, ba_ref, bx_ref,
                         lam_ref, wout_ref,
                         x1_ref, pk_ref, pv_ref, pconv_ref, ph_ref,
                         proj_s, pin_s, pout_s, xtail_s, a_s, b_s, hl_s, h_s, hc_s, kv_s, ymix_s):
    t = pl.program_id(1)

    @pl.when(t == 0)
    def _():
        xtail_s[...] = jnp.zeros((24, D_LRU), F32)
        hc_s[...] = jnp.zeros((8, D_LRU), F32)
        kv_s[:, 0:WINDOW, :] = jnp.zeros((8, WINDOW, D_KV), BF16)

    def sec(i):
        return slice(i * HR, (i + 1) * HR)

    def in_proj(i):
        n1 = _rmsnorm(x_ref[0, sec(i), :], g_ref[...]).astype(BF16)
        proj_s[sec(i), :] = _dot(n1, win_ref[...])

    def seg_rows(j):
        return pl.ds(64 * (j % (SEG // 8)) + j // (SEG // 8), 8, stride=8)

    row8 = lax.broadcasted_iota(jnp.int32, (8, D_LRU), 0)

    def lru_gates(i):
        r0 = i * HR
        for j in range(HR // 8):
            for l in range(D_LRU // 128):
                pin_s[l, seg_rows(j), :] = proj_s[r0 + 8 * j:r0 + 8 * j + 8, l * 128:(l + 1) * 128]
        xs = jnp.concatenate([pin_s[l] for l in range(D_LRU // 128)], axis=1)
        tail = xs[HR - 24:, :]
        prev = xtail_s[...]
        xtail_s[...] = tail
        heads = [jnp.where(row8 == 0, pltpu.roll(prev[8 * q:8 * q + 8, :], 1, 0),
                           pltpu.roll(tail[8 * q:8 * q + 8, :], 1, 0)) for q in range(3)]
        sh1 = jnp.concatenate(heads[2:] + [xs[:HR - 8, :]], axis=0)
        sh2 = jnp.concatenate(heads[1:] + [xs[:HR - 16, :]], axis=0)
        sh3 = jnp.concatenate(heads + [xs[:HR - 24, :]], axis=0)
        wc = wconv_ref[...]
        xc = bconv_ref[...] + wc[0:1] * sh3 + wc[1:2] * sh2 + wc[2:3] * sh1 + wc[3:4] * xs
        a, b = _lru_gates(xc, wax_ref, ba_ref[...], bx_ref[...], lam_ref[...])
        a_s[sec(i), :] = a
        b_s[sec(i), :] = b

    def lru_scan(i, hcar):
        r0 = i * HR
        hl = b_s[r0:r0 + 8, :]
        ac = a_s[r0:r0 + 8, :]
        hl_s[0:8, :] = hl
        for g in range(1, SEG):
            av = a_s[r0 + 8 * g:r0 + 8 * g + 8, :]
            hl = av * hl + b_s[r0 + 8 * g:r0 + 8 * g + 8, :]
            ac = av * ac
            hl_s[8 * g:8 * g + 8, :] = hl
            a_s[r0 + 8 * g:r0 + 8 * g + 8, :] = ac
        hin = hcar
        for s in range(8):
            hend = hl + ac * hin
            if s < 7:
                hin = jnp.where(row8 == s + 1, pltpu.roll(hend, 1, 0), hin)
        hcar = jnp.broadcast_to(hend[7:8, :], (8, D_LRU))
        for g in range(SEG):
            hg = hl_s[8 * g:8 * g + 8, :] + a_s[r0 + 8 * g:r0 + 8 * g + 8, :] * hin
            for l in range(D_LRU // 128):
                pout_s[l, 8 * g:8 * g + 8, :] = hg[:, l * 128:(l + 1) * 128]
        for j in range(HR // 8):
            h_s[r0 + 8 * j:r0 + 8 * j + 8, :] = jnp.concatenate(
                [pout_s[l, seg_rows(j), :] for l in range(D_LRU // 128)], axis=1)
        gate = proj_s[sec(i), D_LRU:2 * D_LRU]
        ymix_s[sec(i), 0:D_LRU] = (h_s[sec(i), :] * _gelu(gate)).astype(BF16)
        return hcar

    def kv_prep(i):
        k = proj_s[sec(i), 1536:1664]
        v = proj_s[sec(i), 1664:1792]
        for n, arr in enumerate(_head_variants(k) + _head_variants(v)):
            kv_s[n, WINDOW + i * HR:WINDOW + (i + 1) * HR, :] = arr

    qi = lax.broadcasted_iota(jnp.int32, (WINDOW, 2 * WINDOW), 0)
    kj = lax.broadcasted_iota(jnp.int32, (WINDOW, 2 * WINDOW), 1)
    dist = qi + WINDOW - kj
    valid = (dist >= 0) & (dist < WINDOW)
    valid0 = valid & (kj >= jnp.where(t == 0, WINDOW, 0))
    distf = dist.astype(F32)
    bias = [jnp.where(valid, -SLOPES[h] * distf, NEG_INF) for h in range(N_Q_HEADS)]
    bias0 = [jnp.where(valid0, -SLOPES[h] * distf, NEG_INF) for h in range(N_Q_HEADS)]
    lane_lo = lax.broadcasted_iota(jnp.int32, (WINDOW, 2 * HEAD_DIM), 1) < HEAD_DIM

    r512 = lax.broadcasted_iota(jnp.int32, (4 * WINDOW, 2 * HEAD_DIM), 0)
    l512 = lax.broadcasted_iota(jnp.int32, (4 * WINDOW, 2 * HEAD_DIM), 1)
    ones_cols = jnp.where((r512 < 2 * WINDOW) == (l512 < HEAD_DIM), 1.0, 0.0).astype(BF16)
    n_iter = (HR // WINDOW) * 4

    def qk(n):
        j, c = divmod(n, 4)
        hk = c // 2
        rows = slice(j * WINDOW, (j + 1) * WINDOW)
        win = slice(j * WINDOW, (j + 2) * WINDOW)
        qc = (proj_s[rows, 1024 + c * 128:1024 + (c + 1) * 128] * ATTN_SCALE).astype(BF16)
        kcat = jnp.concatenate([kv_s[2 * hk, win, :], kv_s[2 * hk + 1, win, :]], axis=0)
        return _dot_nt(qc, kcat)

    def softmax_pv(n, s):
        j, c = divmod(n, 4)
        hk = c // 2
        rows = slice(j * WINDOW, (j + 1) * WINDOW)
        win = slice(j * WINDOW, (j + 2) * WINDOW)
        vcat = jnp.concatenate([kv_s[4 + 2 * hk, win, :], kv_s[5 + 2 * hk, win, :]], axis=0)
        vaug = jnp.concatenate([vcat, ones_cols], axis=1)
        ps, es = [], []
        for half in range(2):
            h = 2 * c + half
            sink = sinks_ref[h]
            sh = s[:, half * 256:(half + 1) * 256] + (bias0[h] if j == 0 else bias[h])
            m = jnp.maximum(jnp.max(sh, axis=-1, keepdims=True), sink)
            ps.append(jnp.exp(sh - m).astype(BF16))
            es.append(jnp.exp(sink - m))
        oa = _dot(jnp.concatenate(ps, axis=1), vaug)
        den = oa[:, 128:256] + jnp.where(lane_lo, es[0], es[1])
        ymix_s[rows, D_LRU + c * 128:D_LRU + (c + 1) * 128] = (oa[:, 0:128] * (1.0 / den)).astype(BF16)

    def attention(i):
        depth = 2
        base = i * n_iter
        pend = [qk(base + n) for n in range(depth)]
        for n in range(n_iter):
            s = pend.pop(0)
            if n + depth < n_iter:
                pend.append(qk(base + n + depth))
            softmax_pv(base + n, s)

    def out_proj(i):
        x1_ref[0, sec(i), :] = x_ref[0, sec(i), :] + _dot(ymix_s[sec(i), :], wout_ref[...])

    n_sec = TM // HR
    h = hc_s[...]
    in_proj(0)
    lru_gates(0)
    for i in range(n_sec):
        if i + 1 < n_sec:
            in_proj(i + 1)
        h = lru_scan(i, h)
        kv_prep(i)
        attention(i)
        if i + 1 < n_sec:
            lru_gates(i + 1)
        out_proj(i)

    hc_s[...] = h
    ph_ref[0] = h
    pconv_ref[0] = proj_s[TM - 8:, 0:D_LRU]
    pk_ref[0] = proj_s[TM - WINDOW:, 1536:1664]
    pv_ref[0] = proj_s[TM - WINDOW:, 1664:1792]
    kv_s[:, 0:WINDOW, :] = kv_s[:, TM:TM + WINDOW, :]


def _ffn_chunks(n3, acc, gprev_fn, wg_ref, wu_ref, wfc_ref, bfc_ref, wd_ref, on_gate):
    nc = D_FF // FF_CHUNK

    def up(c):
        cs = slice(c * FF_CHUNK, (c + 1) * FF_CHUNK)
        return _dot(n3, wg_ref[:, cs]), _dot(n3, wu_ref[:, cs])

    nxt = up(0)
    for c in range(nc):
        cs = slice(c * FF_CHUNK, (c + 1) * FF_CHUNK)
        g, u = nxt
        if c + 1 < nc:
            nxt = up(c + 1)
        g2, g1 = gprev_fn(c, g)
        on_gate(c, g)
        wfc = wfc_ref[:, cs]
        conv = bfc_ref[:, cs] + wfc[0:1] * g2 + wfc[1:2] * g1 + wfc[2:3] * g
        hmid = (_gelu(conv) * u).astype(BF16)
        acc = acc + _dot(hmid, wd_ref[cs, :])
    return acc


def _prompt_ffn_kernel(x1_ref, mk_ref, mv_ref, xs1_ref, cmk_ref, cmv_ref, gc_ref, wq_ref, wo_ref, gf_ref,
                       wg_ref, wu_ref, wfc_ref, bfc_ref, wd_ref, gfin_ref,
                       y_ref, pffn_ref, xs2_ref,
                       oc_s, gbuf_s, gcar_s, sq_s, soc_s):
    t = pl.program_id(1)

    @pl.when(t == 0)
    def _():
        gcar_s[...] = jnp.zeros((8, D_FF), F32)

    _sample_cross(xs1_ref, cmk_ref, cmv_ref, gc_ref, wq_ref, wo_ref, xs2_ref, sq_s, soc_s)

    x1 = x1_ref[0]
    qc = _dot(_rmsnorm(x1, gc_ref[...]).astype(BF16), wq_ref[...]).astype(BF16)
    hsl = [slice(h * MEM_HEAD_DIM, (h + 1) * MEM_HEAD_DIM) for h in range(N_MEM_HEADS)]
    ss = [_dot_nt(qc[:, hs], mk_ref[0, :, hs]) for hs in hsl]
    for h, hs in enumerate(hsl):
        s = ss[h]
        m = jnp.max(s, axis=-1, keepdims=True)
        p = jnp.exp2((s - m) * (MEM_SCALE * LOG2E))
        l = jnp.sum(p, axis=-1, keepdims=True)
        o = _dot(p.astype(BF16), mv_ref[0, :, hs]) * (1.0 / l)
        oc_s[:, hs] = o.astype(BF16)
    x2 = x1 + _dot(oc_s[...], wo_ref[...])
    n3 = _rmsnorm(x2, gf_ref[...]).astype(BF16)

    def gprev(c, g):
        cs = slice(c * FF_CHUNK, (c + 1) * FF_CHUNK)
        gbuf_s[0:8, :] = gcar_s[:, cs]
        gbuf_s[8:TF + 8, :] = g
        return gbuf_s[6:6 + TF, :], gbuf_s[7:7 + TF, :]

    def on_gate(c, g):
        cs = slice(c * FF_CHUNK, (c + 1) * FF_CHUNK)
        tail = g[TF - 8:, :]
        gcar_s[:, cs] = tail
        pffn_ref[0, :, cs] = tail

    acc = _ffn_chunks(n3, x2, gprev, wg_ref, wu_ref, wfc_ref, bfc_ref, wd_ref, on_gate)
    y_ref[0] = _rmsnorm(acc, gfin_ref[...])


def _sample_mixer_kernel(sinks_ref, x_ref, prev4_ref, h0_ref, ck_ref, cv_ref, g_ref, win_ref, wconv_ref, bconv_ref,
                         wax_ref, ba_ref, bx_ref, lam_ref, wout_ref, wkt_ref, wvt_ref,
                         x1_ref, sk_ref, sv_ref, xr_ref, h_ref,
                         q_s, yatt_s):
    R = S_ROWS
    x = x_ref[...]
    n1 = _rmsnorm(x, g_ref[...]).astype(BF16)
    proj = _dot(n1, win_ref[...])
    xr = proj[:, 0:D_LRU]
    gate = proj[:, D_LRU:2 * D_LRU]
    q_s[...] = proj[:, 1024:1536] * ATTN_SCALE
    xr_ref[...] = xr

    tmod = lax.broadcasted_iota(jnp.int32, (R, D_LRU), 0) & 3
    prev4 = prev4_ref[...]
    xs1 = jnp.where(tmod >= 1, pltpu.roll(xr, 1, 0), pltpu.roll(prev4, R - 2, 0))
    xs2 = jnp.where(tmod >= 2, pltpu.roll(xr, 2, 0), pltpu.roll(prev4, R - 1, 0))
    xs3 = jnp.where(tmod >= 3, pltpu.roll(xr, 3, 0), prev4)
    wc = wconv_ref[...]
    xc = bconv_ref[...] + wc[0:1] * xs3 + wc[1:2] * xs2 + wc[2:3] * xs1 + wc[3:4] * xr

    a, b = _lru_gates(xc, wax_ref, ba_ref[...], bx_ref[...], lam_ref[...])
    for s in (1, 2):
        m = tmod >= s
        a_sh = pltpu.roll(a, s, 0)
        b_sh = pltpu.roll(b, s, 0)
        b = jnp.where(m, a * b_sh + b, b)
        a = jnp.where(m, a * a_sh, a)
    h = a * h0_ref[...] + b
    h_ref[...] = h
    y_lru = h * _gelu(gate)

    kt_new = _dot_nt(wkt_ref[...], n1)
    vt_new = _dot_nt(wvt_ref[...], n1)
    kt_pad, vt_pad = kt_new, vt_new
    if R < WINDOW:
        zpad = jnp.zeros((D_KV, WINDOW - R), F32)
        kt_pad = jnp.concatenate([kt_new, zpad], axis=1)
        vt_pad = jnp.concatenate([vt_new, zpad], axis=1)
    kt_pad_b = kt_pad.astype(BF16)
    vt_pad_b = vt_pad.astype(BF16)

    r64 = lax.broadcasted_iota(jnp.int32, (8 * N_Q_HEADS, WINDOW), 0)
    c64 = lax.broadcasted_iota(jnp.int32, (8 * N_Q_HEADS, WINDOW), 1)
    t64 = r64 & 3
    slope = jnp.zeros((8 * N_Q_HEADS, WINDOW), F32)
    sinkcol = jnp.zeros((8 * N_Q_HEADS, 1), F32)
    for hq in range(N_Q_HEADS):
        slope = jnp.where((r64 >> 3) == hq, SLOPES[hq], slope)
        sinkcol = jnp.where((r64[:, 0:1] >> 3) == hq, sinks_ref[hq], sinkcol)
    bias_c = jnp.where(c64 > t64, -slope * (t64 + WINDOW - c64).astype(F32), NEG_INF)
    bat0 = (r64 & 7) < 4
    lane_lo = lax.broadcasted_iota(jnp.int32, (8, D_KV), 1) < HEAD_DIM
    lane128 = lax.broadcasted_iota(jnp.int32, (D_KV, WINDOW), 1)

    def pair_scores(p):
        r0 = p * 8
        q8 = q_s[r0:r0 + 8, :]
        blocks = []
        for hq in range(N_Q_HEADS):
            chunk = q8[:, (hq // 2) * 128:(hq // 2 + 1) * 128]
            hk = hq // (N_Q_HEADS // N_KV_HEADS)
            src = chunk if (hq % 2) == hk else pltpu.roll(chunk, HEAD_DIM, 1)
            blocks.append(jnp.where(lane_lo == (hk == 0), src, 0.0))
        lhs = jnp.concatenate(blocks, axis=0).astype(BF16)
        sb = []
        for bb in range(2):
            bidx = 2 * p + bb
            kt = ck_ref[bidx]
            vt = cv_ref[bidx]
            sb.append(_dot(lhs, kt.astype(BF16)))
            shift = (WINDOW - 4 - 4 * bidx) % WINDOW
            sk_ref[bidx] = jnp.where(lane128 >= WINDOW - 4, pltpu.roll(kt_pad, shift, 1),
                                     pltpu.roll(kt, WINDOW - 4, 1))
            sv_ref[bidx] = jnp.where(lane128 >= WINDOW - 4, pltpu.roll(vt_pad, shift, 1),
                                     pltpu.roll(vt, WINDOW - 4, 1))
        s_c = jnp.where(bat0, sb[0], sb[1]) + bias_c
        same = (c64 >> 2) == (2 * p + ((r64 & 7) >> 2))
        dn = t64 - (c64 & 3)
        bias_n = jnp.where(same & (dn >= 0), -slope * dn.astype(F32), NEG_INF)
        s_n = _dot(lhs, kt_pad_b) + bias_n
        return s_c, s_n

    def pair_finish(p, s_c, s_n):
        r0 = p * 8
        m = jnp.maximum(jnp.maximum(jnp.max(s_c, axis=-1, keepdims=True), jnp.max(s_n, axis=-1, keepdims=True)),
                        sinkcol)
        pc = jnp.exp(s_c - m)
        pn = jnp.exp(s_n - m)
        l = jnp.sum(pc, axis=-1, keepdims=True) + jnp.sum(pn, axis=-1, keepdims=True) + jnp.exp(sinkcol - m)
        pcb = pc.astype(BF16)
        ob = [_dot_nt(pcb, cv_ref[2 * p + bb].astype(BF16)) for bb in range(2)]
        o = (jnp.where(bat0, ob[0], ob[1]) + _dot_nt(pn.astype(BF16), vt_pad_b)) * (1.0 / l)
        for c in range(N_Q_HEADS // 2):
            hk = c // 2
            ev = o[16 * c:16 * c + 8, :]
            od = o[16 * c + 8:16 * c + 16, :]
            if hk == 0:
                od = pltpu.roll(od, HEAD_DIM, 1)
            else:
                ev = pltpu.roll(ev, HEAD_DIM, 1)
            yatt_s[r0:r0 + 8, c * 128:(c + 1) * 128] = jnp.where(lane_lo, ev, od)

    n_pairs = S_NB // 2
    cur = pair_scores(0)
    for p in range(n_pairs):
        nxt = pair_scores(p + 1) if p + 1 < n_pairs else None
        pair_finish(p, *cur)
        cur = nxt

    ymix = jnp.concatenate([y_lru, yatt_s[...]], axis=1).astype(BF16)
    x1_ref[...] = x + _dot(ymix, wout_ref[...])


def _load_mem_heads(ref, b):
    return jnp.concatenate([ref[b, pl.ds(h, MEM_LEN, stride=N_MEM_HEADS), :] for h in range(N_MEM_HEADS)],
                           axis=1).astype(BF16)


def _sample_cross(x1_ref, mk_ref, mv_ref, gc_ref, wq_ref, wo_ref, x2_ref, q_s, oc_s):
    lane_head = lax.broadcasted_iota(jnp.int32, (8, D_MEM), 1) // MEM_HEAD_DIM
    bat0 = (lax.broadcasted_iota(jnp.int32, (32, D_MEM), 0) & 7) < 4
    bat0_s = (lax.broadcasted_iota(jnp.int32, (32, MEM_LEN), 0) & 7) < 4

    def q_proj():
        q_s[...] = _dot(_rmsnorm(x1_ref[...], gc_ref[...]).astype(BF16), wq_ref[...])

    def scores(p):
        q8 = q_s[p * 8:(p + 1) * 8, :]
        lhs = jnp.concatenate([jnp.where(lane_head == h, q8, 0.0) for h in range(N_MEM_HEADS)],
                              axis=0).astype(BF16)
        sb = [_dot_nt(lhs, _load_mem_heads(mk_ref, 2 * p + bb)) for bb in range(2)]
        return jnp.where(bat0_s, sb[0], sb[1]) * MEM_SCALE

    def attend():
        ss = [scores(p) for p in range(C_NB // 2)]
        for p, s in enumerate(ss):
            finish(p, s)

    def finish(p, s):
        m = jnp.max(s, axis=-1, keepdims=True)
        pe = jnp.exp(s - m)
        l = jnp.sum(pe, axis=-1, keepdims=True)
        pb = pe.astype(BF16)
        ob = [_dot(pb, _load_mem_heads(mv_ref, 2 * p + bb)) for bb in range(2)]
        o = jnp.where(bat0, ob[0], ob[1]) * (1.0 / l)
        out = jnp.where(lane_head == 0, o[0:8], 0.0)
        for h in range(1, N_MEM_HEADS):
            out = out + jnp.where(lane_head == h, o[h * 8:(h + 1) * 8], 0.0)
        oc_s[p * 8:(p + 1) * 8, :] = out

    def out_proj():
        x2_ref[...] = x1_ref[...] + _dot(oc_s[...].astype(BF16), wo_ref[...])

    q_proj()
    attend()
    out_proj()


def _sample_ffn_kernel(x2_ref, prev_ref, gf_ref, wg_ref, wu_ref, wfc_ref, bfc_ref, wd_ref, gfin_ref,
                       y_ref, gate_ref, slab_s, p1_s, p2_s):
    R = x2_ref.shape[0]
    nb = R // 4
    nl = FF_CHUNK // 128
    x2 = x2_ref[...]
    n3 = _rmsnorm(x2, gf_ref[...]).astype(BF16)
    tmod = lax.broadcasted_iota(jnp.int32, (R, FF_CHUNK), 0) & 3
    p1_s[...] = jnp.zeros(p1_s.shape, F32)
    p2_s[...] = jnp.zeros(p2_s.shape, F32)

    def gprev(c, g):
        for l in range(nl):
            cols = slice(c * FF_CHUNK + l * 128, c * FF_CHUNK + (l + 1) * 128)
            s0 = prev_ref[0, :, cols]
            s1 = prev_ref[1, :, cols]
            p1_s[l, pl.ds(0, nb, stride=4), :] = s1
            p2_s[l, pl.ds(0, nb, stride=4), :] = s0
            p2_s[l, pl.ds(1, nb, stride=4), :] = s1
        p1 = jnp.concatenate([p1_s[l] for l in range(nl)], axis=1)
        p2 = jnp.concatenate([p2_s[l] for l in range(nl)], axis=1)
        g1 = jnp.where(tmod >= 1, pltpu.roll(g, 1, 0), p1)
        g2 = jnp.where(tmod >= 2, pltpu.roll(g, 2, 0), p2)
        return g2, g1

    def on_gate(c, g):
        for l in range(nl):
            slab_s[l] = g[:, l * 128:(l + 1) * 128]
        for tt in range(2):
            gate_ref[tt, :, c * FF_CHUNK:(c + 1) * FF_CHUNK] = jnp.concatenate(
                [slab_s[l, pl.ds(2 + tt, nb, stride=4), :] for l in range(nl)], axis=1)

    acc = _ffn_chunks(n3, x2, gprev, wg_ref, wu_ref, wfc_ref, bfc_ref, wd_ref, on_gate)
    y_ref[...] = _rmsnorm(acc, gfin_ref[...])


def _const_spec(shape, grid_rank):
    zeros = (0,) * len(shape)
    if grid_rank == 1:
        return pl.BlockSpec(shape, lambda i: zeros, pipeline_mode=pl.Buffered(1))
    return pl.BlockSpec(shape, lambda i, j: zeros, pipeline_mode=pl.Buffered(1))


def _block_diag4(w):
    eye = jnp.eye(4, dtype=w.dtype)
    return (w[:, :, None, :] * eye[:, None, :, None]).reshape(4 * LRU_BLOCK, 4 * LRU_BLOCK)


def _params(sem):
    return pltpu.CompilerParams(dimension_semantics=sem, vmem_limit_bytes=VMEM_LIMIT)


def kernel(x_prompt, x_sample, cache_swa_k, cache_swa_v, cache_mem_k, cache_mem_v, state_lru_conv, state_lru_h, state_ffn_conv, mem_prompt, g_mix, w_in, w_lru_conv, b_lru_conv, w_lru_a, b_lru_a, w_lru_x, b_lru_x, lru_lambda, attn_sinks, w_out, g_cross, g_mem, w_mem_q, w_mem_k, w_mem_v, w_mem_o, g_ffn, w_ffn_gate, w_ffn_up, w_ffn_conv, b_ffn_conv, w_ffn_down, g_final):
    B, T, _ = x_prompt.shape
    NB = x_sample.shape[0]
    NT = T // TM
    SR = NB * 4

    win = w_in[0].astype(BF16)
    wout = w_out[0].astype(BF16)
    wq = w_mem_q[0].astype(BF16)
    wk = w_mem_k[0].astype(BF16)
    wv = w_mem_v[0].astype(BF16)
    wo = w_mem_o[0].astype(BF16)
    wg = w_ffn_gate[0].astype(BF16)
    wu = w_ffn_up[0].astype(BF16)
    wd = w_ffn_down[0].astype(BF16)
    wax = jnp.stack([
        jnp.concatenate([_block_diag4(w_lru_a[0, 4 * gi:4 * gi + 4]), _block_diag4(w_lru_x[0, 4 * gi:4 * gi + 4])],
                        axis=1) for gi in range(2)]).astype(BF16)
    gmix, gcross, gmem, gffn = g_mix, g_cross, g_mem, g_ffn
    gfin = g_final.reshape(1, D_MODEL)
    wconv, bconv = w_lru_conv[0], b_lru_conv
    ba, bx, lam = b_lru_a, b_lru_x, lru_lambda
    wfc, bfc = w_ffn_conv[0], b_ffn_conv
    sinks = attn_sinks[0]
    smem = pl.BlockSpec(memory_space=pltpu.SMEM)

    mk, mv, mkb, mvb = pl.pallas_call(
        _mem_kv_kernel,
        grid=(B,),
        in_specs=[pl.BlockSpec((1, MEM_LEN, D_MODEL), lambda b: (b, 0, 0)),
                  _const_spec((1, D_MODEL), 1), _const_spec((D_MODEL, D_MEM), 1), _const_spec((D_MODEL, D_MEM), 1)],
        out_specs=[pl.BlockSpec((1, MEM_LEN, D_MEM), lambda b: (b, 0, 0))] * 4,
        out_shape=[jax.ShapeDtypeStruct((B, MEM_LEN, D_MEM), F32)] * 2
        + [jax.ShapeDtypeStruct((B, MEM_LEN, D_MEM), BF16)] * 2,
        compiler_params=_params(("arbitrary",)),
        name="mem_kv",
    )(mem_prompt, gmem, wk, wv)

    x1p, pk, pv, pconv8, ph8 = pl.pallas_call(
        _prompt_mixer_kernel,
        grid=(B, NT),
        in_specs=[smem,
                  pl.BlockSpec((1, TM, D_MODEL), lambda b, t: (b, t, 0)),
                  _const_spec((1, D_MODEL), 2), _const_spec((D_MODEL, D_IN), 2),
                  _const_spec((LRU_CONV_W, D_LRU), 2), _const_spec((1, D_LRU), 2),
                  _const_spec((2, 256, 512), 2), _const_spec((1, D_LRU), 2), _const_spec((1, D_LRU), 2),
                  _const_spec((1, D_LRU), 2), _const_spec((D_MODEL, D_MODEL), 2)],
        out_specs=[pl.BlockSpec((1, TM, D_MODEL), lambda b, t: (b, t, 0)),
                   pl.BlockSpec((1, WINDOW, D_KV), lambda b, t: (b, 0, 0)),
                   pl.BlockSpec((1, WINDOW, D_KV), lambda b, t: (b, 0, 0)),
                   pl.BlockSpec((1, 8, D_LRU), lambda b, t: (b, 0, 0)),
                   pl.BlockSpec((1, 8, D_LRU), lambda b, t: (b, 0, 0))],
        out_shape=[jax.ShapeDtypeStruct((B, T, D_MODEL), F32),
                   jax.ShapeDtypeStruct((B, WINDOW, D_KV), F32),
                   jax.ShapeDtypeStruct((B, WINDOW, D_KV), F32),
                   jax.ShapeDtypeStruct((B, 8, D_LRU), F32),
                   jax.ShapeDtypeStruct((B, 8, D_LRU), F32)],
        scratch_shapes=[pltpu.VMEM((TM, D_IN), F32),
                        pltpu.VMEM((D_LRU // 128, HR, 128), F32),
                        pltpu.VMEM((D_LRU // 128, HR, 128), F32),
                        pltpu.VMEM((24, D_LRU), F32),
                        pltpu.VMEM((TM, D_LRU), F32),
                        pltpu.VMEM((TM, D_LRU), F32),
                        pltpu.VMEM((HR, D_LRU), F32),
                        pltpu.VMEM((TM, D_LRU), F32),
                        pltpu.VMEM((8, D_LRU), F32),
                        pltpu.VMEM((8, TM + WINDOW, D_KV), BF16),
                        pltpu.VMEM((TM, D_MODEL), BF16)],
        compiler_params=_params(("arbitrary", "arbitrary")),
        name="prompt_mixer",
    )(sinks, x_prompt, gmix, win, wconv, bconv, wax, ba, bx, lam, wout)

    xs = x_sample.reshape(SR, D_MODEL)
    conv_prev4 = jnp.pad(state_lru_conv[0], ((0, 0), (0, 1), (0, 0))).reshape(SR, D_LRU)
    h0rep = jnp.repeat(state_lru_h[0], 4, axis=0)
    ck = jnp.transpose(cache_swa_k[0], (0, 2, 3, 1)).reshape(NB, D_KV, WINDOW)
    cv = jnp.transpose(cache_swa_v[0], (0, 2, 3, 1)).reshape(NB, D_KV, WINDOW)
    wkt = jnp.transpose(w_in[0][:, 1536:1664]).astype(BF16)
    wvt = jnp.transpose(w_in[0][:, 1664:1792]).astype(BF16)
    row_spec = lambda w: pl.BlockSpec((S_ROWS, w), lambda i: (i, 0))
    cache_spec = pl.BlockSpec((S_NB, D_KV, WINDOW), lambda i: (i, 0, 0))
    x1s, sk, sv, xr_s, h_s = pl.pallas_call(
        _sample_mixer_kernel,
        grid=(NB // S_NB,),
        in_specs=[smem, row_spec(D_MODEL), row_spec(D_LRU), row_spec(D_LRU), cache_spec, cache_spec,
                  _const_spec((1, D_MODEL), 1), _const_spec((D_MODEL, D_IN), 1),
                  _const_spec((LRU_CONV_W, D_LRU), 1), _const_spec((1, D_LRU), 1),
                  _const_spec((2, 256, 512), 1), _const_spec((1, D_LRU), 1), _const_spec((1, D_LRU), 1),
                  _const_spec((1, D_LRU), 1), _const_spec((D_MODEL, D_MODEL), 1),
                  _const_spec((D_KV, D_MODEL), 1), _const_spec((D_KV, D_MODEL), 1)],
        out_specs=[row_spec(D_MODEL), cache_spec, cache_spec, row_spec(D_LRU), row_spec(D_LRU)],
        out_shape=[jax.ShapeDtypeStruct((SR, D_MODEL), F32),
                   jax.ShapeDtypeStruct((NB, D_KV, WINDOW), F32),
                   jax.ShapeDtypeStruct((NB, D_KV, WINDOW), F32),
                   jax.ShapeDtypeStruct((SR, D_LRU), F32),
                   jax.ShapeDtypeStruct((SR, D_LRU), F32)],
        scratch_shapes=[pltpu.VMEM((S_ROWS, D_ATTN), F32),
                        pltpu.VMEM((S_ROWS, D_ATTN), F32)],
        compiler_params=_params(("arbitrary",)),
        name="sample_mixer",
    )(sinks, xs, conv_prev4, h0rep, ck, cv, gmix, win, wconv, bconv, wax, ba, bx, lam, wout, wkt, wvt)

    cmk = cache_mem_k.reshape(NB, MEM_LEN * N_MEM_HEADS, MEM_HEAD_DIM)
    cmv = cache_mem_v.reshape(NB, MEM_LEN * N_MEM_HEADS, MEM_HEAD_DIM)
    ntf = T // TF
    assert NB == C_NB * B * ntf
    crow = pl.BlockSpec((C_ROWS, D_MODEL), lambda b, t: (b * ntf + t, 0))
    cmem = pl.BlockSpec((C_NB, MEM_LEN * N_MEM_HEADS, MEM_HEAD_DIM), lambda b, t: (b * ntf + t, 0, 0))
    y_prompt, pffn8, x2s = pl.pallas_call(
        _prompt_ffn_kernel,
        grid=(B, ntf),
        in_specs=[pl.BlockSpec((1, TF, D_MODEL), lambda b, t: (b, t, 0)),
                  pl.BlockSpec((1, MEM_LEN, D_MEM), lambda b, t: (b, 0, 0)),
                  pl.BlockSpec((1, MEM_LEN, D_MEM), lambda b, t: (b, 0, 0)),
                  crow, cmem, cmem,
                  _const_spec((1, D_MODEL), 2), _const_spec((D_MODEL, D_MEM), 2), _const_spec((D_MEM, D_MODEL), 2),
                  _const_spec((1, D_MODEL), 2), _const_spec((D_MODEL, D_FF), 2), _const_spec((D_MODEL, D_FF), 2),
                  _const_spec((FFN_CONV_W, D_FF), 2), _const_spec((1, D_FF), 2), _const_spec((D_FF, D_MODEL), 2),
                  _const_spec((1, D_MODEL), 2)],
        out_specs=[pl.BlockSpec((1, TF, D_MODEL), lambda b, t: (b, t, 0)),
                   pl.BlockSpec((1, 8, D_FF), lambda b, t: (b, 0, 0)),
                   crow],
        out_shape=[jax.ShapeDtypeStruct((B, T, D_MODEL), F32),
                   jax.ShapeDtypeStruct((B, 8, D_FF), F32),
                   jax.ShapeDtypeStruct((SR, D_MODEL), F32)],
        scratch_shapes=[pltpu.VMEM((TF, D_MEM), BF16),
                        pltpu.VMEM((TF + 8, FF_CHUNK), F32),
                        pltpu.VMEM((8, D_FF), F32),
                        pltpu.VMEM((C_ROWS, D_MEM), F32),
                        pltpu.VMEM((C_ROWS, D_MEM), F32)],
        compiler_params=_params(("arbitrary", "arbitrary")),
        name="prompt_ffn",
    )(x1p, mkb, mvb, x1s, cmk, cmv, gcross, wq, wo, gffn, wg, wu, wfc, bfc, wd, gfin)

    ffn_prev_tm = jnp.transpose(state_ffn_conv[0], (1, 0, 2))
    slab = pltpu.VMEM((FF_CHUNK // 128, SR, 128), F32)
    y_s, gate_tm = pl.pallas_call(
        _sample_ffn_kernel,
        grid=(1,),
        in_specs=[_const_spec((SR, D_MODEL), 1), _const_spec((FFN_CONV_W - 1, NB, D_FF), 1),
                  _const_spec((1, D_MODEL), 1), _const_spec((D_MODEL, D_FF), 1), _const_spec((D_MODEL, D_FF), 1),
                  _const_spec((FFN_CONV_W, D_FF), 1), _const_spec((1, D_FF), 1), _const_spec((D_FF, D_MODEL), 1),
                  _const_spec((1, D_MODEL), 1)],
        out_specs=[pl.BlockSpec((SR, D_MODEL), lambda i: (0, 0)),
                   pl.BlockSpec((FFN_CONV_W - 1, NB, D_FF), lambda i: (0, 0, 0))],
        out_shape=[jax.ShapeDtypeStruct((SR, D_MODEL), F32),
                   jax.ShapeDtypeStruct((FFN_CONV_W - 1, NB, D_FF), F32)],
        scratch_shapes=[slab, slab, slab],
        compiler_params=_params(("arbitrary",)),
        name="sample_ffn",
    )(x2s, ffn_prev_tm, gffn, wg, wu, wfc, bfc, wd, gfin)

    p_swa_k = pk.reshape(1, B, WINDOW, N_KV_HEADS, HEAD_DIM)
    p_swa_v = pv.reshape(1, B, WINDOW, N_KV_HEADS, HEAD_DIM)
    p_mem_k = mk.reshape(1, B, MEM_LEN, N_MEM_HEADS, MEM_HEAD_DIM)
    p_mem_v = mv.reshape(1, B, MEM_LEN, N_MEM_HEADS, MEM_HEAD_DIM)
    p_lru_conv = pconv8[None, :, 8 - (LRU_CONV_W - 1):, :]
    p_lru_h = ph8[None, :, 0, :]
    p_ffn_conv = pffn8[None, :, 8 - (FFN_CONV_W - 1):, :]
    y_sample = y_s.reshape(NB, 4, D_MODEL)
    s_swa_k = jnp.transpose(sk.reshape(NB, N_KV_HEADS, HEAD_DIM, WINDOW), (0, 3, 1, 2))[None]
    s_swa_v = jnp.transpose(sv.reshape(NB, N_KV_HEADS, HEAD_DIM, WINDOW), (0, 3, 1, 2))[None]
    s_lru_conv = xr_s.reshape(NB, 4, D_LRU)[None, :, 1:, :]
    s_lru_h = h_s.reshape(NB, 4, D_LRU)[None, :, 3, :]
    s_ffn_conv = jnp.transpose(gate_tm, (1, 0, 2))[None]
    return (y_prompt, y_sample, p_swa_k, p_swa_v, p_mem_k, p_mem_v, p_lru_conv, p_lru_h, p_ffn_conv,
            s_swa_k, s_swa_v, s_lru_conv, s_lru_h, s_ffn_conv)
```

```python
import functools

import jax
import jax.numpy as jnp
from jax import lax
from jax.experimental import pallas as pl
from jax.experimental.pallas import tpu as pltpu

D_MODEL = 1024
D_LRU = 512
LRU_BLOCK = 64
LRU_CONV_W = 4
LRU_C = 8.0
N_Q_HEADS = 8
N_KV_HEADS = 2
HEAD_DIM = 64
D_ATTN = 512
D_KV = 128
WINDOW = 128
D_IN = 1792
MEM_LEN = 256
N_MEM_HEADS = 4
MEM_HEAD_DIM = 128
D_MEM = 512
D_FF = 3072
FFN_CONV_W = 3
EPS = 1e-6
NEG_INF = -1e30

F32 = jnp.float32
BF16 = jnp.bfloat16

SLOPES = [float(2.0 ** (-8.0 * (i + 1) / N_Q_HEADS)) for i in range(N_Q_HEADS)]
ATTN_SCALE = HEAD_DIM ** -0.5
MEM_SCALE = MEM_HEAD_DIM ** -0.5
LOG2E = 1.4426950408889634
F32_TINY = 1.1754944e-38

TM = 1024
HR = 512
SEG = HR // 8
TF = 512
FF_CHUNK = 1024
S_NB = 16
S_ROWS = 4 * S_NB
C_NB = 4
C_ROWS = 4 * C_NB
VMEM_LIMIT = 56 * 1024 * 1024


def _dot(a, b):
    return jnp.dot(a, b, preferred_element_type=F32)


def _dot_nt(a, b):
    return lax.dot_general(a, b, (((1,), (1,)), ((), ())), preferred_element_type=F32)


def _rmsnorm(x, g):
    ms = jnp.mean(x * x, axis=-1, keepdims=True)
    return x * lax.rsqrt(ms + EPS) * g


def _gelu(x):
    c = 0.7978845608028654
    return x * (0.5 * (1.0 + jnp.tanh(c * (x + 0.044715 * (x * x * x)))))


def _sigmoid(x):
    return 1.0 / (1.0 + jnp.exp(-x))


def _softplus(x):
    return jnp.maximum(x, 0.0) + jnp.log1p(jnp.exp(-jnp.abs(x)))


def _lru_gates(xc, wax_ref, ba, bx, lam):
    xcb = xc.astype(BF16)
    pa, px = [], []
    for gi in range(2):
        pre = _dot(xcb[:, gi * 256:(gi + 1) * 256], wax_ref[gi])
        pa.append(pre[:, :256])
        px.append(pre[:, 256:])
    r = _sigmoid(jnp.concatenate(pa, axis=1) + ba)
    i = _sigmoid(jnp.concatenate(px, axis=1) + bx)
    log_a = (-LRU_C * _softplus(-lam)) * r
    a = jnp.exp(log_a)
    om = -jnp.tanh(log_a) * (a * a + 1.0)
    b = (om * lax.rsqrt(jnp.maximum(om, F32_TINY))) * (i * xc)
    return a, b


def _head_variants(t):
    lo = lax.broadcasted_iota(jnp.int32, t.shape, 1) < HEAD_DIM
    tr = pltpu.roll(t, HEAD_DIM, 1)
    z = jnp.zeros_like(t)
    x0 = jnp.where(lo, t, z)
    y0 = jnp.where(lo, z, tr)
    x1 = jnp.where(lo, tr, z)
    y1 = jnp.where(lo, z, t)
    return [v.astype(BF16) for v in (x0, y0, x1, y1)]


def _mem_kv_kernel(mem_ref, g_ref, wk_ref, wv_ref, mk_ref, mv_ref, mkb_ref, mvb_ref):
    n = _rmsnorm(mem_ref[0], g_ref[...]).astype(BF16)
    mk = _dot(n, wk_ref[...])
    mv = _dot(n, wv_ref[...])
    mk_ref[0] = mk
    mv_ref[0] = mv
    mkb_ref[0] = mk.astype(BF16)
    mvb_ref[0] = mv.astype(BF16)


def _prompt_mixer_kernel(sinks_ref, x_ref, g_ref, win_ref, wconv_ref, bconv_ref, wax_ref, ba_ref, bx_ref,
                         lam_ref, wout_ref,
                         x1_ref, pk_ref, pv_ref, pconv_ref, ph_ref,
                         proj_s, pin_s, pout_s, xtail_s, a_s, b_s, hl_s, h_s, hc_s, kv_s, ymix_s):
    t = pl.program_id(1)

    @pl.when(t == 0)
    def _():
        xtail_s[...] = jnp.zeros((24, D_LRU), F32)
        hc_s[...] = jnp.zeros((8, D_LRU), F32)
        kv_s[:, 0:WINDOW, :] = jnp.zeros((8, WINDOW, D_KV), BF16)

    def sec(i):
        return slice(i * HR, (i + 1) * HR)

    def in_proj(i):
        n1 = _rmsnorm(x_ref[0, sec(i), :], g_ref[...]).astype(BF16)
        proj_s[sec(i), :] = _dot(n1, win_ref[...])

    def seg_rows(j):
        return pl.ds(64 * (j % (SEG // 8)) + j // (SEG // 8), 8, stride=8)

    row8 = lax.broadcasted_iota(jnp.int32, (8, D_LRU), 0)

    def lru_gates(i):
        r0 = i * HR
        for j in range(HR // 8):
            for l in range(D_LRU // 128):
                pin_s[l, seg_rows(j), :] = proj_s[r0 + 8 * j:r0 + 8 * j + 8, l * 128:(l + 1) * 128]
        xs = jnp.concatenate([pin_s[l] for l in range(D_LRU // 128)], axis=1)
        tail = xs[HR - 24:, :]
        prev = xtail_s[...]
        xtail_s[...] = tail
        heads = [jnp.where(row8 == 0, pltpu.roll(prev[8 * q:8 * q + 8, :], 1, 0),
                           pltpu.roll(tail[8 * q:8 * q + 8, :], 1, 0)) for q in range(3)]
        sh1 = jnp.concatenate(heads[2:] + [xs[:HR - 8, :]], axis=0)
        sh2 = jnp.concatenate(heads[1:] + [xs[:HR - 16, :]], axis=0)
        sh3 = jnp.concatenate(heads + [xs[:HR - 24, :]], axis=0)
        wc = wconv_ref[...]
        xc = bconv_ref[...] + wc[0:1] * sh3 + wc[1:2] * sh2 + wc[2:3] * sh1 + wc[3:4] * xs
        a, b = _lru_gates(xc, wax_ref, ba_ref[...], bx_ref[...], lam_ref[...])
        a_s[sec(i), :] = a
        b_s[sec(i), :] = b

    def lru_scan(i, hcar):
        r0 = i * HR
        hl = b_s[r0:r0 + 8, :]
        ac = a_s[r0:r0 + 8, :]
        hl_s[0:8, :] = hl
        for g in range(1, SEG):
            av = a_s[r0 + 8 * g:r0 + 8 * g + 8, :]
            hl = av * hl + b_s[r0 + 8 * g:r0 + 8 * g + 8, :]
            ac = av * ac
            hl_s[8 * g:8 * g + 8, :] = hl
            a_s[r0 + 8 * g:r0 + 8 * g + 8, :] = ac
        hin = hcar
        for s in range(8):
            hend = hl + ac * hin
            if s < 7:
                hin = jnp.where(row8 == s + 1, pltpu.roll(hend, 1, 0), hin)
        hcar = jnp.broadcast_to(hend[7:8, :], (8, D_LRU))
        for g in range(SEG):
            hg = hl_s[8 * g:8 * g + 8, :] + a_s[r0 + 8 * g:r0 + 8 * g + 8, :] * hin
            for l in range(D_LRU // 128):
                pout_s[l, 8 * g:8 * g + 8, :] = hg[:, l * 128:(l + 1) * 128]
        for j in range(HR // 8):
            h_s[r0 + 8 * j:r0 + 8 * j + 8, :] = jnp.concatenate(
                [pout_s[l, seg_rows(j), :] for l in range(D_LRU // 128)], axis=1)
        gate = proj_s[sec(i), D_LRU:2 * D_LRU]
        ymix_s[sec(i), 0:D_LRU] = (h_s[sec(i), :] * _gelu(gate)).astype(BF16)
        return hcar

    def kv_prep(i):
        k = proj_s[sec(i), 1536:1664]
        v = proj_s[sec(i), 1664:1792]
        for n, arr in enumerate(_head_variants(k) + _head_variants(v)):
            kv_s[n, WINDOW + i * HR:WINDOW + (i + 1) * HR, :] = arr

    qi = lax.broadcasted_iota(jnp.int32, (WINDOW, 2 * WINDOW), 0)
    kj = lax.broadcasted_iota(jnp.int32, (WINDOW, 2 * WINDOW), 1)
    dist = qi + WINDOW - kj
    valid = (dist >= 0) & (dist < WINDOW)
    valid0 = valid & (kj >= jnp.where(t == 0, WINDOW, 0))
    distf = dist.astype(F32)
    bias = [jnp.where(valid, -SLOPES[h] * distf, NEG_INF) for h in range(N_Q_HEADS)]
    bias0 = [jnp.where(valid0, -SLOPES[h] * distf, NEG_INF) for h in range(N_Q_HEADS)]
    lane_lo = lax.broadcasted_iota(jnp.int32, (WINDOW, 2 * HEAD_DIM), 1) < HEAD_DIM

    r512 = lax.broadcasted_iota(jnp.int32, (4 * WINDOW, 2 * HEAD_DIM), 0)
    l512 = lax.broadcasted_iota(jnp.int32, (4 * WINDOW, 2 * HEAD_DIM), 1)
    ones_cols = jnp.where((r512 < 2 * WINDOW) == (l512 < HEAD_DIM), 1.0, 0.0).astype(BF16)
    n_iter = (HR // WINDOW) * 4

    def qk(n):
        j, c = divmod(n, 4)
        hk = c // 2
        rows = slice(j * WINDOW, (j + 1) * WINDOW)
        win = slice(j * WINDOW, (j + 2) * WINDOW)
        qc = (proj_s[rows, 1024 + c * 128:1024 + (c + 1) * 128] * ATTN_SCALE).astype(BF16)
        kcat = jnp.concatenate([kv_s[2 * hk, win, :], kv_s[2 * hk + 1, win, :]], axis=0)
        return _dot_nt(qc, kcat)

    def softmax_pv(n, s):
        j, c = divmod(n, 4)
        hk = c // 2
        rows = slice(j * WINDOW, (j + 1) * WINDOW)
        win = slice(j * WINDOW, (j + 2) * WINDOW)
        vcat = jnp.concatenate([kv_s[4 + 2 * hk, win, :], kv_s[5 + 2 * hk, win, :]], axis=0)
        vaug = jnp.concatenate([vcat, ones_cols], axis=1)
        ps, es = [], []
        for half in range(2):
            h = 2 * c + half
            sink = sinks_ref[h]
            sh = s[:, half * 256:(half + 1) * 256] + (bias0[h] if j == 0 else bias[h])
            m = jnp.maximum(jnp.max(sh, axis=-1, keepdims=True), sink)
            ps.append(jnp.exp(sh - m).astype(BF16))
            es.append(jnp.exp(sink - m))
        oa = _dot(jnp.concatenate(ps, axis=1), vaug)
        den = oa[:, 128:256] + jnp.where(lane_lo, es[0], es[1])
        ymix_s[rows, D_LRU + c * 128:D_LRU + (c + 1) * 128] = (oa[:, 0:128] * (1.0 / den)).astype(BF16)

    def attention(i):
        depth = 2
        base = i * n_iter
        pend = [qk(base + n) for n in range(depth)]
        for n in range(n_iter):
            s = pend.pop(0)
            if n + depth < n_iter:
                pend.append(qk(base + n + depth))
            softmax_pv(base + n, s)

    def out_proj(i):
        x1_ref[0, sec(i), :] = x_ref[0, sec(i), :] + _dot(ymix_s[sec(i), :], wout_ref[...])

    n_sec = TM // HR
    h = hc_s[...]
    in_proj(0)
    lru_gates(0)
    for i in range(n_sec):
        if i + 1 < n_sec:
            in_proj(i + 1)
        h = lru_scan(i, h)
        kv_prep(i)
        attention(i)
        if i + 1 < n_sec:
            lru_gates(i + 1)
        out_proj(i)

    hc_s[...] = h
    ph_ref[0] = h
    pconv_ref[0] = proj_s[TM - 8:, 0:D_LRU]
    pk_ref[0] = proj_s[TM - WINDOW:, 1536:1664]
    pv_ref[0] = proj_s[TM - WINDOW:, 1664:1792]
    kv_s[:, 0:WINDOW, :] = kv_s[:, TM:TM + WINDOW, :]


def _ffn_chunks(n3, acc, gprev_fn, wg_ref, wu_ref, wfc_ref, bfc_ref, wd_ref, on_gate):
    nc = D_FF // FF_CHUNK

    def up(c):
        cs = slice(c * FF_CHUNK, (c + 1) * FF_CHUNK)
        return _dot(n3, wg_ref[:, cs]), _dot(n3, wu_ref[:, cs])

    nxt = up(0)
    for c in range(nc):
        cs = slice(c * FF_CHUNK, (c + 1) * FF_CHUNK)
        g, u = nxt
        if c + 1 < nc:
            nxt = up(c + 1)
        g2, g1 = gprev_fn(c, g)
        on_gate(c, g)
        wfc = wfc_ref[:, cs]
        conv = bfc_ref[:, cs] + wfc[0:1] * g2 + wfc[1:2] * g1 + wfc[2:3] * g
        hmid = (_gelu(conv) * u).astype(BF16)
        acc = acc + _dot(hmid, wd_ref[cs, :])
    return acc


def _prompt_ffn_kernel(x1_ref, mk_ref, mv_ref, xs1_ref, cmk_ref, cmv_ref, gc_ref, wq_ref, wo_ref, gf_ref,
                       wg_ref, wu_ref, wfc_ref, bfc_ref, wd_ref, gfin_ref,
                       y_ref, pffn_ref, xs2_ref,
                       oc_s, gbuf_s, gcar_s, sq_s, soc_s):
    t = pl.program_id(1)

    @pl.when(t == 0)
    def _():
        gcar_s[...] = jnp.zeros((8, D_FF), F32)

    _sample_cross(xs1_ref, cmk_ref, cmv_ref, gc_ref, wq_ref, wo_ref, xs2_ref, sq_s, soc_s)

    x1 = x1_ref[0]
    qc = _dot(_rmsnorm(x1, gc_ref[...]).astype(BF16), wq_ref[...]).astype(BF16)
    hsl = [slice(h * MEM_HEAD_DIM, (h + 1) * MEM_HEAD_DIM) for h in range(N_MEM_HEADS)]
    ss = [_dot_nt(qc[:, hs], mk_ref[0, :, hs]) for hs in hsl]
    for h, hs in enumerate(hsl):
        s = ss[h]
        m = jnp.max(s, axis=-1, keepdims=True)
        p = jnp.exp2((s - m) * (MEM_SCALE * LOG2E))
        l = jnp.sum(p, axis=-1, keepdims=True)
        o = _dot(p.astype(BF16), mv_ref[0, :, hs]) * (1.0 / l)
        oc_s[:, hs] = o.astype(BF16)
    x2 = x1 + _dot(oc_s[...], wo_ref[...])
    n3 = _rmsnorm(x2, gf_ref[...]).astype(BF16)

    def gprev(c, g):
        cs = slice(c * FF_CHUNK, (c + 1) * FF_CHUNK)
        gbuf_s[0:8, :] = gcar_s[:, cs]
        gbuf_s[8:TF + 8, :] = g
        return gbuf_s[6:6 + TF, :], gbuf_s[7:7 + TF, :]

    def on_gate(c, g):
        cs = slice(c * FF_CHUNK, (c + 1) * FF_CHUNK)
        tail = g[TF - 8:, :]
        gcar_s[:, cs] = tail
        pffn_ref[0, :, cs] = tail

    acc = _ffn_chunks(n3, x2, gprev, wg_ref, wu_ref, wfc_ref, bfc_ref, wd_ref, on_gate)
    y_ref[0] = _rmsnorm(acc, gfin_ref[...])


def _sample_mixer_kernel(sinks_ref, x_ref, prev4_ref, h0_ref, ck_ref, cv_ref, g_ref, win_ref, wconv_ref, bconv_ref,
                         wax_ref, ba_ref, bx_ref, lam_ref, wout_ref, wkt_ref, wvt_ref,
                         x1_ref, sk_ref, sv_ref, xr_ref, h_ref,
                         q_s, yatt_s):
    R = S_ROWS
    x = x_ref[...]
    n1 = _rmsnorm(x, g_ref[...]).astype(BF16)
    proj = _dot(n1, win_ref[...])
    xr = proj[:, 0:D_LRU]
    gate = proj[:, D_LRU:2 * D_LRU]
    q_s[...] = proj[:, 1024:1536] * ATTN_SCALE
    xr_ref[...] = xr

    tmod = lax.broadcasted_iota(jnp.int32, (R, D_LRU), 0) & 3
    prev4 = prev4_ref[...]
    xs1 = jnp.where(tmod >= 1, pltpu.roll(xr, 1, 0), pltpu.roll(prev4, R - 2, 0))
    xs2 = jnp.where(tmod >= 2, pltpu.roll(xr, 2, 0), pltpu.roll(prev4, R - 1, 0))
    xs3 = jnp.where(tmod >= 3, pltpu.roll(xr, 3, 0), prev4)
    wc = wconv_ref[...]
    xc = bconv_ref[...] + wc[0:1] * xs3 + wc[1:2] * xs2 + wc[2:3] * xs1 + wc[3:4] * xr

    a, b = _lru_gates(xc, wax_ref, ba_ref[...], bx_ref[...], lam_ref[...])
    for s in (1, 2):
        m = tmod >= s
        a_sh = pltpu.roll(a, s, 0)
        b_sh = pltpu.roll(b, s, 0)
        b = jnp.where(m, a * b_sh + b, b)
        a = jnp.where(m, a * a_sh, a)
    h = a * h0_ref[...] + b
    h_ref[...] = h
    y_lru = h * _gelu(gate)

    kt_new = _dot_nt(wkt_ref[...], n1)
    vt_new = _dot_nt(wvt_ref[...], n1)
    zpad = jnp.zeros((D_KV, WINDOW - R), F32)
    kt_pad = jnp.concatenate([kt_new, zpad], axis=1)
    vt_pad = jnp.concatenate([vt_new, zpad], axis=1)
    kt_pad_b = kt_pad.astype(BF16)
    vt_pad_b = vt_pad.astype(BF16)

    r64 = lax.broadcasted_iota(jnp.int32, (8 * N_Q_HEADS, WINDOW), 0)
    c64 = lax.broadcasted_iota(jnp.int32, (8 * N_Q_HEADS, WINDOW), 1)
    t64 = r64 & 3
    slope = jnp.zeros((8 * N_Q_HEADS, WINDOW), F32)
    sinkcol = jnp.zeros((8 * N_Q_HEADS, 1), F32)
    for hq in range(N_Q_HEADS):
        slope = jnp.where((r64 >> 3) == hq, SLOPES[hq], slope)
        sinkcol = jnp.where((r64[:, 0:1] >> 3) == hq, sinks_ref[hq], sinkcol)
    bias_c = jnp.where(c64 > t64, -slope * (t64 + WINDOW - c64).astype(F32), NEG_INF)
    bat0 = (r64 & 7) < 4
    lane_lo = lax.broadcasted_iota(jnp.int32, (8, D_KV), 1) < HEAD_DIM
    lane128 = lax.broadcasted_iota(jnp.int32, (D_KV, WINDOW), 1)

    def pair_scores(p):
        r0 = p * 8
        q8 = q_s[r0:r0 + 8, :]
        blocks = []
        for hq in range(N_Q_HEADS):
            chunk = q8[:, (hq // 2) * 128:(hq // 2 + 1) * 128]
            hk = hq // (N_Q_HEADS // N_KV_HEADS)
            src = chunk if (hq % 2) == hk else pltpu.roll(chunk, HEAD_DIM, 1)
            blocks.append(jnp.where(lane_lo == (hk == 0), src, 0.0))
        lhs = jnp.concatenate(blocks, axis=0).astype(BF16)
        sb = []
        for bb in range(2):
            bidx = 2 * p + bb
            kt = ck_ref[bidx]
            vt = cv_ref[bidx]
            sb.append(_dot(lhs, kt.astype(BF16)))
            shift = (WINDOW - 4 - 4 * bidx) % WINDOW
            sk_ref[bidx] = jnp.where(lane128 >= WINDOW - 4, pltpu.roll(kt_pad, shift, 1),
                                     pltpu.roll(kt, WINDOW - 4, 1))
            sv_ref[bidx] = jnp.where(lane128 >= WINDOW - 4, pltpu.roll(vt_pad, shift, 1),
                                     pltpu.roll(vt, WINDOW - 4, 1))
        s_c = jnp.where(bat0, sb[0], sb[1]) + bias_c
        same = (c64 >> 2) == (2 * p + ((r64 & 7) >> 2))
        dn = t64 - (c64 & 3)
        bias_n = jnp.where(same & (dn >= 0), -slope * dn.astype(F32), NEG_INF)
        s_n = _dot(lhs, kt_pad_b) + bias_n
        return s_c, s_n

    def pair_finish(p, s_c, s_n):
        r0 = p * 8
        m = jnp.maximum(jnp.maximum(jnp.max(s_c, axis=-1, keepdims=True), jnp.max(s_n, axis=-1, keepdims=True)),
                        sinkcol)
        pc = jnp.exp(s_c - m)
        pn = jnp.exp(s_n - m)
        l = jnp.sum(pc, axis=-1, keepdims=True) + jnp.sum(pn, axis=-1, keepdims=True) + jnp.exp(sinkcol - m)
        pcb = pc.astype(BF16)
        ob = [_dot_nt(pcb, cv_ref[2 * p + bb].astype(BF16)) for bb in range(2)]
        o = (jnp.where(bat0, ob[0], ob[1]) + _dot_nt(pn.astype(BF16), vt_pad_b)) * (1.0 / l)
        for c in range(N_Q_HEADS // 2):
            hk = c // 2
            ev = o[16 * c:16 * c + 8, :]
            od = o[16 * c + 8:16 * c + 16, :]
            if hk == 0:
                od = pltpu.roll(od, HEAD_DIM, 1)
            else:
                ev = pltpu.roll(ev, HEAD_DIM, 1)
            yatt_s[r0:r0 + 8, c * 128:(c + 1) * 128] = jnp.where(lane_lo, ev, od)

    n_pairs = S_NB // 2
    cur = pair_scores(0)
    for p in range(n_pairs):
        nxt = pair_scores(p + 1) if p + 1 < n_pairs else None
        pair_finish(p, *cur)
        cur = nxt

    ymix = jnp.concatenate([y_lru, yatt_s[...]], axis=1).astype(BF16)
    x1_ref[...] = x + _dot(ymix, wout_ref[...])


def _load_mem_heads(ref, b):
    return jnp.concatenate([ref[b, pl.ds(h, MEM_LEN, stride=N_MEM_HEADS), :] for h in range(N_MEM_HEADS)],
                           axis=1).astype(BF16)


def _sample_cross(x1_ref, mk_ref, mv_ref, gc_ref, wq_ref, wo_ref, x2_ref, q_s, oc_s):
    lane_head = lax.broadcasted_iota(jnp.int32, (8, D_MEM), 1) // MEM_HEAD_DIM
    bat0 = (lax.broadcasted_iota(jnp.int32, (32, D_MEM), 0) & 7) < 4
    bat0_s = (lax.broadcasted_iota(jnp.int32, (32, MEM_LEN), 0) & 7) < 4

    def q_proj():
        q_s[...] = _dot(_rmsnorm(x1_ref[...], gc_ref[...]).astype(BF16), wq_ref[...])

    def scores(p):
        q8 = q_s[p * 8:(p + 1) * 8, :]
        lhs = jnp.concatenate([jnp.where(lane_head == h, q8, 0.0) for h in range(N_MEM_HEADS)],
                              axis=0).astype(BF16)
        sb = [_dot_nt(lhs, _load_mem_heads(mk_ref, 2 * p + bb)) for bb in range(2)]
        return jnp.where(bat0_s, sb[0], sb[1]) * MEM_SCALE

    def attend():
        ss = [scores(p) for p in range(C_NB // 2)]
        for p, s in enumerate(ss):
            finish(p, s)

    def finish(p, s):
        m = jnp.max(s, axis=-1, keepdims=True)
        pe = jnp.exp(s - m)
        l = jnp.sum(pe, axis=-1, keepdims=True)
        pb = pe.astype(BF16)
        ob = [_dot(pb, _load_mem_heads(mv_ref, 2 * p + bb)) for bb in range(2)]
        o = jnp.where(bat0, ob[0], ob[1]) * (1.0 / l)
        out = jnp.where(lane_head == 0, o[0:8], 0.0)
        for h in range(1, N_MEM_HEADS):
            out = out + jnp.where(lane_head == h, o[h * 8:(h + 1) * 8], 0.0)
        oc_s[p * 8:(p + 1) * 8, :] = out

    def out_proj():
        x2_ref[...] = x1_ref[...] + _dot(oc_s[...].astype(BF16), wo_ref[...])

    q_proj()
    attend()
    out_proj()


def _sample_ffn_kernel(x2_ref, prev_ref, gf_ref, wg_ref, wu_ref, wfc_ref, bfc_ref, wd_ref, gfin_ref,
                       y_ref, gate_ref, slab_s, p1_s, p2_s):
    R = x2_ref.shape[0]
    nb = R // 4
    nl = FF_CHUNK // 128
    x2 = x2_ref[...]
    n3 = _rmsnorm(x2, gf_ref[...]).astype(BF16)
    tmod = lax.broadcasted_iota(jnp.int32, (R, FF_CHUNK), 0) & 3
    p1_s[...] = jnp.zeros(p1_s.shape, F32)
    p2_s[...] = jnp.zeros(p2_s.shape, F32)

    def gprev(c, g):
        for l in range(nl):
            cols = slice(c * FF_CHUNK + l * 128, c * FF_CHUNK + (l + 1) * 128)
            s0 = prev_ref[:, 0, cols]
            s1 = prev_ref[:, 1, cols]
            p1_s[l, pl.ds(0, nb, stride=4), :] = s1
            p2_s[l, pl.ds(0, nb, stride=4), :] = s0
            p2_s[l, pl.ds(1, nb, stride=4), :] = s1
        p1 = jnp.concatenate([p1_s[l] for l in range(nl)], axis=1)
        p2 = jnp.concatenate([p2_s[l] for l in range(nl)], axis=1)
        g1 = jnp.where(tmod >= 1, pltpu.roll(g, 1, 0), p1)
        g2 = jnp.where(tmod >= 2, pltpu.roll(g, 2, 0), p2)
        return g2, g1

    def on_gate(c, g):
        for l in range(nl):
            slab_s[l] = g[:, l * 128:(l + 1) * 128]
        for tt in range(2):
            gate_ref[:, tt, c * FF_CHUNK:(c + 1) * FF_CHUNK] = jnp.concatenate(
                [slab_s[l, pl.ds(2 + tt, nb, stride=4), :] for l in range(nl)], axis=1)

    acc = _ffn_chunks(n3, x2, gprev, wg_ref, wu_ref, wfc_ref, bfc_ref, wd_ref, on_gate)
    y_ref[...] = _rmsnorm(acc, gfin_ref[...])


def _const_spec(shape, grid_rank):
    zeros = (0,) * len(shape)
    if grid_rank == 1:
        return pl.BlockSpec(shape, lambda i: zeros, pipeline_mode=pl.Buffered(1))
    return pl.BlockSpec(shape, lambda i, j: zeros, pipeline_mode=pl.Buffered(1))


def _block_diag4(w):
    eye = jnp.eye(4, dtype=w.dtype)
    return (w[:, :, None, :] * eye[:, None, :, None]).reshape(4 * LRU_BLOCK, 4 * LRU_BLOCK)


def _params(sem):
    return pltpu.CompilerParams(dimension_semantics=sem, vmem_limit_bytes=VMEM_LIMIT)


def kernel(x_prompt, x_sample, cache_swa_k, cache_swa_v, cache_mem_k, cache_mem_v, state_lru_conv, state_lru_h, state_ffn_conv, mem_prompt, g_mix, w_in, w_lru_conv, b_lru_conv, w_lru_a, b_lru_a, w_lru_x, b_lru_x, lru_lambda, attn_sinks, w_out, g_cross, g_mem, w_mem_q, w_mem_k, w_mem_v, w_mem_o, g_ffn, w_ffn_gate, w_ffn_up, w_ffn_conv, b_ffn_conv, w_ffn_down, g_final):
    B, T, _ = x_prompt.shape
    NB = x_sample.shape[0]
    NT = T // TM
    SR = NB * 4

    win = w_in[0].astype(BF16)
    wout = w_out[0].astype(BF16)
    wq = w_mem_q[0].astype(BF16)
    wk = w_mem_k[0].astype(BF16)
    wv = w_mem_v[0].astype(BF16)
    wo = w_mem_o[0].astype(BF16)
    wg = w_ffn_gate[0].astype(BF16)
    wu = w_ffn_up[0].astype(BF16)
    wd = w_ffn_down[0].astype(BF16)
    wax = jnp.stack([
        jnp.concatenate([_block_diag4(w_lru_a[0, 4 * gi:4 * gi + 4]), _block_diag4(w_lru_x[0, 4 * gi:4 * gi + 4])],
                        axis=1) for gi in range(2)]).astype(BF16)
    gmix, gcross, gmem, gffn = g_mix, g_cross, g_mem, g_ffn
    gfin = g_final.reshape(1, D_MODEL)
    wconv, bconv = w_lru_conv[0], b_lru_conv
    ba, bx, lam = b_lru_a, b_lru_x, lru_lambda
    wfc, bfc = w_ffn_conv[0], b_ffn_conv
    sinks = attn_sinks[0]
    smem = pl.BlockSpec(memory_space=pltpu.SMEM)

    mk, mv, mkb, mvb = pl.pallas_call(
        _mem_kv_kernel,
        grid=(B,),
        in_specs=[pl.BlockSpec((1, MEM_LEN, D_MODEL), lambda b: (b, 0, 0)),
                  _const_spec((1, D_MODEL), 1), _const_spec((D_MODEL, D_MEM), 1), _const_spec((D_MODEL, D_MEM), 1)],
        out_specs=[pl.BlockSpec((1, MEM_LEN, D_MEM), lambda b: (b, 0, 0))] * 4,
        out_shape=[jax.ShapeDtypeStruct((B, MEM_LEN, D_MEM), F32)] * 2
        + [jax.ShapeDtypeStruct((B, MEM_LEN, D_MEM), BF16)] * 2,
        compiler_params=_params(("arbitrary",)),
        name="mem_kv",
    )(mem_prompt, gmem, wk, wv)

    x1p, pk, pv, pconv8, ph8 = pl.pallas_call(
        _prompt_mixer_kernel,
        grid=(B, NT),
        in_specs=[smem,
                  pl.BlockSpec((1, TM, D_MODEL), lambda b, t: (b, t, 0)),
                  _const_spec((1, D_MODEL), 2), _const_spec((D_MODEL, D_IN), 2),
                  _const_spec((LRU_CONV_W, D_LRU), 2), _const_spec((1, D_LRU), 2),
                  _const_spec((2, 256, 512), 2), _const_spec((1, D_LRU), 2), _const_spec((1, D_LRU), 2),
                  _const_spec((1, D_LRU), 2), _const_spec((D_MODEL, D_MODEL), 2)],
        out_specs=[pl.BlockSpec((1, TM, D_MODEL), lambda b, t: (b, t, 0)),
                   pl.BlockSpec((1, WINDOW, D_KV), lambda b, t: (b, 0, 0)),
                   pl.BlockSpec((1, WINDOW, D_KV), lambda b, t: (b, 0, 0)),
                   pl.BlockSpec((1, 8, D_LRU), lambda b, t: (b, 0, 0)),
                   pl.BlockSpec((1, 8, D_LRU), lambda b, t: (b, 0, 0))],
        out_shape=[jax.ShapeDtypeStruct((B, T, D_MODEL), F32),
                   jax.ShapeDtypeStruct((B, WINDOW, D_KV), F32),
                   jax.ShapeDtypeStruct((B, WINDOW, D_KV), F32),
                   jax.ShapeDtypeStruct((B, 8, D_LRU), F32),
                   jax.ShapeDtypeStruct((B, 8, D_LRU), F32)],
        scratch_shapes=[pltpu.VMEM((TM, D_IN), F32),
                        pltpu.VMEM((D_LRU // 128, HR, 128), F32),
                        pltpu.VMEM((D_LRU // 128, HR, 128), F32),
                        pltpu.VMEM((24, D_LRU), F32),
                        pltpu.VMEM((TM, D_LRU), F32),
                        pltpu.VMEM((TM, D_LRU), F32),
                        pltpu.VMEM((HR, D_LRU), F32),
                        pltpu.VMEM((TM, D_LRU), F32),
                        pltpu.VMEM((8, D_LRU), F32),
                        pltpu.VMEM((8, TM + WINDOW, D_KV), BF16),
                        pltpu.VMEM((TM, D_MODEL), BF16)],
        compiler_params=_params(("arbitrary", "arbitrary")),
        name="prompt_mixer",
    )(sinks, x_prompt, gmix, win, wconv, bconv, wax, ba, bx, lam, wout)

    xs = x_sample.reshape(SR, D_MODEL)
    conv_prev4 = jnp.pad(state_lru_conv[0], ((0, 0), (0, 1), (0, 0))).reshape(SR, D_LRU)
    h0rep = jnp.repeat(state_lru_h[0], 4, axis=0)
    ck = jnp.transpose(cache_swa_k[0], (0, 2, 3, 1)).reshape(NB, D_KV, WINDOW)
    cv = jnp.transpose(cache_swa_v[0], (0, 2, 3, 1)).reshape(NB, D_KV, WINDOW)
    wkt = jnp.transpose(w_in[0][:, 1536:1664]).astype(BF16)
    wvt = jnp.transpose(w_in[0][:, 1664:1792]).astype(BF16)
    row_spec = lambda w: pl.BlockSpec((S_ROWS, w), lambda i: (i, 0))
    cache_spec = pl.BlockSpec((S_NB, D_KV, WINDOW), lambda i: (i, 0, 0))
    x1s, sk, sv, xr_s, h_s = pl.pallas_call(
        _sample_mixer_kernel,
        grid=(NB // S_NB,),
        in_specs=[smem, row_spec(D_MODEL), row_spec(D_LRU), row_spec(D_LRU), cache_spec, cache_spec,
                  _const_spec((1, D_MODEL), 1), _const_spec((D_MODEL, D_IN), 1),
                  _const_spec((LRU_CONV_W, D_LRU), 1), _const_spec((1, D_LRU), 1),
                  _const_spec((2, 256, 512), 1), _const_spec((1, D_LRU), 1), _const_spec((1, D_LRU), 1),
                  _const_spec((1, D_LRU), 1), _const_spec((D_MODEL, D_MODEL), 1),
                  _const_spec((D_KV, D_MODEL), 1), _const_spec((D_KV, D_MODEL), 1)],
        out_specs=[row_spec(D_MODEL), cache_spec, cache_spec, row_spec(D_LRU), row_spec(D_LRU)],
        out_shape=[jax.ShapeDtypeStruct((SR, D_MODEL), F32),
                   jax.ShapeDtypeStruct((NB, D_KV, WINDOW), F32),
                   jax.ShapeDtypeStruct((NB, D_KV, WINDOW), F32),
                   jax.ShapeDtypeStruct((SR, D_LRU), F32),
                   jax.ShapeDtypeStruct((SR, D_LRU), F32)],
        scratch_shapes=[pltpu.VMEM((S_ROWS, D_ATTN), F32),
                        pltpu.VMEM((S_ROWS, D_ATTN), F32)],
        compiler_params=_params(("arbitrary",)),
        name="sample_mixer",
    )(sinks, xs, conv_prev4, h0rep, ck, cv, gmix, win, wconv, bconv, wax, ba, bx, lam, wout, wkt, wvt)

    cmk = cache_mem_k.reshape(NB, MEM_LEN * N_MEM_HEADS, MEM_HEAD_DIM)
    cmv = cache_mem_v.reshape(NB, MEM_LEN * N_MEM_HEADS, MEM_HEAD_DIM)
    ntf = T // TF
    assert NB == C_NB * B * ntf
    crow = pl.BlockSpec((C_ROWS, D_MODEL), lambda b, t: (b * ntf + t, 0))
    cmem = pl.BlockSpec((C_NB, MEM_LEN * N_MEM_HEADS, MEM_HEAD_DIM), lambda b, t: (b * ntf + t, 0, 0))
    y_prompt, pffn8, x2s = pl.pallas_call(
        _prompt_ffn_kernel,
        grid=(B, ntf),
        in_specs=[pl.BlockSpec((1, TF, D_MODEL), lambda b, t: (b, t, 0)),
                  pl.BlockSpec((1, MEM_LEN, D_MEM), lambda b, t: (b, 0, 0)),
                  pl.BlockSpec((1, MEM_LEN, D_MEM), lambda b, t: (b, 0, 0)),
                  crow, cmem, cmem,
                  _const_spec((1, D_MODEL), 2), _const_spec((D_MODEL, D_MEM), 2), _const_spec((D_MEM, D_MODEL), 2),
                  _const_spec((1, D_MODEL), 2), _const_spec((D_MODEL, D_FF), 2), _const_spec((D_MODEL, D_FF), 2),
                  _const_spec((FFN_CONV_W, D_FF), 2), _const_spec((1, D_FF), 2), _const_spec((D_FF, D_MODEL), 2),
                  _const_spec((1, D_MODEL), 2)],
        out_specs=[pl.BlockSpec((1, TF, D_MODEL), lambda b, t: (b, t, 0)),
                   pl.BlockSpec((1, 8, D_FF), lambda b, t: (b, 0, 0)),
                   crow],
        out_shape=[jax.ShapeDtypeStruct((B, T, D_MODEL), F32),
                   jax.ShapeDtypeStruct((B, 8, D_FF), F32),
                   jax.ShapeDtypeStruct((SR, D_MODEL), F32)],
        scratch_shapes=[pltpu.VMEM((TF, D_MEM), BF16),
                        pltpu.VMEM((TF + 8, FF_CHUNK), F32),
                        pltpu.VMEM((8, D_FF), F32),
                        pltpu.VMEM((C_ROWS, D_MEM), F32),
                        pltpu.VMEM((C_ROWS, D_MEM), F32)],
        compiler_params=_params(("arbitrary", "arbitrary")),
        name="prompt_ffn",
    )(x1p, mkb, mvb, x1s, cmk, cmv, gcross, wq, wo, gffn, wg, wu, wfc, bfc, wd, gfin)

    ffn_prev = state_ffn_conv[0]
    slab = pltpu.VMEM((FF_CHUNK // 128, SR, 128), F32)
    y_s, gate_new = pl.pallas_call(
        _sample_ffn_kernel,
        grid=(1,),
        in_specs=[_const_spec((SR, D_MODEL), 1), _const_spec((NB, FFN_CONV_W - 1, D_FF), 1),
                  _const_spec((1, D_MODEL), 1), _const_spec((D_MODEL, D_FF), 1), _const_spec((D_MODEL, D_FF), 1),
                  _const_spec((FFN_CONV_W, D_FF), 1), _const_spec((1, D_FF), 1), _const_spec((D_FF, D_MODEL), 1),
                  _const_spec((1, D_MODEL), 1)],
        out_specs=[pl.BlockSpec((SR, D_MODEL), lambda i: (0, 0)),
                   pl.BlockSpec((NB, FFN_CONV_W - 1, D_FF), lambda i: (0, 0, 0))],
        out_shape=[jax.ShapeDtypeStruct((SR, D_MODEL), F32),
                   jax.ShapeDtypeStruct((NB, FFN_CONV_W - 1, D_FF), F32)],
        scratch_shapes=[slab, slab, slab],
        compiler_params=_params(("arbitrary",)),
        name="sample_ffn",
    )(x2s, ffn_prev, gffn, wg, wu, wfc, bfc, wd, gfin)

    p_swa_k = pk.reshape(1, B, WINDOW, N_KV_HEADS, HEAD_DIM)
    p_swa_v = pv.reshape(1, B, WINDOW, N_KV_HEADS, HEAD_DIM)
    p_mem_k = mk.reshape(1, B, MEM_LEN, N_MEM_HEADS, MEM_HEAD_DIM)
    p_mem_v = mv.reshape(1, B, MEM_LEN, N_MEM_HEADS, MEM_HEAD_DIM)
    p_lru_conv = pconv8[None, :, 8 - (LRU_CONV_W - 1):, :]
    p_lru_h = ph8[None, :, 0, :]
    p_ffn_conv = pffn8[None, :, 8 - (FFN_CONV_W - 1):, :]
    y_sample = y_s.reshape(NB, 4, D_MODEL)
    s_swa_k = jnp.transpose(sk.reshape(NB, N_KV_HEADS, HEAD_DIM, WINDOW), (0, 3, 1, 2))[None]
    s_swa_v = jnp.transpose(sv.reshape(NB, N_KV_HEADS, HEAD_DIM, WINDOW), (0, 3, 1, 2))[None]
    s_lru_conv = xr_s.reshape(NB, 4, D_LRU)[None, :, 1:, :]
    s_lru_h = h_s.reshape(NB, 4, D_LRU)[None, :, 3, :]
    s_ffn_conv = gate_new[None]
    return (y_prompt, y_sample, p_swa_k, p_swa_v, p_mem_k, p_mem_v, p_lru_conv, p_lru_h, p_ffn_conv,
            s_swa_k, s_swa_v, s_lru_conv, s_lru_h, s_ffn_conv)
```

```python
import functools

import jax
import jax.numpy as jnp
from jax import lax
from jax.experimental import pallas as pl
from jax.experimental.pallas import tpu as pltpu

D_MODEL = 1024
D_LRU = 512
LRU_BLOCK = 64
LRU_CONV_W = 4
LRU_C = 8.0
N_Q_HEADS = 8
N_KV_HEADS = 2
HEAD_DIM = 64
D_ATTN = 512
D_KV = 128
WINDOW = 128
D_IN = 1792
MEM_LEN = 256
N_MEM_HEADS = 4
MEM_HEAD_DIM = 128
D_MEM = 512
D_FF = 3072
FFN_CONV_W = 3
EPS = 1e-6
NEG_INF = -1e30

F32 = jnp.float32
BF16 = jnp.bfloat16

SLOPES = [float(2.0 ** (-8.0 * (i + 1) / N_Q_HEADS)) for i in range(N_Q_HEADS)]
ATTN_SCALE = HEAD_DIM ** -0.5
MEM_SCALE = MEM_HEAD_DIM ** -0.5
LOG2E = 1.4426950408889634
F32_TINY = 1.1754944e-38

TM = 1024
HR = 512
SEG = HR // 8
TF = 512
FF_CHUNK = 1024
S_NB = 16
S_ROWS = 4 * S_NB
C_NB = 4
C_ROWS = 4 * C_NB
VMEM_LIMIT = 56 * 1024 * 1024


def _dot(a, b):
    return jnp.dot(a, b, preferred_element_type=F32)


def _dot_nt(a, b):
    return lax.dot_general(a, b, (((1,), (1,)), ((), ())), preferred_element_type=F32)


def _rstd(x):
    return lax.rsqrt(jnp.mean(x * x, axis=-1, keepdims=True) + EPS)


def _rmsnorm(x, g):
    return x * _rstd(x) * g


def _gelu(x):
    c = 0.7978845608028654
    return x * (0.5 * (1.0 + jnp.tanh(c * (x + 0.044715 * (x * x * x)))))


def _sigmoid(x):
    return 1.0 / (1.0 + jnp.exp(-x))


def _softplus(x):
    return jnp.maximum(x, 0.0) + jnp.log1p(jnp.exp(-jnp.abs(x)))


def _lru_gates(xc, wax_ref, ba, bx, lam):
    xcb = xc.astype(BF16)
    pa, px = [], []
    for gi in range(2):
        pre = _dot(xcb[:, gi * 256:(gi + 1) * 256], wax_ref[gi])
        pa.append(pre[:, :256])
        px.append(pre[:, 256:])
    r = _sigmoid(jnp.concatenate(pa, axis=1) + ba)
    i = _sigmoid(jnp.concatenate(px, axis=1) + bx)
    log_a = (-LRU_C * _softplus(-lam)) * r
    a = jnp.exp(log_a)
    om = -jnp.tanh(log_a) * (a * a + 1.0)
    b = (om * lax.rsqrt(jnp.maximum(om, F32_TINY))) * (i * xc)
    return a, b


def _head_variants(t):
    lo = lax.broadcasted_iota(jnp.int32, t.shape, 1) < HEAD_DIM
    tr = pltpu.roll(t, HEAD_DIM, 1)
    z = jnp.zeros_like(t)
    x0 = jnp.where(lo, t, z)
    y0 = jnp.where(lo, z, tr)
    x1 = jnp.where(lo, tr, z)
    y1 = jnp.where(lo, z, t)
    return [v.astype(BF16) for v in (x0, y0, x1, y1)]


def _mem_kv_kernel(mem_ref, g_ref, wk_ref, wv_ref, mk_ref, mv_ref, mkb_ref, mvb_ref):
    n = _rmsnorm(mem_ref[0], g_ref[...]).astype(BF16)
    mk = _dot(n, wk_ref[...])
    mv = _dot(n, wv_ref[...])
    mk_ref[0] = mk
    mv_ref[0] = mv
    mkb_ref[0] = mk.astype(BF16)
    mvb_ref[0] = mv.astype(BF16)


def _prompt_mixer_kernel(sinks_ref, x_ref, win_ref, wconv_ref, bconv_ref, wax_ref, ba_ref, bx_ref,
                         lam_ref, wout_ref,
                         x1_ref, pk_ref, pv_ref, pconv_ref, ph_ref,
                         proj_s, pin_s, pout_s, xtail_s, a_s, b_s, hl_s, h_s, hc_s, kv_s, ymix_s):
    t = pl.program_id(1)

    @pl.when(t == 0)
    def _():
        xtail_s[...] = jnp.zeros((24, D_LRU), F32)
        hc_s[...] = jnp.zeros((8, D_LRU), F32)
        kv_s[:, 0:WINDOW, :] = jnp.zeros((8, WINDOW, D_KV), BF16)

    def sec(i):
        return slice(i * HR, (i + 1) * HR)

    def in_proj(i):
        x = x_ref[0, sec(i), :]
        proj_s[sec(i), :] = _dot(x.astype(BF16), win_ref[...]) * _rstd(x)

    def seg_rows(j):
        return pl.ds(64 * (j % (SEG // 8)) + j // (SEG // 8), 8, stride=8)

    row8 = lax.broadcasted_iota(jnp.int32, (8, D_LRU), 0)

    def lru_gates(i):
        r0 = i * HR
        for j in range(HR // 8):
            for l in range(D_LRU // 128):
                pin_s[l, seg_rows(j), :] = proj_s[r0 + 8 * j:r0 + 8 * j + 8, l * 128:(l + 1) * 128]
        xs = jnp.concatenate([pin_s[l] for l in range(D_LRU // 128)], axis=1)
        tail = xs[HR - 24:, :]
        prev = xtail_s[...]
        xtail_s[...] = tail
        heads = [jnp.where(row8 == 0, pltpu.roll(prev[8 * q:8 * q + 8, :], 1, 0),
                           pltpu.roll(tail[8 * q:8 * q + 8, :], 1, 0)) for q in range(3)]
        sh1 = jnp.concatenate(heads[2:] + [xs[:HR - 8, :]], axis=0)
        sh2 = jnp.concatenate(heads[1:] + [xs[:HR - 16, :]], axis=0)
        sh3 = jnp.concatenate(heads + [xs[:HR - 24, :]], axis=0)
        wc = wconv_ref[...]
        xc = bconv_ref[...] + wc[0:1] * sh3 + wc[1:2] * sh2 + wc[2:3] * sh1 + wc[3:4] * xs
        a, b = _lru_gates(xc, wax_ref, ba_ref[...], bx_ref[...], lam_ref[...])
        a_s[sec(i), :] = a
        b_s[sec(i), :] = b

    def lru_scan(i, hcar):
        r0 = i * HR
        hl = b_s[r0:r0 + 8, :]
        ac = a_s[r0:r0 + 8, :]
        hl_s[0:8, :] = hl
        for g in range(1, SEG):
            av = a_s[r0 + 8 * g:r0 + 8 * g + 8, :]
            hl = av * hl + b_s[r0 + 8 * g:r0 + 8 * g + 8, :]
            ac = av * ac
            hl_s[8 * g:8 * g + 8, :] = hl
            a_s[r0 + 8 * g:r0 + 8 * g + 8, :] = ac
        hin = hcar
        for s in range(8):
            hend = hl + ac * hin
            if s < 7:
                hin = jnp.where(row8 == s + 1, pltpu.roll(hend, 1, 0), hin)
        hcar = jnp.broadcast_to(hend[7:8, :], (8, D_LRU))
        for g in range(SEG):
            hg = hl_s[8 * g:8 * g + 8, :] + a_s[r0 + 8 * g:r0 + 8 * g + 8, :] * hin
            for l in range(D_LRU // 128):
                pout_s[l, 8 * g:8 * g + 8, :] = hg[:, l * 128:(l + 1) * 128]
        for j in range(HR // 8):
            h_s[r0 + 8 * j:r0 + 8 * j + 8, :] = jnp.concatenate(
                [pout_s[l, seg_rows(j), :] for l in range(D_LRU // 128)], axis=1)
        gate = proj_s[sec(i), D_LRU:2 * D_LRU]
        ymix_s[sec(i), 0:D_LRU] = (h_s[sec(i), :] * _gelu(gate)).astype(BF16)
        return hcar

    def kv_prep(i):
        k = proj_s[sec(i), 1536:1664]
        v = proj_s[sec(i), 1664:1792]
        for n, arr in enumerate(_head_variants(k) + _head_variants(v)):
            kv_s[n, WINDOW + i * HR:WINDOW + (i + 1) * HR, :] = arr

    qi = lax.broadcasted_iota(jnp.int32, (WINDOW, 2 * WINDOW), 0)
    kj = lax.broadcasted_iota(jnp.int32, (WINDOW, 2 * WINDOW), 1)
    dist = qi + WINDOW - kj
    valid = (dist >= 0) & (dist < WINDOW)
    valid0 = valid & (kj >= jnp.where(t == 0, WINDOW, 0))
    distf = dist.astype(F32)
    bias = [jnp.where(valid, -SLOPES[h] * distf, NEG_INF) for h in range(N_Q_HEADS)]
    bias0 = [jnp.where(valid0, -SLOPES[h] * distf, NEG_INF) for h in range(N_Q_HEADS)]
    lane_lo = lax.broadcasted_iota(jnp.int32, (WINDOW, 2 * HEAD_DIM), 1) < HEAD_DIM

    r512 = lax.broadcasted_iota(jnp.int32, (4 * WINDOW, 2 * HEAD_DIM), 0)
    l512 = lax.broadcasted_iota(jnp.int32, (4 * WINDOW, 2 * HEAD_DIM), 1)
    ones_cols = jnp.where((r512 < 2 * WINDOW) == (l512 < HEAD_DIM), 1.0, 0.0).astype(BF16)
    n_iter = (HR // WINDOW) * 4

    def qk(n):
        j, c = divmod(n, 4)
        hk = c // 2
        rows = slice(j * WINDOW, (j + 1) * WINDOW)
        win = slice(j * WINDOW, (j + 2) * WINDOW)
        qc = (proj_s[rows, 1024 + c * 128:1024 + (c + 1) * 128] * ATTN_SCALE).astype(BF16)
        kcat = jnp.concatenate([kv_s[2 * hk, win, :], kv_s[2 * hk + 1, win, :]], axis=0)
        return _dot_nt(qc, kcat)

    def softmax_pv(n, s):
        j, c = divmod(n, 4)
        hk = c // 2
        rows = slice(j * WINDOW, (j + 1) * WINDOW)
        win = slice(j * WINDOW, (j + 2) * WINDOW)
        vcat = jnp.concatenate([kv_s[4 + 2 * hk, win, :], kv_s[5 + 2 * hk, win, :]], axis=0)
        vaug = jnp.concatenate([vcat, ones_cols], axis=1)
        ps, es = [], []
        for half in range(2):
            h = 2 * c + half
            sink = sinks_ref[h]
            sh = s[:, half * 256:(half + 1) * 256] + (bias0[h] if j == 0 else bias[h])
            m = jnp.maximum(jnp.max(sh, axis=-1, keepdims=True), sink)
            ps.append(jnp.exp(sh - m).astype(BF16))
            es.append(jnp.exp(sink - m))
        oa = _dot(jnp.concatenate(ps, axis=1), vaug)
        den = oa[:, 128:256] + jnp.where(lane_lo, es[0], es[1])
        ymix_s[rows, D_LRU + c * 128:D_LRU + (c + 1) * 128] = (oa[:, 0:128] * (1.0 / den)).astype(BF16)

    def attention(i):
        depth = 2
        base = i * n_iter
        pend = [qk(base + n) for n in range(depth)]
        for n in range(n_iter):
            s = pend.pop(0)
            if n + depth < n_iter:
                pend.append(qk(base + n + depth))
            softmax_pv(base + n, s)

    def out_proj(i):
        x1_ref[0, sec(i), :] = x_ref[0, sec(i), :] + _dot(ymix_s[sec(i), :], wout_ref[...])

    n_sec = TM // HR
    h = hc_s[...]
    in_proj(0)
    lru_gates(0)
    for i in range(n_sec):
        if i + 1 < n_sec:
            in_proj(i + 1)
        h = lru_scan(i, h)
        kv_prep(i)
        attention(i)
        if i + 1 < n_sec:
            lru_gates(i + 1)
        out_proj(i)

    hc_s[...] = h
    ph_ref[0] = h
    pconv_ref[0] = proj_s[TM - 8:, 0:D_LRU]
    pk_ref[0] = proj_s[TM - WINDOW:, 1536:1664]
    pv_ref[0] = proj_s[TM - WINDOW:, 1664:1792]
    kv_s[:, 0:WINDOW, :] = kv_s[:, TM:TM + WINDOW, :]


def _ffn_chunks(n3, acc, gprev_fn, wg_ref, wu_ref, wfc_ref, bfc_ref, wd_ref, on_gate, row_scale=None):
    nc = D_FF // FF_CHUNK

    def up(c):
        cs = slice(c * FF_CHUNK, (c + 1) * FF_CHUNK)
        g, u = _dot(n3, wg_ref[:, cs]), _dot(n3, wu_ref[:, cs])
        if row_scale is not None:
            g, u = g * row_scale, u * row_scale
        return g, u

    nxt = up(0)
    for c in range(nc):
        cs = slice(c * FF_CHUNK, (c + 1) * FF_CHUNK)
        g, u = nxt
        if c + 1 < nc:
            nxt = up(c + 1)
        g2, g1 = gprev_fn(c, g)
        on_gate(c, g)
        wfc = wfc_ref[:, cs]
        conv = bfc_ref[:, cs] + wfc[0:1] * g2 + wfc[1:2] * g1 + wfc[2:3] * g
        hmid = (_gelu(conv) * u).astype(BF16)
        acc = acc + _dot(hmid, wd_ref[cs, :])
    return acc


def _prompt_ffn_kernel(x1_ref, mk_ref, mv_ref, xs1_ref, cmk_ref, cmv_ref, wq_ref, wo_ref,
                       wg_ref, wu_ref, wfc_ref, bfc_ref, wd_ref, gfin_ref,
                       y_ref, pffn_ref, xs2_ref,
                       oc_s, gbuf_s, gcar_s, sq_s, soc_s):
    t = pl.program_id(1)

    @pl.when(t == 0)
    def _():
        gcar_s[...] = jnp.zeros((8, D_FF), F32)

    _sample_cross(xs1_ref, cmk_ref, cmv_ref, wq_ref, wo_ref, xs2_ref, sq_s, soc_s)

    x1 = x1_ref[0]
    qc = (_dot(x1.astype(BF16), wq_ref[...]) * _rstd(x1)).astype(BF16)
    hsl = [slice(h * MEM_HEAD_DIM, (h + 1) * MEM_HEAD_DIM) for h in range(N_MEM_HEADS)]
    ss = [_dot_nt(qc[:, hs], mk_ref[0, :, hs]) for hs in hsl]
    for h, hs in enumerate(hsl):
        s = ss[h]
        m = jnp.max(s, axis=-1, keepdims=True)
        p = jnp.exp2((s - m) * (MEM_SCALE * LOG2E))
        l = jnp.sum(p, axis=-1, keepdims=True)
        o = _dot(p.astype(BF16), mv_ref[0, :, hs]) * (1.0 / l)
        oc_s[:, hs] = o.astype(BF16)
    x2 = x1 + _dot(oc_s[...], wo_ref[...])

    def gprev(c, g):
        cs = slice(c * FF_CHUNK, (c + 1) * FF_CHUNK)
        gbuf_s[0:8, :] = gcar_s[:, cs]
        gbuf_s[8:TF + 8, :] = g
        return gbuf_s[6:6 + TF, :], gbuf_s[7:7 + TF, :]

    def on_gate(c, g):
        cs = slice(c * FF_CHUNK, (c + 1) * FF_CHUNK)
        tail = g[TF - 8:, :]
        gcar_s[:, cs] = tail
        pffn_ref[0, :, cs] = tail

    acc = _ffn_chunks(x2.astype(BF16), x2, gprev, wg_ref, wu_ref, wfc_ref, bfc_ref, wd_ref, on_gate, _rstd(x2))
    y_ref[0] = _rmsnorm(acc, gfin_ref[...])


def _sample_mixer_kernel(sinks_ref, x_ref, prev4_ref, h0_ref, ck_ref, cv_ref, win_ref, wconv_ref, bconv_ref,
                         wax_ref, ba_ref, bx_ref, lam_ref, wout_ref, wkt_ref, wvt_ref,
                         x1_ref, sk_ref, sv_ref, xr_ref, h_ref,
                         q_s, yatt_s):
    R = S_ROWS
    x = x_ref[...]
    n1 = (x * _rstd(x)).astype(BF16)
    proj = _dot(n1, win_ref[...])
    xr = proj[:, 0:D_LRU]
    gate = proj[:, D_LRU:2 * D_LRU]
    q_s[...] = proj[:, 1024:1536] * ATTN_SCALE
    xr_ref[...] = xr

    tmod = lax.broadcasted_iota(jnp.int32, (R, D_LRU), 0) & 3
    prev4 = prev4_ref[...]
    xs1 = jnp.where(tmod >= 1, pltpu.roll(xr, 1, 0), pltpu.roll(prev4, R - 2, 0))
    xs2 = jnp.where(tmod >= 2, pltpu.roll(xr, 2, 0), pltpu.roll(prev4, R - 1, 0))
    xs3 = jnp.where(tmod >= 3, pltpu.roll(xr, 3, 0), prev4)
    wc = wconv_ref[...]
    xc = bconv_ref[...] + wc[0:1] * xs3 + wc[1:2] * xs2 + wc[2:3] * xs1 + wc[3:4] * xr

    a, b = _lru_gates(xc, wax_ref, ba_ref[...], bx_ref[...], lam_ref[...])
    for s in (1, 2):
        m = tmod >= s
        a_sh = pltpu.roll(a, s, 0)
        b_sh = pltpu.roll(b, s, 0)
        b = jnp.where(m, a * b_sh + b, b)
        a = jnp.where(m, a * a_sh, a)
    h = a * h0_ref[...] + b
    h_ref[...] = h
    y_lru = h * _gelu(gate)

    kt_new = _dot_nt(wkt_ref[...], n1)
    vt_new = _dot_nt(wvt_ref[...], n1)
    zpad = jnp.zeros((D_KV, WINDOW - R), F32)
    kt_pad = jnp.concatenate([kt_new, zpad], axis=1)
    vt_pad = jnp.concatenate([vt_new, zpad], axis=1)
    kt_pad_b = kt_pad.astype(BF16)
    vt_pad_b = vt_pad.astype(BF16)

    r64 = lax.broadcasted_iota(jnp.int32, (8 * N_Q_HEADS, WINDOW), 0)
    c64 = lax.broadcasted_iota(jnp.int32, (8 * N_Q_HEADS, WINDOW), 1)
    t64 = r64 & 3
    slope = jnp.zeros((8 * N_Q_HEADS, WINDOW), F32)
    sinkcol = jnp.zeros((8 * N_Q_HEADS, 1), F32)
    for hq in range(N_Q_HEADS):
        slope = jnp.where((r64 >> 3) == hq, SLOPES[hq], slope)
        sinkcol = jnp.where((r64[:, 0:1] >> 3) == hq, sinks_ref[hq], sinkcol)
    bias_c = jnp.where(c64 > t64, -slope * (t64 + WINDOW - c64).astype(F32), NEG_INF)
    bat0 = (r64 & 7) < 4
    lane_lo = lax.broadcasted_iota(jnp.int32, (8, D_KV), 1) < HEAD_DIM
    lane128 = lax.broadcasted_iota(jnp.int32, (D_KV, WINDOW), 1)

    def pair_scores(p):
        r0 = p * 8
        q8 = q_s[r0:r0 + 8, :]
        blocks = []
        for hq in range(N_Q_HEADS):
            chunk = q8[:, (hq // 2) * 128:(hq // 2 + 1) * 128]
            hk = hq // (N_Q_HEADS // N_KV_HEADS)
            src = chunk if (hq % 2) == hk else pltpu.roll(chunk, HEAD_DIM, 1)
            blocks.append(jnp.where(lane_lo == (hk == 0), src, 0.0))
        lhs = jnp.concatenate(blocks, axis=0).astype(BF16)
        sb = []
        for bb in range(2):
            bidx = 2 * p + bb
            kt = ck_ref[bidx]
            vt = cv_ref[bidx]
            sb.append(_dot(lhs, kt.astype(BF16)))
            shift = (WINDOW - 4 - 4 * bidx) % WINDOW
            sk_ref[bidx] = jnp.where(lane128 >= WINDOW - 4, pltpu.roll(kt_pad, shift, 1),
                                     pltpu.roll(kt, WINDOW - 4, 1))
            sv_ref[bidx] = jnp.where(lane128 >= WINDOW - 4, pltpu.roll(vt_pad, shift, 1),
                                     pltpu.roll(vt, WINDOW - 4, 1))
        s_c = jnp.where(bat0, sb[0], sb[1]) + bias_c
        same = (c64 >> 2) == (2 * p + ((r64 & 7) >> 2))
        dn = t64 - (c64 & 3)
        bias_n = jnp.where(same & (dn >= 0), -slope * dn.astype(F32), NEG_INF)
        s_n = _dot(lhs, kt_pad_b) + bias_n
        return s_c, s_n

    def pair_finish(p, s_c, s_n):
        r0 = p * 8
        m = jnp.maximum(jnp.maximum(jnp.max(s_c, axis=-1, keepdims=True), jnp.max(s_n, axis=-1, keepdims=True)),
                        sinkcol)
        pc = jnp.exp(s_c - m)
        pn = jnp.exp(s_n - m)
        l = jnp.sum(pc, axis=-1, keepdims=True) + jnp.sum(pn, axis=-1, keepdims=True) + jnp.exp(sinkcol - m)
        pcb = pc.astype(BF16)
        ob = [_dot_nt(pcb, cv_ref[2 * p + bb].astype(BF16)) for bb in range(2)]
        o = (jnp.where(bat0, ob[0], ob[1]) + _dot_nt(pn.astype(BF16), vt_pad_b)) * (1.0 / l)
        for c in range(N_Q_HEADS // 2):
            hk = c // 2
            ev = o[16 * c:16 * c + 8, :]
            od = o[16 * c + 8:16 * c + 16, :]
            if hk == 0:
                od = pltpu.roll(od, HEAD_DIM, 1)
            else:
                ev = pltpu.roll(ev, HEAD_DIM, 1)
            yatt_s[r0:r0 + 8, c * 128:(c + 1) * 128] = jnp.where(lane_lo, ev, od)

    n_pairs = S_NB // 2
    cur = pair_scores(0)
    for p in range(n_pairs):
        nxt = pair_scores(p + 1) if p + 1 < n_pairs else None
        pair_finish(p, *cur)
        cur = nxt

    ymix = jnp.concatenate([y_lru, yatt_s[...]], axis=1).astype(BF16)
    x1_ref[...] = x + _dot(ymix, wout_ref[...])


def _load_mem_heads(ref, b):
    return jnp.concatenate([ref[b, pl.ds(h, MEM_LEN, stride=N_MEM_HEADS), :] for h in range(N_MEM_HEADS)],
                           axis=1).astype(BF16)


def _sample_cross(x1_ref, mk_ref, mv_ref, wq_ref, wo_ref, x2_ref, q_s, oc_s):
    lane_head = lax.broadcasted_iota(jnp.int32, (8, D_MEM), 1) // MEM_HEAD_DIM
    bat0 = (lax.broadcasted_iota(jnp.int32, (32, D_MEM), 0) & 7) < 4
    bat0_s = (lax.broadcasted_iota(jnp.int32, (32, MEM_LEN), 0) & 7) < 4

    def q_proj():
        x1 = x1_ref[...]
        q_s[...] = _dot((x1 * _rstd(x1)).astype(BF16), wq_ref[...])

    def scores(p):
        q8 = q_s[p * 8:(p + 1) * 8, :]
        lhs = jnp.concatenate([jnp.where(lane_head == h, q8, 0.0) for h in range(N_MEM_HEADS)],
                              axis=0).astype(BF16)
        sb = [_dot_nt(lhs, _load_mem_heads(mk_ref, 2 * p + bb)) for bb in range(2)]
        return jnp.where(bat0_s, sb[0], sb[1]) * MEM_SCALE

    def attend():
        ss = [scores(p) for p in range(C_NB // 2)]
        for p, s in enumerate(ss):
            finish(p, s)

    def finish(p, s):
        m = jnp.max(s, axis=-1, keepdims=True)
        pe = jnp.exp(s - m)
        l = jnp.sum(pe, axis=-1, keepdims=True)
        pb = pe.astype(BF16)
        ob = [_dot(pb, _load_mem_heads(mv_ref, 2 * p + bb)) for bb in range(2)]
        o = jnp.where(bat0, ob[0], ob[1]) * (1.0 / l)
        out = jnp.where(lane_head == 0, o[0:8], 0.0)
        for h in range(1, N_MEM_HEADS):
            out = out + jnp.where(lane_head == h, o[h * 8:(h + 1) * 8], 0.0)
        oc_s[p * 8:(p + 1) * 8, :] = out

    def out_proj():
        x2_ref[...] = x1_ref[...] + _dot(oc_s[...].astype(BF16), wo_ref[...])

    q_proj()
    attend()
    out_proj()


def _sample_ffn_kernel(x2_ref, prev_ref, wg_ref, wu_ref, wfc_ref, bfc_ref, wd_ref, gfin_ref,
                       y_ref, gate_ref, slab_s, p1_s, p2_s):
    R = x2_ref.shape[0]
    nb = R // 4
    nl = FF_CHUNK // 128
    x2 = x2_ref[...]
    n3 = (x2 * _rstd(x2)).astype(BF16)
    tmod = lax.broadcasted_iota(jnp.int32, (R, FF_CHUNK), 0) & 3
    p1_s[...] = jnp.zeros(p1_s.shape, F32)
    p2_s[...] = jnp.zeros(p2_s.shape, F32)

    def gprev(c, g):
        for l in range(nl):
            cols = slice(c * FF_CHUNK + l * 128, c * FF_CHUNK + (l + 1) * 128)
            s0 = prev_ref[0, :, cols]
            s1 = prev_ref[1, :, cols]
            p1_s[l, pl.ds(0, nb, stride=4), :] = s1
            p2_s[l, pl.ds(0, nb, stride=4), :] = s0
            p2_s[l, pl.ds(1, nb, stride=4), :] = s1
        p1 = jnp.concatenate([p1_s[l] for l in range(nl)], axis=1)
        p2 = jnp.concatenate([p2_s[l] for l in range(nl)], axis=1)
        g1 = jnp.where(tmod >= 1, pltpu.roll(g, 1, 0), p1)
        g2 = jnp.where(tmod >= 2, pltpu.roll(g, 2, 0), p2)
        return g2, g1

    def on_gate(c, g):
        for l in range(nl):
            slab_s[l] = g[:, l * 128:(l + 1) * 128]
        for tt in range(2):
            gate_ref[tt, :, c * FF_CHUNK:(c + 1) * FF_CHUNK] = jnp.concatenate(
                [slab_s[l, pl.ds(2 + tt, nb, stride=4), :] for l in range(nl)], axis=1)

    acc = _ffn_chunks(n3, x2, gprev, wg_ref, wu_ref, wfc_ref, bfc_ref, wd_ref, on_gate)
    y_ref[...] = _rmsnorm(acc, gfin_ref[...])


def _const_spec(shape, grid_rank):
    zeros = (0,) * len(shape)
    if grid_rank == 1:
        return pl.BlockSpec(shape, lambda i: zeros, pipeline_mode=pl.Buffered(1))
    return pl.BlockSpec(shape, lambda i, j: zeros, pipeline_mode=pl.Buffered(1))


def _block_diag4(w):
    eye = jnp.eye(4, dtype=w.dtype)
    return (w[:, :, None, :] * eye[:, None, :, None]).reshape(4 * LRU_BLOCK, 4 * LRU_BLOCK)


def _params(sem):
    return pltpu.CompilerParams(dimension_semantics=sem, vmem_limit_bytes=VMEM_LIMIT)


def kernel(x_prompt, x_sample, cache_swa_k, cache_swa_v, cache_mem_k, cache_mem_v, state_lru_conv, state_lru_h, state_ffn_conv, mem_prompt, g_mix, w_in, w_lru_conv, b_lru_conv, w_lru_a, b_lru_a, w_lru_x, b_lru_x, lru_lambda, attn_sinks, w_out, g_cross, g_mem, w_mem_q, w_mem_k, w_mem_v, w_mem_o, g_ffn, w_ffn_gate, w_ffn_up, w_ffn_conv, b_ffn_conv, w_ffn_down, g_final):
    B, T, _ = x_prompt.shape
    NB = x_sample.shape[0]
    NT = T // TM
    SR = NB * 4

    win_f = g_mix[0][:, None] * w_in[0]
    win = win_f.astype(BF16)
    wout = w_out[0].astype(BF16)
    wq = (g_cross[0][:, None] * w_mem_q[0]).astype(BF16)
    wk = w_mem_k[0].astype(BF16)
    wv = w_mem_v[0].astype(BF16)
    wo = w_mem_o[0].astype(BF16)
    wg = (g_ffn[0][:, None] * w_ffn_gate[0]).astype(BF16)
    wu = (g_ffn[0][:, None] * w_ffn_up[0]).astype(BF16)
    wd = w_ffn_down[0].astype(BF16)
    wax = jnp.stack([
        jnp.concatenate([_block_diag4(w_lru_a[0, 4 * gi:4 * gi + 4]), _block_diag4(w_lru_x[0, 4 * gi:4 * gi + 4])],
                        axis=1) for gi in range(2)]).astype(BF16)
    gmem = g_mem
    gfin = g_final.reshape(1, D_MODEL)
    wconv, bconv = w_lru_conv[0], b_lru_conv
    ba, bx, lam = b_lru_a, b_lru_x, lru_lambda
    wfc, bfc = w_ffn_conv[0], b_ffn_conv
    sinks = attn_sinks[0]
    smem = pl.BlockSpec(memory_space=pltpu.SMEM)

    mk, mv, mkb, mvb = pl.pallas_call(
        _mem_kv_kernel,
        grid=(B,),
        in_specs=[pl.BlockSpec((1, MEM_LEN, D_MODEL), lambda b: (b, 0, 0)),
                  _const_spec((1, D_MODEL), 1), _const_spec((D_MODEL, D_MEM), 1), _const_spec((D_MODEL, D_MEM), 1)],
        out_specs=[pl.BlockSpec((1, MEM_LEN, D_MEM), lambda b: (b, 0, 0))] * 4,
        out_shape=[jax.ShapeDtypeStruct((B, MEM_LEN, D_MEM), F32)] * 2
        + [jax.ShapeDtypeStruct((B, MEM_LEN, D_MEM), BF16)] * 2,
        compiler_params=_params(("arbitrary",)),
        name="mem_kv",
    )(mem_prompt, gmem, wk, wv)

    x1p, pk, pv, pconv8, ph8 = pl.pallas_call(
        _prompt_mixer_kernel,
        grid=(B, NT),
        in_specs=[smem,
                  pl.BlockSpec((1, TM, D_MODEL), lambda b, t: (b, t, 0)),
                  _const_spec((D_MODEL, D_IN), 2),
                  _const_spec((LRU_CONV_W, D_LRU), 2), _const_spec((1, D_LRU), 2),
                  _const_spec((2, 256, 512), 2), _const_spec((1, D_LRU), 2), _const_spec((1, D_LRU), 2),
                  _const_spec((1, D_LRU), 2), _const_spec((D_MODEL, D_MODEL), 2)],
        out_specs=[pl.BlockSpec((1, TM, D_MODEL), lambda b, t: (b, t, 0)),
                   pl.BlockSpec((1, WINDOW, D_KV), lambda b, t: (b, 0, 0)),
                   pl.BlockSpec((1, WINDOW, D_KV), lambda b, t: (b, 0, 0)),
                   pl.BlockSpec((1, 8, D_LRU), lambda b, t: (b, 0, 0)),
                   pl.BlockSpec((1, 8, D_LRU), lambda b, t: (b, 0, 0))],
        out_shape=[jax.ShapeDtypeStruct((B, T, D_MODEL), F32),
                   jax.ShapeDtypeStruct((B, WINDOW, D_KV), F32),
                   jax.ShapeDtypeStruct((B, WINDOW, D_KV), F32),
                   jax.ShapeDtypeStruct((B, 8, D_LRU), F32),
                   jax.ShapeDtypeStruct((B, 8, D_LRU), F32)],
        scratch_shapes=[pltpu.VMEM((TM, D_IN), F32),
                        pltpu.VMEM((D_LRU // 128, HR, 128), F32),
                        pltpu.VMEM((D_LRU // 128, HR, 128), F32),
                        pltpu.VMEM((24, D_LRU), F32),
                        pltpu.VMEM((TM, D_LRU), F32),
                        pltpu.VMEM((TM, D_LRU), F32),
                        pltpu.VMEM((HR, D_LRU), F32),
                        pltpu.VMEM((TM, D_LRU), F32),
                        pltpu.VMEM((8, D_LRU), F32),
                        pltpu.VMEM((8, TM + WINDOW, D_KV), BF16),
                        pltpu.VMEM((TM, D_MODEL), BF16)],
        compiler_params=_params(("arbitrary", "arbitrary")),
        name="prompt_mixer",
    )(sinks, x_prompt, win, wconv, bconv, wax, ba, bx, lam, wout)

    xs = x_sample.reshape(SR, D_MODEL)
    conv_prev4 = jnp.pad(state_lru_conv[0], ((0, 0), (0, 1), (0, 0))).reshape(SR, D_LRU)
    h0rep = jnp.repeat(state_lru_h[0], 4, axis=0)
    ck = jnp.transpose(cache_swa_k[0], (0, 2, 3, 1)).reshape(NB, D_KV, WINDOW)
    cv = jnp.transpose(cache_swa_v[0], (0, 2, 3, 1)).reshape(NB, D_KV, WINDOW)
    wkt = jnp.transpose(win_f[:, 1536:1664]).astype(BF16)
    wvt = jnp.transpose(win_f[:, 1664:1792]).astype(BF16)
    row_spec = lambda w: pl.BlockSpec((S_ROWS, w), lambda i: (i, 0))
    cache_spec = pl.BlockSpec((S_NB, D_KV, WINDOW), lambda i: (i, 0, 0))
    x1s, sk, sv, xr_s, h_s = pl.pallas_call(
        _sample_mixer_kernel,
        grid=(NB // S_NB,),
        in_specs=[smem, row_spec(D_MODEL), row_spec(D_LRU), row_spec(D_LRU), cache_spec, cache_spec,
                  _const_spec((D_MODEL, D_IN), 1),
                  _const_spec((LRU_CONV_W, D_LRU), 1), _const_spec((1, D_LRU), 1),
                  _const_spec((2, 256, 512), 1), _const_spec((1, D_LRU), 1), _const_spec((1, D_LRU), 1),
                  _const_spec((1, D_LRU), 1), _const_spec((D_MODEL, D_MODEL), 1),
                  _const_spec((D_KV, D_MODEL), 1), _const_spec((D_KV, D_MODEL), 1)],
        out_specs=[row_spec(D_MODEL), cache_spec, cache_spec, row_spec(D_LRU), row_spec(D_LRU)],
        out_shape=[jax.ShapeDtypeStruct((SR, D_MODEL), F32),
                   jax.ShapeDtypeStruct((NB, D_KV, WINDOW), F32),
                   jax.ShapeDtypeStruct((NB, D_KV, WINDOW), F32),
                   jax.ShapeDtypeStruct((SR, D_LRU), F32),
                   jax.ShapeDtypeStruct((SR, D_LRU), F32)],
        scratch_shapes=[pltpu.VMEM((S_ROWS, D_ATTN), F32),
                        pltpu.VMEM((S_ROWS, D_ATTN), F32)],
        compiler_params=_params(("arbitrary",)),
        name="sample_mixer",
    )(sinks, xs, conv_prev4, h0rep, ck, cv, win, wconv, bconv, wax, ba, bx, lam, wout, wkt, wvt)

    cmk = cache_mem_k.reshape(NB, MEM_LEN * N_MEM_HEADS, MEM_HEAD_DIM)
    cmv = cache_mem_v.reshape(NB, MEM_LEN * N_MEM_HEADS, MEM_HEAD_DIM)
    ntf = T // TF
    assert NB == C_NB * B * ntf
    crow = pl.BlockSpec((C_ROWS, D_MODEL), lambda b, t: (b * ntf + t, 0))
    cmem = pl.BlockSpec((C_NB, MEM_LEN * N_MEM_HEADS, MEM_HEAD_DIM), lambda b, t: (b * ntf + t, 0, 0))
    y_prompt, pffn8, x2s = pl.pallas_call(
        _prompt_ffn_kernel,
        grid=(B, ntf),
        in_specs=[pl.BlockSpec((1, TF, D_MODEL), lambda b, t: (b, t, 0)),
                  pl.BlockSpec((1, MEM_LEN, D_MEM), lambda b, t: (b, 0, 0)),
                  pl.BlockSpec((1, MEM_LEN, D_MEM), lambda b, t: (b, 0, 0)),
                  crow, cmem, cmem,
                  _const_spec((D_MODEL, D_MEM), 2), _const_spec((D_MEM, D_MODEL), 2),
                  _const_spec((D_MODEL, D_FF), 2), _const_spec((D_MODEL, D_FF), 2),
                  _const_spec((FFN_CONV_W, D_FF), 2), _const_spec((1, D_FF), 2), _const_spec((D_FF, D_MODEL), 2),
                  _const_spec((1, D_MODEL), 2)],
        out_specs=[pl.BlockSpec((1, TF, D_MODEL), lambda b, t: (b, t, 0)),
                   pl.BlockSpec((1, 8, D_FF), lambda b, t: (b, 0, 0)),
                   crow],
        out_shape=[jax.ShapeDtypeStruct((B, T, D_MODEL), F32),
                   jax.ShapeDtypeStruct((B, 8, D_FF), F32),
                   jax.ShapeDtypeStruct((SR, D_MODEL), F32)],
        scratch_shapes=[pltpu.VMEM((TF, D_MEM), BF16),
                        pltpu.VMEM((TF + 8, FF_CHUNK), F32),
                        pltpu.VMEM((8, D_FF), F32),
                        pltpu.VMEM((C_ROWS, D_MEM), F32),
                        pltpu.VMEM((C_ROWS, D_MEM), F32)],
        compiler_params=_params(("arbitrary", "arbitrary")),
        name="prompt_ffn",
    )(x1p, mkb, mvb, x1s, cmk, cmv, wq, wo, wg, wu, wfc, bfc, wd, gfin)

    ffn_prev_tm = jnp.transpose(state_ffn_conv[0], (1, 0, 2))
    slab = pltpu.VMEM((FF_CHUNK // 128, SR, 128), F32)
    y_s, gate_tm = pl.pallas_call(
        _sample_ffn_kernel,
        grid=(1,),
        in_specs=[_const_spec((SR, D_MODEL), 1), _const_spec((FFN_CONV_W - 1, NB, D_FF), 1),
                  _const_spec((D_MODEL, D_FF), 1), _const_spec((D_MODEL, D_FF), 1),
                  _const_spec((FFN_CONV_W, D_FF), 1), _const_spec((1, D_FF), 1), _const_spec((D_FF, D_MODEL), 1),
                  _const_spec((1, D_MODEL), 1)],
        out_specs=[pl.BlockSpec((SR, D_MODEL), lambda i: (0, 0)),
                   pl.BlockSpec((FFN_CONV_W - 1, NB, D_FF), lambda i: (0, 0, 0))],
        out_shape=[jax.ShapeDtypeStruct((SR, D_MODEL), F32),
                   jax.ShapeDtypeStruct((FFN_CONV_W - 1, NB, D_FF), F32)],
        scratch_shapes=[slab, slab, slab],
        compiler_params=_params(("arbitrary",)),
        name="sample_ffn",
    )(x2s, ffn_prev_tm, wg, wu, wfc, bfc, wd, gfin)

    p_swa_k = pk.reshape(1, B, WINDOW, N_KV_HEADS, HEAD_DIM)
    p_swa_v = pv.reshape(1, B, WINDOW, N_KV_HEADS, HEAD_DIM)
    p_mem_k = mk.reshape(1, B, MEM_LEN, N_MEM_HEADS, MEM_HEAD_DIM)
    p_mem_v = mv.reshape(1, B, MEM_LEN, N_MEM_HEADS, MEM_HEAD_DIM)
    p_lru_conv = pconv8[None, :, 8 - (LRU_CONV_W - 1):, :]
    p_lru_h = ph8[None, :, 0, :]
    p_ffn_conv = pffn8[None, :, 8 - (FFN_CONV_W - 1):, :]
    y_sample = y_s.reshape(NB, 4, D_MODEL)
    s_swa_k = jnp.transpose(sk.reshape(NB, N_KV_HEADS, HEAD_DIM, WINDOW), (0, 3, 1, 2))[None]
    s_swa_v = jnp.transpose(sv.reshape(NB, N_KV_HEADS, HEAD_DIM, WINDOW), (0, 3, 1, 2))[None]
    s_lru_conv = xr_s.reshape(NB, 4, D_LRU)[None, :, 1:, :]
    s_lru_h = h_s.reshape(NB, 4, D_LRU)[None, :, 3, :]
    s_ffn_conv = jnp.transpose(gate_tm, (1, 0, 2))[None]
    return (y_prompt, y_sample, p_swa_k, p_swa_v, p_mem_k, p_mem_v, p_lru_conv, p_lru_h, p_ffn_conv,
            s_swa_k, s_swa_v, s_lru_conv, s_lru_h, s_ffn_conv)
```

```python
import functools

import jax
import jax.numpy as jnp
from jax import lax
from jax.experimental import pallas as pl
from jax.experimental.pallas import tpu as pltpu

D_MODEL = 1024
D_LRU = 512
LRU_BLOCK = 64
LRU_CONV_W = 4
LRU_C = 8.0
N_Q_HEADS = 8
N_KV_HEADS = 2
HEAD_DIM = 64
D_ATTN = 512
D_KV = 128
WINDOW = 128
D_IN = 1792
MEM_LEN = 256
N_MEM_HEADS = 4
MEM_HEAD_DIM = 128
D_MEM = 512
D_FF = 3072
FFN_CONV_W = 3
EPS = 1e-6
NEG_INF = -1e30

F32 = jnp.float32
BF16 = jnp.bfloat16

SLOPES = [float(2.0 ** (-8.0 * (i + 1) / N_Q_HEADS)) for i in range(N_Q_HEADS)]
ATTN_SCALE = HEAD_DIM ** -0.5
MEM_SCALE = MEM_HEAD_DIM ** -0.5
LOG2E = 1.4426950408889634
F32_TINY = 1.1754944e-38

TM = 1024
HR = 512
SEG = HR // 8
TF = 512
FF_CHUNK = 1024
S_NB = 32
S_ROWS = 4 * S_NB
C_NB = 4
C_ROWS = 4 * C_NB
VMEM_LIMIT = 56 * 1024 * 1024


def _dot(a, b):
    return jnp.dot(a, b, preferred_element_type=F32)


def _dot_nt(a, b):
    return lax.dot_general(a, b, (((1,), (1,)), ((), ())), preferred_element_type=F32)


def _rmsnorm(x, g):
    ms = jnp.mean(x * x, axis=-1, keepdims=True)
    return x * lax.rsqrt(ms + EPS) * g


def _gelu(x):
    c = 0.7978845608028654
    return x * (0.5 * (1.0 + jnp.tanh(c * (x + 0.044715 * (x * x * x)))))


def _sigmoid(x):
    return 1.0 / (1.0 + jnp.exp(-x))


def _softplus(x):
    return jnp.maximum(x, 0.0) + jnp.log1p(jnp.exp(-jnp.abs(x)))


def _lru_gates(xc, wax_ref, ba, bx, lam):
    xcb = xc.astype(BF16)
    pa, px = [], []
    for gi in range(2):
        pre = _dot(xcb[:, gi * 256:(gi + 1) * 256], wax_ref[gi])
        pa.append(pre[:, :256])
        px.append(pre[:, 256:])
    r = _sigmoid(jnp.concatenate(pa, axis=1) + ba)
    i = _sigmoid(jnp.concatenate(px, axis=1) + bx)
    log_a = (-LRU_C * _softplus(-lam)) * r
    a = jnp.exp(log_a)
    om = -jnp.tanh(log_a) * (a * a + 1.0)
    b = (om * lax.rsqrt(jnp.maximum(om, F32_TINY))) * (i * xc)
    return a, b


def _head_variants(t):
    lo = lax.broadcasted_iota(jnp.int32, t.shape, 1) < HEAD_DIM
    tr = pltpu.roll(t, HEAD_DIM, 1)
    z = jnp.zeros_like(t)
    x0 = jnp.where(lo, t, z)
    y0 = jnp.where(lo, z, tr)
    x1 = jnp.where(lo, tr, z)
    y1 = jnp.where(lo, z, t)
    return [v.astype(BF16) for v in (x0, y0, x1, y1)]


def _mem_kv_kernel(mem_ref, g_ref, wk_ref, wv_ref, mk_ref, mv_ref, mkb_ref, mvb_ref):
    n = _rmsnorm(mem_ref[0], g_ref[...]).astype(BF16)
    mk = _dot(n, wk_ref[...])
    mv = _dot(n, wv_ref[...])
    mk_ref[0] = mk
    mv_ref[0] = mv
    mkb_ref[0] = mk.astype(BF16)
    mvb_ref[0] = mv.astype(BF16)


def _prompt_mixer_kernel(sinks_ref, x_ref, g_ref, win_ref, wconv_ref, bconv_ref, wax_ref, ba_ref, bx_ref,
                         lam_ref, wout_ref,
                         x1_ref, pk_ref, pv_ref, pconv_ref, ph_ref,
                         proj_s, pin_s, pout_s, xtail_s, a_s, b_s, hl_s, h_s, hc_s, kv_s, ymix_s):
    t = pl.program_id(1)

    @pl.when(t == 0)
    def _():
        xtail_s[...] = jnp.zeros((24, D_LRU), F32)
        hc_s[...] = jnp.zeros((8, D_LRU), F32)
        kv_s[:, 0:WINDOW, :] = jnp.zeros((8, WINDOW, D_KV), BF16)

    def sec(i):
        return slice(i * HR, (i + 1) * HR)

    def in_proj(i):
        n1 = _rmsnorm(x_ref[0, sec(i), :], g_ref[...]).astype(BF16)
        proj_s[sec(i), :] = _dot(n1, win_ref[...])

    def seg_rows(j):
        return pl.ds(64 * (j % (SEG // 8)) + j // (SEG // 8), 8, stride=8)

    row8 = lax.broadcasted_iota(jnp.int32, (8, D_LRU), 0)

    def lru_gates(i):
        r0 = i * HR
        for j in range(HR // 8):
            for l in range(D_LRU // 128):
                pin_s[l, seg_rows(j), :] = proj_s[r0 + 8 * j:r0 + 8 * j + 8, l * 128:(l + 1) * 128]
        xs = jnp.concatenate([pin_s[l] for l in range(D_LRU // 128)], axis=1)
        tail = xs[HR - 24:, :]
        prev = xtail_s[...]
        xtail_s[...] = tail
        heads = [jnp.where(row8 == 0, pltpu.roll(prev[8 * q:8 * q + 8, :], 1, 0),
                           pltpu.roll(tail[8 * q:8 * q + 8, :], 1, 0)) for q in range(3)]
        sh1 = jnp.concatenate(heads[2:] + [xs[:HR - 8, :]], axis=0)
        sh2 = jnp.concatenate(heads[1:] + [xs[:HR - 16, :]], axis=0)
        sh3 = jnp.concatenate(heads + [xs[:HR - 24, :]], axis=0)
        wc = wconv_ref[...]
        xc = bconv_ref[...] + wc[0:1] * sh3 + wc[1:2] * sh2 + wc[2:3] * sh1 + wc[3:4] * xs
        a, b = _lru_gates(xc, wax_ref, ba_ref[...], bx_ref[...], lam_ref[...])
        a_s[sec(i), :] = a
        b_s[sec(i), :] = b

    def lru_scan(i, hcar):
        r0 = i * HR
        hl = b_s[r0:r0 + 8, :]
        ac = a_s[r0:r0 + 8, :]
        hl_s[0:8, :] = hl
        for g in range(1, SEG):
            av = a_s[r0 + 8 * g:r0 + 8 * g + 8, :]
            hl = av * hl + b_s[r0 + 8 * g:r0 + 8 * g + 8, :]
            ac = av * ac
            hl_s[8 * g:8 * g + 8, :] = hl
            a_s[r0 + 8 * g:r0 + 8 * g + 8, :] = ac
        hin = hcar
        for s in range(8):
            hend = hl + ac * hin
            if s < 7:
                hin = jnp.where(row8 == s + 1, pltpu.roll(hend, 1, 0), hin)
        hcar = jnp.broadcast_to(hend[7:8, :], (8, D_LRU))
        for g in range(SEG):
            hg = hl_s[8 * g:8 * g + 8, :] + a_s[r0 + 8 * g:r0 + 8 * g + 8, :] * hin
            for l in range(D_LRU // 128):
                pout_s[l, 8 * g:8 * g + 8, :] = hg[:, l * 128:(l + 1) * 128]
        for j in range(HR // 8):
            h_s[r0 + 8 * j:r0 + 8 * j + 8, :] = jnp.concatenate(
                [pout_s[l, seg_rows(j), :] for l in range(D_LRU // 128)], axis=1)
        gate = proj_s[sec(i), D_LRU:2 * D_LRU]
        ymix_s[sec(i), 0:D_LRU] = (h_s[sec(i), :] * _gelu(gate)).astype(BF16)
        return hcar

    def kv_prep(i):
        k = proj_s[sec(i), 1536:1664]
        v = proj_s[sec(i), 1664:1792]
        for n, arr in enumerate(_head_variants(k) + _head_variants(v)):
            kv_s[n, WINDOW + i * HR:WINDOW + (i + 1) * HR, :] = arr

    qi = lax.broadcasted_iota(jnp.int32, (WINDOW, 2 * WINDOW), 0)
    kj = lax.broadcasted_iota(jnp.int32, (WINDOW, 2 * WINDOW), 1)
    dist = qi + WINDOW - kj
    valid = (dist >= 0) & (dist < WINDOW)
    valid0 = valid & (kj >= jnp.where(t == 0, WINDOW, 0))
    distf = dist.astype(F32)
    bias = [jnp.where(valid, -SLOPES[h] * distf, NEG_INF) for h in range(N_Q_HEADS)]
    bias0 = [jnp.where(valid0, -SLOPES[h] * distf, NEG_INF) for h in range(N_Q_HEADS)]
    lane_lo = lax.broadcasted_iota(jnp.int32, (WINDOW, 2 * HEAD_DIM), 1) < HEAD_DIM

    r512 = lax.broadcasted_iota(jnp.int32, (4 * WINDOW, 2 * HEAD_DIM), 0)
    l512 = lax.broadcasted_iota(jnp.int32, (4 * WINDOW, 2 * HEAD_DIM), 1)
    ones_cols = jnp.where((r512 < 2 * WINDOW) == (l512 < HEAD_DIM), 1.0, 0.0).astype(BF16)
    n_iter = (HR // WINDOW) * 4

    def qk(n):
        j, c = divmod(n, 4)
        hk = c // 2
        rows = slice(j * WINDOW, (j + 1) * WINDOW)
        win = slice(j * WINDOW, (j + 2) * WINDOW)
        qc = (proj_s[rows, 1024 + c * 128:1024 + (c + 1) * 128] * ATTN_SCALE).astype(BF16)
        kcat = jnp.concatenate([kv_s[2 * hk, win, :], kv_s[2 * hk + 1, win, :]], axis=0)
        return _dot_nt(qc, kcat)

    def softmax_pv(n, s):
        j, c = divmod(n, 4)
        hk = c // 2
        rows = slice(j * WINDOW, (j + 1) * WINDOW)
        win = slice(j * WINDOW, (j + 2) * WINDOW)
        vcat = jnp.concatenate([kv_s[4 + 2 * hk, win, :], kv_s[5 + 2 * hk, win, :]], axis=0)
        vaug = jnp.concatenate([vcat, ones_cols], axis=1)
        ps, es = [], []
        for half in range(2):
            h = 2 * c + half
            sink = sinks_ref[h]
            sh = s[:, half * 256:(half + 1) * 256] + (bias0[h] if j == 0 else bias[h])
            m = jnp.maximum(jnp.max(sh, axis=-1, keepdims=True), sink)
            ps.append(jnp.exp(sh - m).astype(BF16))
            es.append(jnp.exp(sink - m))
        oa = _dot(jnp.concatenate(ps, axis=1), vaug)
        den = oa[:, 128:256] + jnp.where(lane_lo, es[0], es[1])
        ymix_s[rows, D_LRU + c * 128:D_LRU + (c + 1) * 128] = (oa[:, 0:128] * (1.0 / den)).astype(BF16)

    def attention(i):
        depth = 2
        base = i * n_iter
        pend = [qk(base + n) for n in range(depth)]
        for n in range(n_iter):
            s = pend.pop(0)
            if n + depth < n_iter:
                pend.append(qk(base + n + depth))
            softmax_pv(base + n, s)

    def out_proj(i):
        x1_ref[0, sec(i), :] = x_ref[0, sec(i), :] + _dot(ymix_s[sec(i), :], wout_ref[...])

    n_sec = TM // HR
    h = hc_s[...]
    in_proj(0)
    lru_gates(0)
    for i in range(n_sec):
        if i + 1 < n_sec:
            in_proj(i + 1)
        h = lru_scan(i, h)
        kv_prep(i)
        attention(i)
        if i + 1 < n_sec:
            lru_gates(i + 1)
        out_proj(i)

    hc_s[...] = h
    ph_ref[0] = h
    pconv_ref[0] = proj_s[TM - 8:, 0:D_LRU]
    pk_ref[0] = proj_s[TM - WINDOW:, 1536:1664]
    pv_ref[0] = proj_s[TM - WINDOW:, 1664:1792]
    kv_s[:, 0:WINDOW, :] = kv_s[:, TM:TM + WINDOW, :]


def _ffn_chunks(n3, acc, gprev_fn, wg_ref, wu_ref, wfc_ref, bfc_ref, wd_ref, on_gate):
    nc = D_FF // FF_CHUNK

    def up(c):
        cs = slice(c * FF_CHUNK, (c + 1) * FF_CHUNK)
        return _dot(n3, wg_ref[:, cs]), _dot(n3, wu_ref[:, cs])

    nxt = up(0)
    out = None
    for c in range(nc):
        cs = slice(c * FF_CHUNK, (c + 1) * FF_CHUNK)
        g, u = nxt
        if c + 1 < nc:
            nxt = up(c + 1)
        g2, g1 = gprev_fn(c, g)
        on_gate(c, g)
        wfc = wfc_ref[:, cs]
        conv = bfc_ref[:, cs] + wfc[0:1] * g2 + wfc[1:2] * g1 + wfc[2:3] * g
        hmid = (_gelu(conv) * u).astype(BF16)
        d = _dot(hmid, wd_ref[cs, :])
        out = d if out is None else out + d
    return acc + out


def _prompt_ffn_kernel(x1_ref, mk_ref, mv_ref, xs1_ref, cmk_ref, cmv_ref, gc_ref, wq_ref, wo_ref, gf_ref,
                       wg_ref, wu_ref, wfc_ref, bfc_ref, wd_ref, gfin_ref,
                       y_ref, pffn_ref, xs2_ref,
                       oc_s, gbuf_s, gcar_s, sq_s, soc_s):
    t = pl.program_id(1)

    @pl.when(t == 0)
    def _():
        gcar_s[...] = jnp.zeros((8, D_FF), F32)

    _sample_cross(xs1_ref, cmk_ref, cmv_ref, gc_ref, wq_ref, wo_ref, xs2_ref, sq_s, soc_s)

    x1 = x1_ref[0]
    qc = _dot(_rmsnorm(x1, gc_ref[...]).astype(BF16), wq_ref[...]).astype(BF16)
    hsl = [slice(h * MEM_HEAD_DIM, (h + 1) * MEM_HEAD_DIM) for h in range(N_MEM_HEADS)]
    ss = [_dot_nt(qc[:, hs], mk_ref[0, :, hs]) for hs in hsl]
    for h, hs in enumerate(hsl):
        s = ss[h]
        m = jnp.max(s, axis=-1, keepdims=True)
        p = jnp.exp2((s - m) * (MEM_SCALE * LOG2E))
        l = jnp.sum(p, axis=-1, keepdims=True)
        o = _dot(p.astype(BF16), mv_ref[0, :, hs]) * (1.0 / l)
        oc_s[:, hs] = o.astype(BF16)
    x2 = x1 + _dot(oc_s[...], wo_ref[...])
    n3 = _rmsnorm(x2, gf_ref[...]).astype(BF16)

    def gprev(c, g):
        cs = slice(c * FF_CHUNK, (c + 1) * FF_CHUNK)
        gbuf_s[0:8, :] = gcar_s[:, cs]
        gbuf_s[8:TF + 8, :] = g
        return gbuf_s[6:6 + TF, :], gbuf_s[7:7 + TF, :]

    def on_gate(c, g):
        cs = slice(c * FF_CHUNK, (c + 1) * FF_CHUNK)
        tail = g[TF - 8:, :]
        gcar_s[:, cs] = tail
        pffn_ref[0, :, cs] = tail

    acc = _ffn_chunks(n3, x2, gprev, wg_ref, wu_ref, wfc_ref, bfc_ref, wd_ref, on_gate)
    y_ref[0] = _rmsnorm(acc, gfin_ref[...])


def _sample_mixer_kernel(sinks_ref, x_ref, prev4_ref, h0_ref, ck_ref, cv_ref, g_ref, win_ref, wconv_ref, bconv_ref,
                         wax_ref, ba_ref, bx_ref, lam_ref, wout_ref, wkt_ref, wvt_ref,
                         x1_ref, sk_ref, sv_ref, xr_ref, h_ref,
                         q_s, yatt_s):
    R = S_ROWS
    x = x_ref[...]
    n1 = _rmsnorm(x, g_ref[...]).astype(BF16)
    proj = _dot(n1, win_ref[...])
    xr = proj[:, 0:D_LRU]
    gate = proj[:, D_LRU:2 * D_LRU]
    q_s[...] = proj[:, 1024:1536] * ATTN_SCALE
    xr_ref[...] = xr

    tmod = lax.broadcasted_iota(jnp.int32, (R, D_LRU), 0) & 3
    prev4 = prev4_ref[...]
    xs1 = jnp.where(tmod >= 1, pltpu.roll(xr, 1, 0), pltpu.roll(prev4, R - 2, 0))
    xs2 = jnp.where(tmod >= 2, pltpu.roll(xr, 2, 0), pltpu.roll(prev4, R - 1, 0))
    xs3 = jnp.where(tmod >= 3, pltpu.roll(xr, 3, 0), prev4)
    wc = wconv_ref[...]
    xc = bconv_ref[...] + wc[0:1] * xs3 + wc[1:2] * xs2 + wc[2:3] * xs1 + wc[3:4] * xr

    a, b = _lru_gates(xc, wax_ref, ba_ref[...], bx_ref[...], lam_ref[...])
    for s in (1, 2):
        m = tmod >= s
        a_sh = pltpu.roll(a, s, 0)
        b_sh = pltpu.roll(b, s, 0)
        b = jnp.where(m, a * b_sh + b, b)
        a = jnp.where(m, a * a_sh, a)
    h = a * h0_ref[...] + b
    h_ref[...] = h
    y_lru = h * _gelu(gate)

    kt_new = _dot_nt(wkt_ref[...], n1)
    vt_new = _dot_nt(wvt_ref[...], n1)
    kt_pad, vt_pad = kt_new, vt_new
    if R < WINDOW:
        zpad = jnp.zeros((D_KV, WINDOW - R), F32)
        kt_pad = jnp.concatenate([kt_new, zpad], axis=1)
        vt_pad = jnp.concatenate([vt_new, zpad], axis=1)
    kt_pad_b = kt_pad.astype(BF16)
    vt_pad_b = vt_pad.astype(BF16)

    r64 = lax.broadcasted_iota(jnp.int32, (8 * N_Q_HEADS, WINDOW), 0)
    c64 = lax.broadcasted_iota(jnp.int32, (8 * N_Q_HEADS, WINDOW), 1)
    t64 = r64 & 3
    slope = jnp.zeros((8 * N_Q_HEADS, WINDOW), F32)
    sinkcol = jnp.zeros((8 * N_Q_HEADS, 1), F32)
    for hq in range(N_Q_HEADS):
        slope = jnp.where((r64 >> 3) == hq, SLOPES[hq], slope)
        sinkcol = jnp.where((r64[:, 0:1] >> 3) == hq, sinks_ref[hq], sinkcol)
    bias_c = jnp.where(c64 > t64, -slope * (t64 + WINDOW - c64).astype(F32), NEG_INF)
    bat0 = (r64 & 7) < 4
    lane_lo = lax.broadcasted_iota(jnp.int32, (8, D_KV), 1) < HEAD_DIM
    lane128 = lax.broadcasted_iota(jnp.int32, (D_KV, WINDOW), 1)

    def pair_scores(p):
        r0 = p * 8
        q8 = q_s[r0:r0 + 8, :]
        blocks = []
        for hq in range(N_Q_HEADS):
            chunk = q8[:, (hq // 2) * 128:(hq // 2 + 1) * 128]
            hk = hq // (N_Q_HEADS // N_KV_HEADS)
            src = chunk if (hq % 2) == hk else pltpu.roll(chunk, HEAD_DIM, 1)
            blocks.append(jnp.where(lane_lo == (hk == 0), src, 0.0))
        lhs = jnp.concatenate(blocks, axis=0).astype(BF16)
        sb = []
        for bb in range(2):
            bidx = 2 * p + bb
            kt = ck_ref[bidx]
            vt = cv_ref[bidx]
            sb.append(_dot(lhs, kt.astype(BF16)))
            shift = (WINDOW - 4 - 4 * bidx) % WINDOW
            sk_ref[bidx] = jnp.where(lane128 >= WINDOW - 4, pltpu.roll(kt_pad, shift, 1),
                                     pltpu.roll(kt, WINDOW - 4, 1))
            sv_ref[bidx] = jnp.where(lane128 >= WINDOW - 4, pltpu.roll(vt_pad, shift, 1),
                                     pltpu.roll(vt, WINDOW - 4, 1))
        s_c = jnp.where(bat0, sb[0], sb[1]) + bias_c
        same = (c64 >> 2) == (2 * p + ((r64 & 7) >> 2))
        dn = t64 - (c64 & 3)
        bias_n = jnp.where(same & (dn >= 0), -slope * dn.astype(F32), NEG_INF)
        s_n = _dot(lhs, kt_pad_b) + bias_n
        return s_c, s_n

    def pair_finish(p, s_c, s_n):
        r0 = p * 8
        m = jnp.maximum(jnp.maximum(jnp.max(s_c, axis=-1, keepdims=True), jnp.max(s_n, axis=-1, keepdims=True)),
                        sinkcol)
        pc = jnp.exp(s_c - m)
        pn = jnp.exp(s_n - m)
        l = jnp.sum(pc, axis=-1, keepdims=True) + jnp.sum(pn, axis=-1, keepdims=True) + jnp.exp(sinkcol - m)
        pcb = pc.astype(BF16)
        ob = [_dot_nt(pcb, cv_ref[2 * p + bb].astype(BF16)) for bb in range(2)]
        o = (jnp.where(bat0, ob[0], ob[1]) + _dot_nt(pn.astype(BF16), vt_pad_b)) * (1.0 / l)
        for c in range(N_Q_HEADS // 2):
            hk = c // 2
            ev = o[16 * c:16 * c + 8, :]
            od = o[16 * c + 8:16 * c + 16, :]
            if hk == 0:
                od = pltpu.roll(od, HEAD_DIM, 1)
            else:
                ev = pltpu.roll(ev, HEAD_DIM, 1)
            yatt_s[r0:r0 + 8, c * 128:(c + 1) * 128] = jnp.where(lane_lo, ev, od)

    n_pairs = S_NB // 2
    cur = pair_scores(0)
    for p in range(n_pairs):
        nxt = pair_scores(p + 1) if p + 1 < n_pairs else None
        pair_finish(p, *cur)
        cur = nxt

    ymix = jnp.concatenate([y_lru, yatt_s[...]], axis=1).astype(BF16)
    x1_ref[...] = x + _dot(ymix, wout_ref[...])


def _load_mem_heads(ref, b):
    return jnp.concatenate([ref[b, pl.ds(h, MEM_LEN, stride=N_MEM_HEADS), :] for h in range(N_MEM_HEADS)],
                           axis=1).astype(BF16)


def _sample_cross(x1_ref, mk_ref, mv_ref, gc_ref, wq_ref, wo_ref, x2_ref, q_s, oc_s):
    lane_head = lax.broadcasted_iota(jnp.int32, (8, D_MEM), 1) // MEM_HEAD_DIM
    bat0 = (lax.broadcasted_iota(jnp.int32, (32, D_MEM), 0) & 7) < 4
    bat0_s = (lax.broadcasted_iota(jnp.int32, (32, MEM_LEN), 0) & 7) < 4

    def q_proj():
        q_s[...] = _dot(_rmsnorm(x1_ref[...], gc_ref[...]).astype(BF16), wq_ref[...])

    def scores(p):
        q8 = q_s[p * 8:(p + 1) * 8, :]
        lhs = jnp.concatenate([jnp.where(lane_head == h, q8, 0.0) for h in range(N_MEM_HEADS)],
                              axis=0).astype(BF16)
        sb = [_dot_nt(lhs, _load_mem_heads(mk_ref, 2 * p + bb)) for bb in range(2)]
        return jnp.where(bat0_s, sb[0], sb[1]) * MEM_SCALE

    def attend():
        ss = [scores(p) for p in range(C_NB // 2)]
        for p, s in enumerate(ss):
            finish(p, s)

    def finish(p, s):
        m = jnp.max(s, axis=-1, keepdims=True)
        pe = jnp.exp(s - m)
        l = jnp.sum(pe, axis=-1, keepdims=True)
        pb = pe.astype(BF16)
        ob = [_dot(pb, _load_mem_heads(mv_ref, 2 * p + bb)) for bb in range(2)]
        o = jnp.where(bat0, ob[0], ob[1]) * (1.0 / l)
        out = jnp.where(lane_head == 0, o[0:8], 0.0)
        for h in range(1, N_MEM_HEADS):
            out = out + jnp.where(lane_head == h, o[h * 8:(h + 1) * 8], 0.0)
        oc_s[p * 8:(p + 1) * 8, :] = out

    def out_proj():
        x2_ref[...] = x1_ref[...] + _dot(oc_s[...].astype(BF16), wo_ref[...])

    q_proj()
    attend()
    out_proj()


def _sample_ffn_kernel(x2_ref, prev_ref, gf_ref, wg_ref, wu_ref, wfc_ref, bfc_ref, wd_ref, gfin_ref,
                       y_ref, gate_ref, slab_s, p1_s, p2_s):
    R = x2_ref.shape[0]
    nb = R // 4
    nl = FF_CHUNK // 128
    x2 = x2_ref[...]
    n3 = _rmsnorm(x2, gf_ref[...]).astype(BF16)
    tmod = lax.broadcasted_iota(jnp.int32, (R, FF_CHUNK), 0) & 3
    p1_s[...] = jnp.zeros(p1_s.shape, F32)
    p2_s[...] = jnp.zeros(p2_s.shape, F32)

    def gprev(c, g):
        for l in range(nl):
            cols = slice(c * FF_CHUNK + l * 128, c * FF_CHUNK + (l + 1) * 128)
            s0 = prev_ref[0, :, cols]
            s1 = prev_ref[1, :, cols]
            p1_s[l, pl.ds(0, nb, stride=4), :] = s1
            p2_s[l, pl.ds(0, nb, stride=4), :] = s0
            p2_s[l, pl.ds(1, nb, stride=4), :] = s1
        p1 = jnp.concatenate([p1_s[l] for l in range(nl)], axis=1)
        p2 = jnp.concatenate([p2_s[l] for l in range(nl)], axis=1)
        g1 = jnp.where(tmod >= 1, pltpu.roll(g, 1, 0), p1)
        g2 = jnp.where(tmod >= 2, pltpu.roll(g, 2, 0), p2)
        return g2, g1

    def on_gate(c, g):
        for l in range(nl):
            slab_s[l] = g[:, l * 128:(l + 1) * 128]
        for tt in range(2):
            gate_ref[tt, :, c * FF_CHUNK:(c + 1) * FF_CHUNK] = jnp.concatenate(
                [slab_s[l, pl.ds(2 + tt, nb, stride=4), :] for l in range(nl)], axis=1)

    acc = _ffn_chunks(n3, x2, gprev, wg_ref, wu_ref, wfc_ref, bfc_ref, wd_ref, on_gate)
    y_ref[...] = _rmsnorm(acc, gfin_ref[...])


def _const_spec(shape, grid_rank):
    zeros = (0,) * len(shape)
    if grid_rank == 1:
        return pl.BlockSpec(shape, lambda i: zeros, pipeline_mode=pl.Buffered(1))
    return pl.BlockSpec(shape, lambda i, j: zeros, pipeline_mode=pl.Buffered(1))


def _block_diag4(w):
    eye = jnp.eye(4, dtype=w.dtype)
    return (w[:, :, None, :] * eye[:, None, :, None]).reshape(4 * LRU_BLOCK, 4 * LRU_BLOCK)


def _params(sem):
    return pltpu.CompilerParams(dimension_semantics=sem, vmem_limit_bytes=VMEM_LIMIT)


def kernel(x_prompt, x_sample, cache_swa_k, cache_swa_v, cache_mem_k, cache_mem_v, state_lru_conv, state_lru_h, state_ffn_conv, mem_prompt, g_mix, w_in, w_lru_conv, b_lru_conv, w_lru_a, b_lru_a, w_lru_x, b_lru_x, lru_lambda, attn_sinks, w_out, g_cross, g_mem, w_mem_q, w_mem_k, w_mem_v, w_mem_o, g_ffn, w_ffn_gate, w_ffn_up, w_ffn_conv, b_ffn_conv, w_ffn_down, g_final):
    B, T, _ = x_prompt.shape
    NB = x_sample.shape[0]
    NT = T // TM
    SR = NB * 4

    win = w_in[0].astype(BF16)
    wout = w_out[0].astype(BF16)
    wq = w_mem_q[0].astype(BF16)
    wk = w_mem_k[0].astype(BF16)
    wv = w_mem_v[0].astype(BF16)
    wo = w_mem_o[0].astype(BF16)
    wg = w_ffn_gate[0].astype(BF16)
    wu = w_ffn_up[0].astype(BF16)
    wd = w_ffn_down[0].astype(BF16)
    wax = jnp.stack([
        jnp.concatenate([_block_diag4(w_lru_a[0, 4 * gi:4 * gi + 4]), _block_diag4(w_lru_x[0, 4 * gi:4 * gi + 4])],
                        axis=1) for gi in range(2)]).astype(BF16)
    gmix, gcross, gmem, gffn = g_mix, g_cross, g_mem, g_ffn
    gfin = g_final.reshape(1, D_MODEL)
    wconv, bconv = w_lru_conv[0], b_lru_conv
    ba, bx, lam = b_lru_a, b_lru_x, lru_lambda
    wfc, bfc = w_ffn_conv[0], b_ffn_conv
    sinks = attn_sinks[0]
    smem = pl.BlockSpec(memory_space=pltpu.SMEM)

    mk, mv, mkb, mvb = pl.pallas_call(
        _mem_kv_kernel,
        grid=(B,),
        in_specs=[pl.BlockSpec((1, MEM_LEN, D_MODEL), lambda b: (b, 0, 0)),
                  _const_spec((1, D_MODEL), 1), _const_spec((D_MODEL, D_MEM), 1), _const_spec((D_MODEL, D_MEM), 1)],
        out_specs=[pl.BlockSpec((1, MEM_LEN, D_MEM), lambda b: (b, 0, 0))] * 4,
        out_shape=[jax.ShapeDtypeStruct((B, MEM_LEN, D_MEM), F32)] * 2
        + [jax.ShapeDtypeStruct((B, MEM_LEN, D_MEM), BF16)] * 2,
        compiler_params=_params(("arbitrary",)),
        name="mem_kv",
    )(mem_prompt, gmem, wk, wv)

    x1p, pk, pv, pconv8, ph8 = pl.pallas_call(
        _prompt_mixer_kernel,
        grid=(B, NT),
        in_specs=[smem,
                  pl.BlockSpec((1, TM, D_MODEL), lambda b, t: (b, t, 0)),
                  _const_spec((1, D_MODEL), 2), _const_spec((D_MODEL, D_IN), 2),
                  _const_spec((LRU_CONV_W, D_LRU), 2), _const_spec((1, D_LRU), 2),
                  _const_spec((2, 256, 512), 2), _const_spec((1, D_LRU), 2), _const_spec((1, D_LRU), 2),
                  _const_spec((1, D_LRU), 2), _const_spec((D_MODEL, D_MODEL), 2)],
        out_specs=[pl.BlockSpec((1, TM, D_MODEL), lambda b, t: (b, t, 0)),
                   pl.BlockSpec((1, WINDOW, D_KV), lambda b, t: (b, 0, 0)),
                   pl.BlockSpec((1, WINDOW, D_KV), lambda b, t: (b, 0, 0)),
                   pl.BlockSpec((1, 8, D_LRU), lambda b, t: (b, 0, 0)),
                   pl.BlockSpec((1, 8, D_LRU), lambda b, t: (b, 0, 0))],
        out_shape=[jax.ShapeDtypeStruct((B, T, D_MODEL), F32),
                   jax.ShapeDtypeStruct((B, WINDOW, D_KV), F32),
                   jax.ShapeDtypeStruct((B, WINDOW, D_KV), F32),
                   jax.ShapeDtypeStruct((B, 8, D_LRU), F32),
                   jax.ShapeDtypeStruct((B, 8, D_LRU), F32)],
        scratch_shapes=[pltpu.VMEM((TM, D_IN), F32),
                        pltpu.VMEM((D_LRU // 128, HR, 128), F32),
                        pltpu.VMEM((D_LRU // 128, HR, 128), F32),
                        pltpu.VMEM((24, D_LRU), F32),
                        pltpu.VMEM((TM, D_LRU), F32),
                        pltpu.VMEM((TM, D_LRU), F32),
                        pltpu.VMEM((HR, D_LRU), F32),
                        pltpu.VMEM((TM, D_LRU), F32),
                        pltpu.VMEM((8, D_LRU), F32),
                        pltpu.VMEM((8, TM + WINDOW, D_KV), BF16),
                        pltpu.VMEM((TM, D_MODEL), BF16)],
        compiler_params=_params(("arbitrary", "arbitrary")),
        name="prompt_mixer",
    )(sinks, x_prompt, gmix, win, wconv, bconv, wax, ba, bx, lam, wout)

    xs = x_sample.reshape(SR, D_MODEL)
    conv_prev4 = jnp.pad(state_lru_conv[0], ((0, 0), (0, 1), (0, 0))).reshape(SR, D_LRU)
    h0rep = jnp.repeat(state_lru_h[0], 4, axis=0)
    ck = jnp.transpose(cache_swa_k[0], (0, 2, 3, 1)).reshape(NB, D_KV, WINDOW)
    cv = jnp.transpose(cache_swa_v[0], (0, 2, 3, 1)).reshape(NB, D_KV, WINDOW)
    wkt = jnp.transpose(w_in[0][:, 1536:1664]).astype(BF16)
    wvt = jnp.transpose(w_in[0][:, 1664:1792]).astype(BF16)
    row_spec = lambda w: pl.BlockSpec((S_ROWS, w), lambda i: (i, 0))
    cache_spec = pl.BlockSpec((S_NB, D_KV, WINDOW), lambda i: (i, 0, 0))
    x1s, sk, sv, xr_s, h_s = pl.pallas_call(
        _sample_mixer_kernel,
        grid=(NB // S_NB,),
        in_specs=[smem, row_spec(D_MODEL), row_spec(D_LRU), row_spec(D_LRU), cache_spec, cache_spec,
                  _const_spec((1, D_MODEL), 1), _const_spec((D_MODEL, D_IN), 1),
                  _const_spec((LRU_CONV_W, D_LRU), 1), _const_spec((1, D_LRU), 1),
                  _const_spec((2, 256, 512), 1), _const_spec((1, D_LRU), 1), _const_spec((1, D_LRU), 1),
                  _const_spec((1, D_LRU), 1), _const_spec((D_MODEL, D_MODEL), 1),
                  _const_spec((D_KV, D_MODEL), 1), _const_spec((D_KV, D_MODEL), 1)],
        out_specs=[row_spec(D_MODEL), cache_spec, cache_spec, row_spec(D_LRU), row_spec(D_LRU)],
        out_shape=[jax.ShapeDtypeStruct((SR, D_MODEL), F32),
                   jax.ShapeDtypeStruct((NB, D_KV, WINDOW), F32),
                   jax.ShapeDtypeStruct((NB, D_KV, WINDOW), F32),
                   jax.ShapeDtypeStruct((SR, D_LRU), F32),
                   jax.ShapeDtypeStruct((SR, D_LRU), F32)],
        scratch_shapes=[pltpu.VMEM((S_ROWS, D_ATTN), F32),
                        pltpu.VMEM((S_ROWS, D_ATTN), F32)],
        compiler_params=_params(("arbitrary",)),
        name="sample_mixer",
    )(sinks, xs, conv_prev4, h0rep, ck, cv, gmix, win, wconv, bconv, wax, ba, bx, lam, wout, wkt, wvt)

    cmk = cache_mem_k.reshape(NB, MEM_LEN * N_MEM_HEADS, MEM_HEAD_DIM)
    cmv = cache_mem_v.reshape(NB, MEM_LEN * N_MEM_HEADS, MEM_HEAD_DIM)
    ntf = T // TF
    assert NB == C_NB * B * ntf
    crow = pl.BlockSpec((C_ROWS, D_MODEL), lambda b, t: (b * ntf + t, 0))
    cmem = pl.BlockSpec((C_NB, MEM_LEN * N_MEM_HEADS, MEM_HEAD_DIM), lambda b, t: (b * ntf + t, 0, 0))
    y_prompt, pffn8, x2s = pl.pallas_call(
        _prompt_ffn_kernel,
        grid=(B, ntf),
        in_specs=[pl.BlockSpec((1, TF, D_MODEL), lambda b, t: (b, t, 0)),
                  pl.BlockSpec((1, MEM_LEN, D_MEM), lambda b, t: (b, 0, 0)),
                  pl.BlockSpec((1, MEM_LEN, D_MEM), lambda b, t: (b, 0, 0)),
                  crow, cmem, cmem,
                  _const_spec((1, D_MODEL), 2), _const_spec((D_MODEL, D_MEM), 2), _const_spec((D_MEM, D_MODEL), 2),
                  _const_spec((1, D_MODEL), 2), _const_spec((D_MODEL, D_FF), 2), _const_spec((D_MODEL, D_FF), 2),
                  _const_spec((FFN_CONV_W, D_FF), 2), _const_spec((1, D_FF), 2), _const_spec((D_FF, D_MODEL), 2),
                  _const_spec((1, D_MODEL), 2)],
        out_specs=[pl.BlockSpec((1, TF, D_MODEL), lambda b, t: (b, t, 0)),
                   pl.BlockSpec((1, 8, D_FF), lambda b, t: (b, 0, 0)),
                   crow],
        out_shape=[jax.ShapeDtypeStruct((B, T, D_MODEL), F32),
                   jax.ShapeDtypeStruct((B, 8, D_FF), F32),
                   jax.ShapeDtypeStruct((SR, D_MODEL), F32)],
        scratch_shapes=[pltpu.VMEM((TF, D_MEM), BF16),
                        pltpu.VMEM((TF + 8, FF_CHUNK), F32),
                        pltpu.VMEM((8, D_FF), F32),
                        pltpu.VMEM((C_ROWS, D_MEM), F32),
                        pltpu.VMEM((C_ROWS, D_MEM), F32)],
        compiler_params=_params(("arbitrary", "arbitrary")),
        name="prompt_ffn",
    )(x1p, mkb, mvb, x1s, cmk, cmv, gcross, wq, wo, gffn, wg, wu, wfc, bfc, wd, gfin)

    ffn_prev_tm = jnp.transpose(state_ffn_conv[0], (1, 0, 2))
    slab = pltpu.VMEM((FF_CHUNK // 128, SR, 128), F32)
    y_s, gate_tm = pl.pallas_call(
        _sample_ffn_kernel,
        grid=(1,),
        in_specs=[_const_spec((SR, D_MODEL), 1), _const_spec((FFN_CONV_W - 1, NB, D_FF), 1),
                  _const_spec((1, D_MODEL), 1), _const_spec((D_MODEL, D_FF), 1), _const_spec((D_MODEL, D_FF), 1),
                  _const_spec((FFN_CONV_W, D_FF), 1), _const_spec((1, D_FF), 1), _const_spec((D_FF, D_MODEL), 1),
                  _const_spec((1, D_MODEL), 1)],
        out_specs=[pl.BlockSpec((SR, D_MODEL), lambda i: (0, 0)),
                   pl.BlockSpec((FFN_CONV_W - 1, NB, D_FF), lambda i: (0, 0, 0))],
        out_shape=[jax.ShapeDtypeStruct((SR, D_MODEL), F32),
                   jax.ShapeDtypeStruct((FFN_CONV_W - 1, NB, D_FF), F32)],
        scratch_shapes=[slab, slab, slab],
        compiler_params=_params(("arbitrary",)),
        name="sample_ffn",
    )(x2s, ffn_prev_tm, gffn, wg, wu, wfc, bfc, wd, gfin)

    p_swa_k = pk.reshape(1, B, WINDOW, N_KV_HEADS, HEAD_DIM)
    p_swa_v = pv.reshape(1, B, WINDOW, N_KV_HEADS, HEAD_DIM)
    p_mem_k = mk.reshape(1, B, MEM_LEN, N_MEM_HEADS, MEM_HEAD_DIM)
    p_mem_v = mv.reshape(1, B, MEM_LEN, N_MEM_HEADS, MEM_HEAD_DIM)
    p_lru_conv = pconv8[None, :, 8 - (LRU_CONV_W - 1):, :]
    p_lru_h = ph8[None, :, 0, :]
    p_ffn_conv = pffn8[None, :, 8 - (FFN_CONV_W - 1):, :]
    y_sample = y_s.reshape(NB, 4, D_MODEL)
    s_swa_k = jnp.transpose(sk.reshape(NB, N_KV_HEADS, HEAD_DIM, WINDOW), (0, 3, 1, 2))[None]
    s_swa_v = jnp.transpose(sv.reshape(NB, N_KV_HEADS, HEAD_DIM, WINDOW), (0, 3, 1, 2))[None]
    s_lru_conv = xr_s.reshape(NB, 4, D_LRU)[None, :, 1:, :]
    s_lru_h = h_s.reshape(NB, 4, D_LRU)[None, :, 3, :]
    s_ffn_conv = jnp.transpose(gate_tm, (1, 0, 2))[None]
    return (y_prompt, y_sample, p_swa_k, p_swa_v, p_mem_k, p_mem_v, p_lru_conv, p_lru_h, p_ffn_conv,
            s_swa_k, s_swa_v, s_lru_conv, s_lru_h, s_ffn_conv)
```

```python
import functools

import jax
import jax.numpy as jnp
from jax import lax
from jax.experimental import pallas as pl
from jax.experimental.pallas import tpu as pltpu

D_MODEL = 1024
D_LRU = 512
LRU_BLOCK = 64
LRU_CONV_W = 4
LRU_C = 8.0
N_Q_HEADS = 8
N_KV_HEADS = 2
HEAD_DIM = 64
D_ATTN = 512
D_KV = 128
WINDOW = 128
D_IN = 1792
MEM_LEN = 256
N_MEM_HEADS = 4
MEM_HEAD_DIM = 128
D_MEM = 512
D_FF = 3072
FFN_CONV_W = 3
EPS = 1e-6
NEG_INF = -1e30

F32 = jnp.float32
BF16 = jnp.bfloat16

SLOPES = [float(2.0 ** (-8.0 * (i + 1) / N_Q_HEADS)) for i in range(N_Q_HEADS)]
ATTN_SCALE = HEAD_DIM ** -0.5
MEM_SCALE = MEM_HEAD_DIM ** -0.5
LOG2E = 1.4426950408889634
F32_TINY = 1.1754944e-38

TM = 1024
HR = 512
SEG = HR // 8
TF = 512
FF_CHUNK = 1024
S_NB = 16
S_ROWS = 4 * S_NB
C_NB = 4
C_ROWS = 4 * C_NB
VMEM_LIMIT = 56 * 1024 * 1024


def _dot(a, b):
    return jnp.dot(a, b, preferred_element_type=F32)


def _dot_nt(a, b):
    return lax.dot_general(a, b, (((1,), (1,)), ((), ())), preferred_element_type=F32)


def _rmsnorm(x, g):
    ms = jnp.mean(x * x, axis=-1, keepdims=True)
    return x * lax.rsqrt(ms + EPS) * g


def _gelu(x):
    c = 0.7978845608028654
    return x * (0.5 * (1.0 + jnp.tanh(c * (x + 0.044715 * (x * x * x)))))


def _sigmoid(x):
    return 1.0 / (1.0 + jnp.exp(-x))


def _softplus(x):
    return jnp.maximum(x, 0.0) + jnp.log1p(jnp.exp(-jnp.abs(x)))


def _lru_gates(xc, wax_ref, ba, bx, lam):
    xcb = xc.astype(BF16)
    pa, px = [], []
    for gi in range(2):
        pre = _dot(xcb[:, gi * 256:(gi + 1) * 256], wax_ref[gi])
        pa.append(pre[:, :256])
        px.append(pre[:, 256:])
    r = _sigmoid(jnp.concatenate(pa, axis=1) + ba)
    i = _sigmoid(jnp.concatenate(px, axis=1) + bx)
    log_a = (-LRU_C * _softplus(-lam)) * r
    a = jnp.exp(log_a)
    om = -jnp.tanh(log_a) * (a * a + 1.0)
    b = (om * lax.rsqrt(jnp.maximum(om, F32_TINY))) * (i * xc)
    return a, b


def _head_variants(t):
    lo = lax.broadcasted_iota(jnp.int32, t.shape, 1) < HEAD_DIM
    tr = pltpu.roll(t, HEAD_DIM, 1)
    z = jnp.zeros_like(t)
    x0 = jnp.where(lo, t, z)
    y0 = jnp.where(lo, z, tr)
    x1 = jnp.where(lo, tr, z)
    y1 = jnp.where(lo, z, t)
    return [v.astype(BF16) for v in (x0, y0, x1, y1)]


def _mem_kv_kernel(mem_ref, g_ref, wk_ref, wv_ref, mk_ref, mv_ref, mkb_ref, mvb_ref):
    n = _rmsnorm(mem_ref[0], g_ref[...]).astype(BF16)
    mk = _dot(n, wk_ref[...])
    mv = _dot(n, wv_ref[...])
    mk_ref[0] = mk
    mv_ref[0] = mv
    mkb_ref[0] = mk.astype(BF16)
    mvb_ref[0] = mv.astype(BF16)


def _prompt_mixer_kernel(sinks_ref, x_ref, g_ref, win_ref, wconv_ref, bconv_ref, wax_ref, ba_ref, bx_ref,
                         lam_ref, wout_ref,
                         x1_ref, pk_ref, pv_ref, pconv_ref, ph_ref,
                         proj_s, pin_s, pout_s, xtail_s, a_s, b_s, hl_s, h_s, hc_s, kv_s, ymix_s):
    t = pl.program_id(1)

    @pl.when(t == 0)
    def _():
        xtail_s[...] = jnp.zeros((24, D_LRU), F32)
        hc_s[...] = jnp.zeros((8, D_LRU), F32)
        kv_s[:, 0:WINDOW, :] = jnp.zeros((8, WINDOW, D_KV), BF16)

    def sec(i):
        return slice(i * HR, (i + 1) * HR)

    def in_proj(i):
        n1 = _rmsnorm(x_ref[0, sec(i), :], g_ref[...]).astype(BF16)
        proj_s[sec(i), :] = _dot(n1, win_ref[...])

    def seg_rows(j):
        return pl.ds(64 * (j % (SEG // 8)) + j // (SEG // 8), 8, stride=8)

    row8 = lax.broadcasted_iota(jnp.int32, (8, D_LRU), 0)

    def lru_gates(i):
        r0 = i * HR
        for j in range(HR // 8):
            for l in range(D_LRU // 128):
                pin_s[l, seg_rows(j), :] = proj_s[r0 + 8 * j:r0 + 8 * j + 8, l * 128:(l + 1) * 128]
        xs = jnp.concatenate([pin_s[l] for l in range(D_LRU // 128)], axis=1)
        tail = xs[HR - 24:, :]
        prev = xtail_s[...]
        xtail_s[...] = tail
        heads = [jnp.where(row8 == 0, pltpu.roll(prev[8 * q:8 * q + 8, :], 1, 0),
                           pltpu.roll(tail[8 * q:8 * q + 8, :], 1, 0)) for q in range(3)]
        sh1 = jnp.concatenate(heads[2:] + [xs[:HR - 8, :]], axis=0)
        sh2 = jnp.concatenate(heads[1:] + [xs[:HR - 16, :]], axis=0)
        sh3 = jnp.concatenate(heads + [xs[:HR - 24, :]], axis=0)
        wc = wconv_ref[...]
        xc = bconv_ref[...] + wc[0:1] * sh3 + wc[1:2] * sh2 + wc[2:3] * sh1 + wc[3:4] * xs
        a, b = _lru_gates(xc, wax_ref, ba_ref[...], bx_ref[...], lam_ref[...])
        a_s[sec(i), :] = a
        b_s[sec(i), :] = b

    def lru_scan(i, hcar):
        r0 = i * HR
        hl = b_s[r0:r0 + 8, :]
        ac = a_s[r0:r0 + 8, :]
        hl_s[0:8, :] = hl
        for g in range(1, SEG):
            av = a_s[r0 + 8 * g:r0 + 8 * g + 8, :]
            hl = av * hl + b_s[r0 + 8 * g:r0 + 8 * g + 8, :]
            ac = av * ac
            hl_s[8 * g:8 * g + 8, :] = hl
            a_s[r0 + 8 * g:r0 + 8 * g + 8, :] = ac
        hin = hcar
        for s in range(8):
            hend = hl + ac * hin
            if s < 7:
                hin = jnp.where(row8 == s + 1, pltpu.roll(hend, 1, 0), hin)
        hcar = jnp.broadcast_to(hend[7:8, :], (8, D_LRU))
        for g in range(SEG):
            hg = hl_s[8 * g:8 * g + 8, :] + a_s[r0 + 8 * g:r0 + 8 * g + 8, :] * hin
            for l in range(D_LRU // 128):
                pout_s[l, 8 * g:8 * g + 8, :] = hg[:, l * 128:(l + 1) * 128]
        for j in range(HR // 8):
            h_s[r0 + 8 * j:r0 + 8 * j + 8, :] = jnp.concatenate(
                [pout_s[l, seg_rows(j), :] for l in range(D_LRU // 128)], axis=1)
        gate = proj_s[sec(i), D_LRU:2 * D_LRU]
        ymix_s[sec(i), 0:D_LRU] = (h_s[sec(i), :] * _gelu(gate)).astype(BF16)
        return hcar

    def kv_prep(i):
        k = proj_s[sec(i), 1536:1664]
        v = proj_s[sec(i), 1664:1792]
        for n, arr in enumerate(_head_variants(k) + _head_variants(v)):
            kv_s[n, WINDOW + i * HR:WINDOW + (i + 1) * HR, :] = arr

    qi = lax.broadcasted_iota(jnp.int32, (WINDOW, 2 * WINDOW), 0)
    kj = lax.broadcasted_iota(jnp.int32, (WINDOW, 2 * WINDOW), 1)
    dist = qi + WINDOW - kj
    valid = (dist >= 0) & (dist < WINDOW)
    valid0 = valid & (kj >= jnp.where(t == 0, WINDOW, 0))
    distf = dist.astype(F32)
    bias = [jnp.where(valid, (-SLOPES[h] * LOG2E) * distf, NEG_INF) for h in range(N_Q_HEADS)]
    bias0 = [jnp.where(valid0, (-SLOPES[h] * LOG2E) * distf, NEG_INF) for h in range(N_Q_HEADS)]
    lane_lo = lax.broadcasted_iota(jnp.int32, (WINDOW, 2 * HEAD_DIM), 1) < HEAD_DIM

    r512 = lax.broadcasted_iota(jnp.int32, (4 * WINDOW, 2 * HEAD_DIM), 0)
    l512 = lax.broadcasted_iota(jnp.int32, (4 * WINDOW, 2 * HEAD_DIM), 1)
    ones_cols = jnp.where((r512 < 2 * WINDOW) == (l512 < HEAD_DIM), 1.0, 0.0).astype(BF16)
    n_iter = (HR // WINDOW) * 4

    def qk(n):
        j, c = divmod(n, 4)
        hk = c // 2
        rows = slice(j * WINDOW, (j + 1) * WINDOW)
        win = slice(j * WINDOW, (j + 2) * WINDOW)
        qc = (proj_s[rows, 1024 + c * 128:1024 + (c + 1) * 128] * (ATTN_SCALE * LOG2E)).astype(BF16)
        kcat = jnp.concatenate([kv_s[2 * hk, win, :], kv_s[2 * hk + 1, win, :]], axis=0)
        return _dot_nt(qc, kcat)

    def softmax_pv(n, s):
        j, c = divmod(n, 4)
        hk = c // 2
        rows = slice(j * WINDOW, (j + 1) * WINDOW)
        win = slice(j * WINDOW, (j + 2) * WINDOW)
        vcat = jnp.concatenate([kv_s[4 + 2 * hk, win, :], kv_s[5 + 2 * hk, win, :]], axis=0)
        vaug = jnp.concatenate([vcat, ones_cols], axis=1)
        ps, es = [], []
        for half in range(2):
            h = 2 * c + half
            sink = sinks_ref[h] * LOG2E
            sh = s[:, half * 256:(half + 1) * 256] + (bias0[h] if j == 0 else bias[h])
            m = jnp.maximum(jnp.max(sh, axis=-1, keepdims=True), sink)
            ps.append(jnp.exp2(sh - m).astype(BF16))
            es.append(jnp.exp2(sink - m))
        oa = _dot(jnp.concatenate(ps, axis=1), vaug)
        den = oa[:, 128:256] + jnp.where(lane_lo, es[0], es[1])
        ymix_s[rows, D_LRU + c * 128:D_LRU + (c + 1) * 128] = (oa[:, 0:128] * (1.0 / den)).astype(BF16)

    def attention(i):
        depth = 2
        base = i * n_iter
        pend = [qk(base + n) for n in range(depth)]
        for n in range(n_iter):
            s = pend.pop(0)
            if n + depth < n_iter:
                pend.append(qk(base + n + depth))
            softmax_pv(base + n, s)

    def out_proj(i):
        x1_ref[0, sec(i), :] = x_ref[0, sec(i), :] + _dot(ymix_s[sec(i), :], wout_ref[...])

    n_sec = TM // HR
    h = hc_s[...]
    in_proj(0)
    lru_gates(0)
    for i in range(n_sec):
        if i + 1 < n_sec:
            in_proj(i + 1)
        h = lru_scan(i, h)
        kv_prep(i)
        attention(i)
        if i + 1 < n_sec:
            lru_gates(i + 1)
        out_proj(i)

    hc_s[...] = h
    ph_ref[0] = h
    pconv_ref[0] = proj_s[TM - 8:, 0:D_LRU]
    pk_ref[0] = proj_s[TM - WINDOW:, 1536:1664]
    pv_ref[0] = proj_s[TM - WINDOW:, 1664:1792]
    kv_s[:, 0:WINDOW, :] = kv_s[:, TM:TM + WINDOW, :]


def _ffn_chunks(n3, acc, gprev_fn, wg_ref, wu_ref, wfc_ref, bfc_ref, wd_ref, on_gate):
    nc = D_FF // FF_CHUNK

    def up(c):
        cs = slice(c * FF_CHUNK, (c + 1) * FF_CHUNK)
        return _dot(n3, wg_ref[:, cs]), _dot(n3, wu_ref[:, cs])

    nxt = up(0)
    for c in range(nc):
        cs = slice(c * FF_CHUNK, (c + 1) * FF_CHUNK)
        g, u = nxt
        if c + 1 < nc:
            nxt = up(c + 1)
        g2, g1 = gprev_fn(c, g)
        on_gate(c, g)
        wfc = wfc_ref[:, cs]
        conv = bfc_ref[:, cs] + wfc[0:1] * g2 + wfc[1:2] * g1 + wfc[2:3] * g
        hmid = (_gelu(conv) * u).astype(BF16)
        acc = acc + _dot(hmid, wd_ref[cs, :])
    return acc


def _prompt_ffn_kernel(x1_ref, mk_ref, mv_ref, xs1_ref, cmk_ref, cmv_ref, gc_ref, wq_ref, wo_ref, gf_ref,
                       wg_ref, wu_ref, wfc_ref, bfc_ref, wd_ref, gfin_ref,
                       y_ref, pffn_ref, xs2_ref,
                       oc_s, gbuf_s, gcar_s, sq_s, soc_s):
    t = pl.program_id(1)

    @pl.when(t == 0)
    def _():
        gcar_s[...] = jnp.zeros((8, D_FF), F32)

    _sample_cross(xs1_ref, cmk_ref, cmv_ref, gc_ref, wq_ref, wo_ref, xs2_ref, sq_s, soc_s)

    x1 = x1_ref[0]
    qc = _dot(_rmsnorm(x1, gc_ref[...]).astype(BF16), wq_ref[...]).astype(BF16)
    hsl = [slice(h * MEM_HEAD_DIM, (h + 1) * MEM_HEAD_DIM) for h in range(N_MEM_HEADS)]
    ss = [_dot_nt(qc[:, hs], mk_ref[0, :, hs]) for hs in hsl]
    for h, hs in enumerate(hsl):
        s = ss[h]
        m = jnp.max(s, axis=-1, keepdims=True)
        p = jnp.exp2((s - m) * (MEM_SCALE * LOG2E))
        l = jnp.sum(p, axis=-1, keepdims=True)
        o = _dot(p.astype(BF16), mv_ref[0, :, hs]) * (1.0 / l)
        oc_s[:, hs] = o.astype(BF16)
    x2 = x1 + _dot(oc_s[...], wo_ref[...])
    n3 = _rmsnorm(x2, gf_ref[...]).astype(BF16)

    def gprev(c, g):
        cs = slice(c * FF_CHUNK, (c + 1) * FF_CHUNK)
        gbuf_s[0:8, :] = gcar_s[:, cs]
        gbuf_s[8:TF + 8, :] = g
        return gbuf_s[6:6 + TF, :], gbuf_s[7:7 + TF, :]

    def on_gate(c, g):
        cs = slice(c * FF_CHUNK, (c + 1) * FF_CHUNK)
        tail = g[TF - 8:, :]
        gcar_s[:, cs] = tail
        pffn_ref[0, :, cs] = tail

    acc = _ffn_chunks(n3, x2, gprev, wg_ref, wu_ref, wfc_ref, bfc_ref, wd_ref, on_gate)
    y_ref[0] = _rmsnorm(acc, gfin_ref[...])


def _sample_mixer_kernel(sinks_ref, x_ref, prev4_ref, h0_ref, ck_ref, cv_ref, g_ref, win_ref, wconv_ref, bconv_ref,
                         wax_ref, ba_ref, bx_ref, lam_ref, wout_ref, wkt_ref, wvt_ref,
                         x1_ref, sk_ref, sv_ref, xr_ref, h_ref,
                         q_s, yatt_s):
    R = S_ROWS
    x = x_ref[...]
    n1 = _rmsnorm(x, g_ref[...]).astype(BF16)
    proj = _dot(n1, win_ref[...])
    xr = proj[:, 0:D_LRU]
    gate = proj[:, D_LRU:2 * D_LRU]
    q_s[...] = proj[:, 1024:1536] * ATTN_SCALE
    xr_ref[...] = xr

    tmod = lax.broadcasted_iota(jnp.int32, (R, D_LRU), 0) & 3
    prev4 = prev4_ref[...]
    xs1 = jnp.where(tmod >= 1, pltpu.roll(xr, 1, 0), pltpu.roll(prev4, R - 2, 0))
    xs2 = jnp.where(tmod >= 2, pltpu.roll(xr, 2, 0), pltpu.roll(prev4, R - 1, 0))
    xs3 = jnp.where(tmod >= 3, pltpu.roll(xr, 3, 0), prev4)
    wc = wconv_ref[...]
    xc = bconv_ref[...] + wc[0:1] * xs3 + wc[1:2] * xs2 + wc[2:3] * xs1 + wc[3:4] * xr

    a, b = _lru_gates(xc, wax_ref, ba_ref[...], bx_ref[...], lam_ref[...])
    for s in (1, 2):
        m = tmod >= s
        a_sh = pltpu.roll(a, s, 0)
        b_sh = pltpu.roll(b, s, 0)
        b = jnp.where(m, a * b_sh + b, b)
        a = jnp.where(m, a * a_sh, a)
    h = a * h0_ref[...] + b
    h_ref[...] = h
    y_lru = h * _gelu(gate)

    kt_new = _dot_nt(wkt_ref[...], n1)
    vt_new = _dot_nt(wvt_ref[...], n1)
    zpad = jnp.zeros((D_KV, WINDOW - R), F32)
    kt_pad = jnp.concatenate([kt_new, zpad], axis=1)
    vt_pad = jnp.concatenate([vt_new, zpad], axis=1)
    kt_pad_b = kt_pad.astype(BF16)
    vt_pad_b = vt_pad.astype(BF16)

    r64 = lax.broadcasted_iota(jnp.int32, (8 * N_Q_HEADS, WINDOW), 0)
    c64 = lax.broadcasted_iota(jnp.int32, (8 * N_Q_HEADS, WINDOW), 1)
    t64 = r64 & 3
    slope = jnp.zeros((8 * N_Q_HEADS, WINDOW), F32)
    sinkcol = jnp.zeros((8 * N_Q_HEADS, 1), F32)
    for hq in range(N_Q_HEADS):
        slope = jnp.where((r64 >> 3) == hq, SLOPES[hq], slope)
        sinkcol = jnp.where((r64[:, 0:1] >> 3) == hq, sinks_ref[hq], sinkcol)
    bias_c = jnp.where(c64 > t64, -slope * (t64 + WINDOW - c64).astype(F32), NEG_INF)
    bat0 = (r64 & 7) < 4
    lane_lo = lax.broadcasted_iota(jnp.int32, (8, D_KV), 1) < HEAD_DIM
    lane128 = lax.broadcasted_iota(jnp.int32, (D_KV, WINDOW), 1)

    def pair_scores(p):
        r0 = p * 8
        q8 = q_s[r0:r0 + 8, :]
        blocks = []
        for hq in range(N_Q_HEADS):
            chunk = q8[:, (hq // 2) * 128:(hq // 2 + 1) * 128]
            hk = hq // (N_Q_HEADS // N_KV_HEADS)
            src = chunk if (hq % 2) == hk else pltpu.roll(chunk, HEAD_DIM, 1)
            blocks.append(jnp.where(lane_lo == (hk == 0), src, 0.0))
        lhs = jnp.concatenate(blocks, axis=0).astype(BF16)
        sb = []
        for bb in range(2):
            bidx = 2 * p + bb
            kt = ck_ref[bidx]
            vt = cv_ref[bidx]
            sb.append(_dot(lhs, kt.astype(BF16)))
            shift = (WINDOW - 4 - 4 * bidx) % WINDOW
            sk_ref[bidx] = jnp.where(lane128 >= WINDOW - 4, pltpu.roll(kt_pad, shift, 1),
                                     pltpu.roll(kt, WINDOW - 4, 1))
            sv_ref[bidx] = jnp.where(lane128 >= WINDOW - 4, pltpu.roll(vt_pad, shift, 1),
                                     pltpu.roll(vt, WINDOW - 4, 1))
        s_c = jnp.where(bat0, sb[0], sb[1]) + bias_c
        same = (c64 >> 2) == (2 * p + ((r64 & 7) >> 2))
        dn = t64 - (c64 & 3)
        bias_n = jnp.where(same & (dn >= 0), -slope * dn.astype(F32), NEG_INF)
        s_n = _dot(lhs, kt_pad_b) + bias_n
        return s_c, s_n

    def pair_finish(p, s_c, s_n):
        r0 = p * 8
        m = jnp.maximum(jnp.maximum(jnp.max(s_c, axis=-1, keepdims=True), jnp.max(s_n, axis=-1, keepdims=True)),
                        sinkcol)
        pc = jnp.exp(s_c - m)
        pn = jnp.exp(s_n - m)
        l = jnp.sum(pc, axis=-1, keepdims=True) + jnp.sum(pn, axis=-1, keepdims=True) + jnp.exp(sinkcol - m)
        pcb = pc.astype(BF16)
        ob = [_dot_nt(pcb, cv_ref[2 * p + bb].astype(BF16)) for bb in range(2)]
        o = (jnp.where(bat0, ob[0], ob[1]) + _dot_nt(pn.astype(BF16), vt_pad_b)) * (1.0 / l)
        for c in range(N_Q_HEADS // 2):
            hk = c // 2
            ev = o[16 * c:16 * c + 8, :]
            od = o[16 * c + 8:16 * c + 16, :]
            if hk == 0:
                od = pltpu.roll(od, HEAD_DIM, 1)
            else:
                ev = pltpu.roll(ev, HEAD_DIM, 1)
            yatt_s[r0:r0 + 8, c * 128:(c + 1) * 128] = jnp.where(lane_lo, ev, od)

    n_pairs = S_NB // 2
    cur = pair_scores(0)
    for p in range(n_pairs):
        nxt = pair_scores(p + 1) if p + 1 < n_pairs else None
        pair_finish(p, *cur)
        cur = nxt

    ymix = jnp.concatenate([y_lru, yatt_s[...]], axis=1).astype(BF16)
    x1_ref[...] = x + _dot(ymix, wout_ref[...])


def _load_mem_heads(ref, b):
    return jnp.concatenate([ref[b, pl.ds(h, MEM_LEN, stride=N_MEM_HEADS), :] for h in range(N_MEM_HEADS)],
                           axis=1).astype(BF16)


def _sample_cross(x1_ref, mk_ref, mv_ref, gc_ref, wq_ref, wo_ref, x2_ref, q_s, oc_s):
    lane_head = lax.broadcasted_iota(jnp.int32, (8, D_MEM), 1) // MEM_HEAD_DIM
    bat0 = (lax.broadcasted_iota(jnp.int32, (32, D_MEM), 0) & 7) < 4
    bat0_s = (lax.broadcasted_iota(jnp.int32, (32, MEM_LEN), 0) & 7) < 4

    def q_proj():
        q_s[...] = _dot(_rmsnorm(x1_ref[...], gc_ref[...]).astype(BF16), wq_ref[...])

    def scores(p):
        q8 = q_s[p * 8:(p + 1) * 8, :]
        lhs = jnp.concatenate([jnp.where(lane_head == h, q8, 0.0) for h in range(N_MEM_HEADS)],
                              axis=0).astype(BF16)
        sb = [_dot_nt(lhs, _load_mem_heads(mk_ref, 2 * p + bb)) for bb in range(2)]
        return jnp.where(bat0_s, sb[0], sb[1]) * MEM_SCALE

    def attend():
        ss = [scores(p) for p in range(C_NB // 2)]
        for p, s in enumerate(ss):
            finish(p, s)

    def finish(p, s):
        m = jnp.max(s, axis=-1, keepdims=True)
        pe = jnp.exp(s - m)
        l = jnp.sum(pe, axis=-1, keepdims=True)
        pb = pe.astype(BF16)
        ob = [_dot(pb, _load_mem_heads(mv_ref, 2 * p + bb)) for bb in range(2)]
        o = jnp.where(bat0, ob[0], ob[1]) * (1.0 / l)
        out = jnp.where(lane_head == 0, o[0:8], 0.0)
        for h in range(1, N_MEM_HEADS):
            out = out + jnp.where(lane_head == h, o[h * 8:(h + 1) * 8], 0.0)
        oc_s[p * 8:(p + 1) * 8, :] = out

    def out_proj():
        x2_ref[...] = x1_ref[...] + _dot(oc_s[...].astype(BF16), wo_ref[...])

    q_proj()
    attend()
    out_proj()


def _sample_ffn_kernel(x2_ref, prev_ref, gf_ref, wg_ref, wu_ref, wfc_ref, bfc_ref, wd_ref, gfin_ref,
                       y_ref, gate_ref, slab_s, p1_s, p2_s):
    R = x2_ref.shape[0]
    nb = R // 4
    nl = FF_CHUNK // 128
    x2 = x2_ref[...]
    n3 = _rmsnorm(x2, gf_ref[...]).astype(BF16)
    tmod = lax.broadcasted_iota(jnp.int32, (R, FF_CHUNK), 0) & 3
    p1_s[...] = jnp.zeros(p1_s.shape, F32)
    p2_s[...] = jnp.zeros(p2_s.shape, F32)

    def gprev(c, g):
        for l in range(nl):
            cols = slice(c * FF_CHUNK + l * 128, c * FF_CHUNK + (l + 1) * 128)
            s0 = prev_ref[0, :, cols]
            s1 = prev_ref[1, :, cols]
            p1_s[l, pl.ds(0, nb, stride=4), :] = s1
            p2_s[l, pl.ds(0, nb, stride=4), :] = s0
            p2_s[l, pl.ds(1, nb, stride=4), :] = s1
        p1 = jnp.concatenate([p1_s[l] for l in range(nl)], axis=1)
        p2 = jnp.concatenate([p2_s[l] for l in range(nl)], axis=1)
        g1 = jnp.where(tmod >= 1, pltpu.roll(g, 1, 0), p1)
        g2 = jnp.where(tmod >= 2, pltpu.roll(g, 2, 0), p2)
        return g2, g1

    def on_gate(c, g):
        for l in range(nl):
            slab_s[l] = g[:, l * 128:(l + 1) * 128]
        for tt in range(2):
            gate_ref[tt, :, c * FF_CHUNK:(c + 1) * FF_CHUNK] = jnp.concatenate(
                [slab_s[l, pl.ds(2 + tt, nb, stride=4), :] for l in range(nl)], axis=1)

    acc = _ffn_chunks(n3, x2, gprev, wg_ref, wu_ref, wfc_ref, bfc_ref, wd_ref, on_gate)
    y_ref[...] = _rmsnorm(acc, gfin_ref[...])


def _const_spec(shape, grid_rank):
    zeros = (0,) * len(shape)
    if grid_rank == 1:
        return pl.BlockSpec(shape, lambda i: zeros, pipeline_mode=pl.Buffered(1))
    return pl.BlockSpec(shape, lambda i, j: zeros, pipeline_mode=pl.Buffered(1))


def _block_diag4(w):
    eye = jnp.eye(4, dtype=w.dtype)
    return (w[:, :, None, :] * eye[:, None, :, None]).reshape(4 * LRU_BLOCK, 4 * LRU_BLOCK)


def _params(sem):
    return pltpu.CompilerParams(dimension_semantics=sem, vmem_limit_bytes=VMEM_LIMIT)


def kernel(x_prompt, x_sample, cache_swa_k, cache_swa_v, cache_mem_k, cache_mem_v, state_lru_conv, state_lru_h, state_ffn_conv, mem_prompt, g_mix, w_in, w_lru_conv, b_lru_conv, w_lru_a, b_lru_a, w_lru_x, b_lru_x, lru_lambda, attn_sinks, w_out, g_cross, g_mem, w_mem_q, w_mem_k, w_mem_v, w_mem_o, g_ffn, w_ffn_gate, w_ffn_up, w_ffn_conv, b_ffn_conv, w_ffn_down, g_final):
    B, T, _ = x_prompt.shape
    NB = x_sample.shape[0]
    NT = T // TM
    SR = NB * 4

    win = w_in[0].astype(BF16)
    wout = w_out[0].astype(BF16)
    wq = w_mem_q[0].astype(BF16)
    wk = w_mem_k[0].astype(BF16)
    wv = w_mem_v[0].astype(BF16)
    wo = w_mem_o[0].astype(BF16)
    wg = w_ffn_gate[0].astype(BF16)
    wu = w_ffn_up[0].astype(BF16)
    wd = w_ffn_down[0].astype(BF16)
    wax = jnp.stack([
        jnp.concatenate([_block_diag4(w_lru_a[0, 4 * gi:4 * gi + 4]), _block_diag4(w_lru_x[0, 4 * gi:4 * gi + 4])],
                        axis=1) for gi in range(2)]).astype(BF16)
    gmix, gcross, gmem, gffn = g_mix, g_cross, g_mem, g_ffn
    gfin = g_final.reshape(1, D_MODEL)
    wconv, bconv = w_lru_conv[0], b_lru_conv
    ba, bx, lam = b_lru_a, b_lru_x, lru_lambda
    wfc, bfc = w_ffn_conv[0], b_ffn_conv
    sinks = attn_sinks[0]
    smem = pl.BlockSpec(memory_space=pltpu.SMEM)

    mk, mv, mkb, mvb = pl.pallas_call(
        _mem_kv_kernel,
        grid=(B,),
        in_specs=[pl.BlockSpec((1, MEM_LEN, D_MODEL), lambda b: (b, 0, 0)),
                  _const_spec((1, D_MODEL), 1), _const_spec((D_MODEL, D_MEM), 1), _const_spec((D_MODEL, D_MEM), 1)],
        out_specs=[pl.BlockSpec((1, MEM_LEN, D_MEM), lambda b: (b, 0, 0))] * 4,
        out_shape=[jax.ShapeDtypeStruct((B, MEM_LEN, D_MEM), F32)] * 2
        + [jax.ShapeDtypeStruct((B, MEM_LEN, D_MEM), BF16)] * 2,
        compiler_params=_params(("arbitrary",)),
        name="mem_kv",
    )(mem_prompt, gmem, wk, wv)

    x1p, pk, pv, pconv8, ph8 = pl.pallas_call(
        _prompt_mixer_kernel,
        grid=(B, NT),
        in_specs=[smem,
                  pl.BlockSpec((1, TM, D_MODEL), lambda b, t: (b, t, 0)),
                  _const_spec((1, D_MODEL), 2), _const_spec((D_MODEL, D_IN), 2),
                  _const_spec((LRU_CONV_W, D_LRU), 2), _const_spec((1, D_LRU), 2),
                  _const_spec((2, 256, 512), 2), _const_spec((1, D_LRU), 2), _const_spec((1, D_LRU), 2),
                  _const_spec((1, D_LRU), 2), _const_spec((D_MODEL, D_MODEL), 2)],
        out_specs=[pl.BlockSpec((1, TM, D_MODEL), lambda b, t: (b, t, 0)),
                   pl.BlockSpec((1, WINDOW, D_KV), lambda b, t: (b, 0, 0)),
                   pl.BlockSpec((1, WINDOW, D_KV), lambda b, t: (b, 0, 0)),
                   pl.BlockSpec((1, 8, D_LRU), lambda b, t: (b, 0, 0)),
                   pl.BlockSpec((1, 8, D_LRU), lambda b, t: (b, 0, 0))],
        out_shape=[jax.ShapeDtypeStruct((B, T, D_MODEL), F32),
                   jax.ShapeDtypeStruct((B, WINDOW, D_KV), F32),
                   jax.ShapeDtypeStruct((B, WINDOW, D_KV), F32),
                   jax.ShapeDtypeStruct((B, 8, D_LRU), F32),
                   jax.ShapeDtypeStruct((B, 8, D_LRU), F32)],
        scratch_shapes=[pltpu.VMEM((TM, D_IN), F32),
                        pltpu.VMEM((D_LRU // 128, HR, 128), F32),
                        pltpu.VMEM((D_LRU // 128, HR, 128), F32),
                        pltpu.VMEM((24, D_LRU), F32),
                        pltpu.VMEM((TM, D_LRU), F32),
                        pltpu.VMEM((TM, D_LRU), F32),
                        pltpu.VMEM((HR, D_LRU), F32),
                        pltpu.VMEM((TM, D_LRU), F32),
                        pltpu.VMEM((8, D_LRU), F32),
                        pltpu.VMEM((8, TM + WINDOW, D_KV), BF16),
                        pltpu.VMEM((TM, D_MODEL), BF16)],
        compiler_params=_params(("arbitrary", "arbitrary")),
        name="prompt_mixer",
    )(sinks, x_prompt, gmix, win, wconv, bconv, wax, ba, bx, lam, wout)

    xs = x_sample.reshape(SR, D_MODEL)
    conv_prev4 = jnp.pad(state_lru_conv[0], ((0, 0), (0, 1), (0, 0))).reshape(SR, D_LRU)
    h0rep = jnp.repeat(state_lru_h[0], 4, axis=0)
    ck = jnp.transpose(cache_swa_k[0], (0, 2, 3, 1)).reshape(NB, D_KV, WINDOW)
    cv = jnp.transpose(cache_swa_v[0], (0, 2, 3, 1)).reshape(NB, D_KV, WINDOW)
    wkt = jnp.transpose(w_in[0][:, 1536:1664]).astype(BF16)
    wvt = jnp.transpose(w_in[0][:, 1664:1792]).astype(BF16)
    row_spec = lambda w: pl.BlockSpec((S_ROWS, w), lambda i: (i, 0))
    cache_spec = pl.BlockSpec((S_NB, D_KV, WINDOW), lambda i: (i, 0, 0))
    x1s, sk, sv, xr_s, h_s = pl.pallas_call(
        _sample_mixer_kernel,
        grid=(NB // S_NB,),
        in_specs=[smem, row_spec(D_MODEL), row_spec(D_LRU), row_spec(D_LRU), cache_spec, cache_spec,
                  _const_spec((1, D_MODEL), 1), _const_spec((D_MODEL, D_IN), 1),
                  _const_spec((LRU_CONV_W, D_LRU), 1), _const_spec((1, D_LRU), 1),
                  _const_spec((2, 256, 512), 1), _const_spec((1, D_LRU), 1), _const_spec((1, D_LRU), 1),
                  _const_spec((1, D_LRU), 1), _const_spec((D_MODEL, D_MODEL), 1),
                  _const_spec((D_KV, D_MODEL), 1), _const_spec((D_KV, D_MODEL), 1)],
        out_specs=[row_spec(D_MODEL), cache_spec, cache_spec, row_spec(D_LRU), row_spec(D_LRU)],
        out_shape=[jax.ShapeDtypeStruct((SR, D_MODEL), F32),
                   jax.ShapeDtypeStruct((NB, D_KV, WINDOW), F32),
                   jax.ShapeDtypeStruct((NB, D_KV, WINDOW), F32),
                   jax.ShapeDtypeStruct((SR, D_LRU), F32),
                   jax.ShapeDtypeStruct((SR, D_LRU), F32)],
        scratch_shapes=[pltpu.VMEM((S_ROWS, D_ATTN), F32),
                        pltpu.VMEM((S_ROWS, D_ATTN), F32)],
        compiler_params=_params(("arbitrary",)),
        name="sample_mixer",
    )(sinks, xs, conv_prev4, h0rep, ck, cv, gmix, win, wconv, bconv, wax, ba, bx, lam, wout, wkt, wvt)

    cmk = cache_mem_k.reshape(NB, MEM_LEN * N_MEM_HEADS, MEM_HEAD_DIM)
    cmv = cache_mem_v.reshape(NB, MEM_LEN * N_MEM_HEADS, MEM_HEAD_DIM)
    ntf = T // TF
    assert NB == C_NB * B * ntf
    crow = pl.BlockSpec((C_ROWS, D_MODEL), lambda b, t: (b * ntf + t, 0))
    cmem = pl.BlockSpec((C_NB, MEM_LEN * N_MEM_HEADS, MEM_HEAD_DIM), lambda b, t: (b * ntf + t, 0, 0))
    y_prompt, pffn8, x2s = pl.pallas_call(
        _prompt_ffn_kernel,
        grid=(B, ntf),
        in_specs=[pl.BlockSpec((1, TF, D_MODEL), lambda b, t: (b, t, 0)),
                  pl.BlockSpec((1, MEM_LEN, D_MEM), lambda b, t: (b, 0, 0)),
                  pl.BlockSpec((1, MEM_LEN, D_MEM), lambda b, t: (b, 0, 0)),
                  crow, cmem, cmem,
                  _const_spec((1, D_MODEL), 2), _const_spec((D_MODEL, D_MEM), 2), _const_spec((D_MEM, D_MODEL), 2),
                  _const_spec((1, D_MODEL), 2), _const_spec((D_MODEL, D_FF), 2), _const_spec((D_MODEL, D_FF), 2),
                  _const_spec((FFN_CONV_W, D_FF), 2), _const_spec((1, D_FF), 2), _const_spec((D_FF, D_MODEL), 2),
                  _const_spec((1, D_MODEL), 2)],
        out_specs=[pl.BlockSpec((1, TF, D_MODEL), lambda b, t: (b, t, 0)),
                   pl.BlockSpec((1, 8, D_FF), lambda b, t: (b, 0, 0)),
                   crow],
        out_shape=[jax.ShapeDtypeStruct((B, T, D_MODEL), F32),
                   jax.ShapeDtypeStruct((B, 8, D_FF), F32),
                   jax.ShapeDtypeStruct((SR, D_MODEL), F32)],
        scratch_shapes=[pltpu.VMEM((TF, D_MEM), BF16),
                        pltpu.VMEM((TF + 8, FF_CHUNK), F32),
                        pltpu.VMEM((8, D_FF), F32),
                        pltpu.VMEM((C_ROWS, D_MEM), F32),
                        pltpu.VMEM((C_ROWS, D_MEM), F32)],
        compiler_params=_params(("arbitrary", "arbitrary")),
        name="prompt_ffn",
    )(x1p, mkb, mvb, x1s, cmk, cmv, gcross, wq, wo, gffn, wg, wu, wfc, bfc, wd, gfin)

    ffn_prev_tm = jnp.transpose(state_ffn_conv[0], (1, 0, 2))
    slab = pltpu.VMEM((FF_CHUNK // 128, SR, 128), F32)
    y_s, gate_tm = pl.pallas_call(
        _sample_ffn_kernel,
        grid=(1,),
        in_specs=[_const_spec((SR, D_MODEL), 1), _const_spec((FFN_CONV_W - 1, NB, D_FF), 1),
                  _const_spec((1, D_MODEL), 1), _const_spec((D_MODEL, D_FF), 1), _const_spec((D_MODEL, D_FF), 1),
                  _const_spec((FFN_CONV_W, D_FF), 1), _const_spec((1, D_FF), 1), _const_spec((D_FF, D_MODEL), 1),
                  _const_spec((1, D_MODEL), 1)],
        out_specs=[pl.BlockSpec((SR, D_MODEL), lambda i: (0, 0)),
                   pl.BlockSpec((FFN_CONV_W - 1, NB, D_FF), lambda i: (0, 0, 0))],
        out_shape=[jax.ShapeDtypeStruct((SR, D_MODEL), F32),
                   jax.ShapeDtypeStruct((FFN_CONV_W - 1, NB, D_FF), F32)],
        scratch_shapes=[slab, slab, slab],
        compiler_params=_params(("arbitrary",)),
        name="sample_ffn",
    )(x2s, ffn_prev_tm, gffn, wg, wu, wfc, bfc, wd, gfin)

    p_swa_k = pk.reshape(1, B, WINDOW, N_KV_HEADS, HEAD_DIM)
    p_swa_v = pv.reshape(1, B, WINDOW, N_KV_HEADS, HEAD_DIM)
    p_mem_k = mk.reshape(1, B, MEM_LEN, N_MEM_HEADS, MEM_HEAD_DIM)
    p_mem_v = mv.reshape(1, B, MEM_LEN, N_MEM_HEADS, MEM_HEAD_DIM)
    p_lru_conv = pconv8[None, :, 8 - (LRU_CONV_W - 1):, :]
    p_lru_h = ph8[None, :, 0, :]
    p_ffn_conv = pffn8[None, :, 8 - (FFN_CONV_W - 1):, :]
    y_sample = y_s.reshape(NB, 4, D_MODEL)
    s_swa_k = jnp.transpose(sk.reshape(NB, N_KV_HEADS, HEAD_DIM, WINDOW), (0, 3, 1, 2))[None]
    s_swa_v = jnp.transpose(sv.reshape(NB, N_KV_HEADS, HEAD_DIM, WINDOW), (0, 3, 1, 2))[None]
    s_lru_conv = xr_s.reshape(NB, 4, D_LRU)[None, :, 1:, :]
    s_lru_h = h_s.reshape(NB, 4, D_LRU)[None, :, 3, :]
    s_ffn_conv = jnp.transpose(gate_tm, (1, 0, 2))[None]
    return (y_prompt, y_sample, p_swa_k, p_swa_v, p_mem_k, p_mem_v, p_lru_conv, p_lru_h, p_ffn_conv,
            s_swa_k, s_swa_v, s_lru_conv, s_lru_h, s_ffn_conv)
```

```python
import functools

import jax
import jax.numpy as jnp
from jax import lax
from jax.experimental import pallas as pl
from jax.experimental.pallas import tpu as pltpu

D_MODEL = 1024
D_LRU = 512
LRU_BLOCK = 64
LRU_CONV_W = 4
LRU_C = 8.0
N_Q_HEADS = 8
N_KV_HEADS = 2
HEAD_DIM = 64
D_ATTN = 512
D_KV = 128
WINDOW = 128
D_IN = 1792
MEM_LEN = 256
N_MEM_HEADS = 4
MEM_HEAD_DIM = 128
D_MEM = 512
D_FF = 3072
FFN_CONV_W = 3
EPS = 1e-6
NEG_INF = -1e30

F32 = jnp.float32
BF16 = jnp.bfloat16

SLOPES = [float(2.0 ** (-8.0 * (i + 1) / N_Q_HEADS)) for i in range(N_Q_HEADS)]
ATTN_SCALE = HEAD_DIM ** -0.5
MEM_SCALE = MEM_HEAD_DIM ** -0.5
LOG2E = 1.4426950408889634
F32_TINY = 1.1754944e-38

TM = 1024
HR = 512
SEG = HR // 8
TF = 512
FF_CHUNK = 1024
S_NB = 32
S_ROWS = 4 * S_NB
C_NB = 4
C_ROWS = 4 * C_NB
VMEM_LIMIT = 56 * 1024 * 1024


def _dot(a, b):
    return jnp.dot(a, b, preferred_element_type=F32)


def _dot_nt(a, b):
    return lax.dot_general(a, b, (((1,), (1,)), ((), ())), preferred_element_type=F32)


def _rmsnorm(x, g):
    ms = jnp.mean(x * x, axis=-1, keepdims=True)
    return x * lax.rsqrt(ms + EPS) * g


def _gelu(x):
    c = 0.7978845608028654
    return x * (0.5 * (1.0 + jnp.tanh(c * (x + 0.044715 * (x * x * x)))))


def _sigmoid(x):
    return 1.0 / (1.0 + jnp.exp(-x))


def _softplus(x):
    return jnp.maximum(x, 0.0) + jnp.log1p(jnp.exp(-jnp.abs(x)))


def _lru_gates(xc, wax_ref, ba, bx, lam):
    xcb = xc.astype(BF16)
    pa, px = [], []
    for gi in range(2):
        pre = _dot(xcb[:, gi * 256:(gi + 1) * 256], wax_ref[gi])
        pa.append(pre[:, :256])
        px.append(pre[:, 256:])
    r = _sigmoid(jnp.concatenate(pa, axis=1) + ba)
    i = _sigmoid(jnp.concatenate(px, axis=1) + bx)
    log_a = (-LRU_C * _softplus(-lam)) * r
    a = jnp.exp(log_a)
    om = -jnp.tanh(log_a) * (a * a + 1.0)
    b = (om * lax.rsqrt(jnp.maximum(om, F32_TINY))) * (i * xc)
    return a, b


def _head_variants(t):
    lo = lax.broadcasted_iota(jnp.int32, t.shape, 1) < HEAD_DIM
    tr = pltpu.roll(t, HEAD_DIM, 1)
    z = jnp.zeros_like(t)
    x0 = jnp.where(lo, t, z)
    y0 = jnp.where(lo, z, tr)
    x1 = jnp.where(lo, tr, z)
    y1 = jnp.where(lo, z, t)
    return [v.astype(BF16) for v in (x0, y0, x1, y1)]


def _mem_kv_kernel(mem_ref, g_ref, wk_ref, wv_ref, mk_ref, mv_ref, mkb_ref, mvb_ref):
    n = _rmsnorm(mem_ref[0], g_ref[...]).astype(BF16)
    mk = _dot(n, wk_ref[...])
    mv = _dot(n, wv_ref[...])
    mk_ref[0] = mk
    mv_ref[0] = mv
    mkb_ref[0] = mk.astype(BF16)
    mvb_ref[0] = mv.astype(BF16)


def _prompt_mixer_kernel(sinks_ref, x_ref, g_ref, win_ref, wconv_ref, bconv_ref, wax_ref, ba_ref, bx_ref,
                         lam_ref, wout_ref,
                         x1_ref, pk_ref, pv_ref, pconv_ref, ph_ref,
                         proj_s, pin_s, pout_s, xtail_s, a_s, b_s, hl_s, h_s, hc_s, kv_s, ymix_s):
    t = pl.program_id(1)

    @pl.when(t == 0)
    def _():
        xtail_s[...] = jnp.zeros((24, D_LRU), F32)
        hc_s[...] = jnp.zeros((8, D_LRU), F32)
        kv_s[:, 0:WINDOW, :] = jnp.zeros((8, WINDOW, D_KV), BF16)

    def sec(i):
        return slice(i * HR, (i + 1) * HR)

    def in_proj(i):
        n1 = _rmsnorm(x_ref[0, sec(i), :], g_ref[...]).astype(BF16)
        proj_s[sec(i), :] = _dot(n1, win_ref[...])

    def seg_rows(j):
        return pl.ds(64 * (j % (SEG // 8)) + j // (SEG // 8), 8, stride=8)

    row8 = lax.broadcasted_iota(jnp.int32, (8, D_LRU), 0)

    def lru_gates(i):
        r0 = i * HR
        for j in range(HR // 8):
            for l in range(D_LRU // 128):
                pin_s[l, seg_rows(j), :] = proj_s[r0 + 8 * j:r0 + 8 * j + 8, l * 128:(l + 1) * 128]
        xs = jnp.concatenate([pin_s[l] for l in range(D_LRU // 128)], axis=1)
        tail = xs[HR - 24:, :]
        prev = xtail_s[...]
        xtail_s[...] = tail
        heads = [jnp.where(row8 == 0, pltpu.roll(prev[8 * q:8 * q + 8, :], 1, 0),
                           pltpu.roll(tail[8 * q:8 * q + 8, :], 1, 0)) for q in range(3)]
        sh1 = jnp.concatenate(heads[2:] + [xs[:HR - 8, :]], axis=0)
        sh2 = jnp.concatenate(heads[1:] + [xs[:HR - 16, :]], axis=0)
        sh3 = jnp.concatenate(heads + [xs[:HR - 24, :]], axis=0)
        wc = wconv_ref[...]
        xc = bconv_ref[...] + wc[0:1] * sh3 + wc[1:2] * sh2 + wc[2:3] * sh1 + wc[3:4] * xs
        a, b = _lru_gates(xc, wax_ref, ba_ref[...], bx_ref[...], lam_ref[...])
        a_s[sec(i), :] = a
        b_s[sec(i), :] = b

    def lru_scan(i, hcar):
        r0 = i * HR
        hl = b_s[r0:r0 + 8, :]
        ac = a_s[r0:r0 + 8, :]
        hl_s[0:8, :] = hl
        for g in range(1, SEG):
            av = a_s[r0 + 8 * g:r0 + 8 * g + 8, :]
            hl = av * hl + b_s[r0 + 8 * g:r0 + 8 * g + 8, :]
            ac = av * ac
            hl_s[8 * g:8 * g + 8, :] = hl
            a_s[r0 + 8 * g:r0 + 8 * g + 8, :] = ac
        hin = hcar
        for s in range(8):
            hend = hl + ac * hin
            if s < 7:
                hin = jnp.where(row8 == s + 1, pltpu.roll(hend, 1, 0), hin)
        hcar = jnp.broadcast_to(hend[7:8, :], (8, D_LRU))
        for g in range(SEG):
            hg = hl_s[8 * g:8 * g + 8, :] + a_s[r0 + 8 * g:r0 + 8 * g + 8, :] * hin
            for l in range(D_LRU // 128):
                pout_s[l, 8 * g:8 * g + 8, :] = hg[:, l * 128:(l + 1) * 128]
        for j in range(HR // 8):
            h_s[r0 + 8 * j:r0 + 8 * j + 8, :] = jnp.concatenate(
                [pout_s[l, seg_rows(j), :] for l in range(D_LRU // 128)], axis=1)
        gate = proj_s[sec(i), D_LRU:2 * D_LRU]
        ymix_s[sec(i), 0:D_LRU] = (h_s[sec(i), :] * _gelu(gate)).astype(BF16)
        return hcar

    def kv_prep(i):
        k = proj_s[sec(i), 1536:1664]
        v = proj_s[sec(i), 1664:1792]
        for n, arr in enumerate(_head_variants(k) + _head_variants(v)):
            kv_s[n, WINDOW + i * HR:WINDOW + (i + 1) * HR, :] = arr

    qi = lax.broadcasted_iota(jnp.int32, (WINDOW, 2 * WINDOW), 0)
    kj = lax.broadcasted_iota(jnp.int32, (WINDOW, 2 * WINDOW), 1)
    dist = qi + WINDOW - kj
    valid = (dist >= 0) & (dist < WINDOW)
    valid0 = valid & (kj >= jnp.where(t == 0, WINDOW, 0))
    distf = dist.astype(F32)
    bias = [jnp.where(valid, (-SLOPES[h] * LOG2E) * distf, NEG_INF) for h in range(N_Q_HEADS)]
    bias0 = [jnp.where(valid0, (-SLOPES[h] * LOG2E) * distf, NEG_INF) for h in range(N_Q_HEADS)]
    lane_lo = lax.broadcasted_iota(jnp.int32, (WINDOW, 2 * HEAD_DIM), 1) < HEAD_DIM

    r512 = lax.broadcasted_iota(jnp.int32, (4 * WINDOW, 2 * HEAD_DIM), 0)
    l512 = lax.broadcasted_iota(jnp.int32, (4 * WINDOW, 2 * HEAD_DIM), 1)
    ones_cols = jnp.where((r512 < 2 * WINDOW) == (l512 < HEAD_DIM), 1.0, 0.0).astype(BF16)
    n_iter = (HR // WINDOW) * 4

    def qk(n):
        j, c = divmod(n, 4)
        hk = c // 2
        rows = slice(j * WINDOW, (j + 1) * WINDOW)
        win = slice(j * WINDOW, (j + 2) * WINDOW)
        qc = (proj_s[rows, 1024 + c * 128:1024 + (c + 1) * 128] * (ATTN_SCALE * LOG2E)).astype(BF16)
        kcat = jnp.concatenate([kv_s[2 * hk, win, :], kv_s[2 * hk + 1, win, :]], axis=0)
        return _dot_nt(qc, kcat)

    def softmax_pv(n, s):
        j, c = divmod(n, 4)
        hk = c // 2
        rows = slice(j * WINDOW, (j + 1) * WINDOW)
        win = slice(j * WINDOW, (j + 2) * WINDOW)
        vcat = jnp.concatenate([kv_s[4 + 2 * hk, win, :], kv_s[5 + 2 * hk, win, :]], axis=0)
        vaug = jnp.concatenate([vcat, ones_cols], axis=1)
        ps, es = [], []
        for half in range(2):
            h = 2 * c + half
            sink = sinks_ref[h] * LOG2E
            sh = s[:, half * 256:(half + 1) * 256] + (bias0[h] if j == 0 else bias[h])
            m = jnp.maximum(jnp.max(sh, axis=-1, keepdims=True), sink)
            ps.append(jnp.exp2(sh - m).astype(BF16))
            es.append(jnp.exp2(sink - m))
        oa = _dot(jnp.concatenate(ps, axis=1), vaug)
        den = oa[:, 128:256] + jnp.where(lane_lo, es[0], es[1])
        ymix_s[rows, D_LRU + c * 128:D_LRU + (c + 1) * 128] = (oa[:, 0:128] * (1.0 / den)).astype(BF16)

    def attention(i):
        depth = 2
        base = i * n_iter
        pend = [qk(base + n) for n in range(depth)]
        for n in range(n_iter):
            s = pend.pop(0)
            if n + depth < n_iter:
                pend.append(qk(base + n + depth))
            softmax_pv(base + n, s)

    def out_proj(i):
        x1_ref[0, sec(i), :] = x_ref[0, sec(i), :] + _dot(ymix_s[sec(i), :], wout_ref[...])

    n_sec = TM // HR
    h = hc_s[...]
    in_proj(0)
    lru_gates(0)
    for i in range(n_sec):
        if i + 1 < n_sec:
            in_proj(i + 1)
        h = lru_scan(i, h)
        kv_prep(i)
        attention(i)
        if i + 1 < n_sec:
            lru_gates(i + 1)
        out_proj(i)

    hc_s[...] = h
    ph_ref[0] = h
    pconv_ref[0] = proj_s[TM - 8:, 0:D_LRU]
    pk_ref[0] = proj_s[TM - WINDOW:, 1536:1664]
    pv_ref[0] = proj_s[TM - WINDOW:, 1664:1792]
    kv_s[:, 0:WINDOW, :] = kv_s[:, TM:TM + WINDOW, :]


def _ffn_chunks(n3, acc, gprev_fn, wg_ref, wu_ref, wfc_ref, bfc_ref, wd_ref, on_gate):
    nc = D_FF // FF_CHUNK

    def up(c):
        cs = slice(c * FF_CHUNK, (c + 1) * FF_CHUNK)
        return _dot(n3, wg_ref[:, cs]), _dot(n3, wu_ref[:, cs])

    nxt = up(0)
    for c in range(nc):
        cs = slice(c * FF_CHUNK, (c + 1) * FF_CHUNK)
        g, u = nxt
        if c + 1 < nc:
            nxt = up(c + 1)
        g2, g1 = gprev_fn(c, g)
        on_gate(c, g)
        wfc = wfc_ref[:, cs]
        conv = bfc_ref[:, cs] + wfc[0:1] * g2 + wfc[1:2] * g1 + wfc[2:3] * g
        hmid = (_gelu(conv) * u).astype(BF16)
        acc = acc + _dot(hmid, wd_ref[cs, :])
    return acc


def _prompt_ffn_kernel(x1_ref, mk_ref, mv_ref, xs1_ref, cmk_ref, cmv_ref, gc_ref, wq_ref, wo_ref, gf_ref,
                       wg_ref, wu_ref, wfc_ref, bfc_ref, wd_ref, gfin_ref,
                       y_ref, pffn_ref, xs2_ref,
                       oc_s, gbuf_s, gcar_s, sq_s, soc_s):
    t = pl.program_id(1)

    @pl.when(t == 0)
    def _():
        gcar_s[...] = jnp.zeros((8, D_FF), F32)

    _sample_cross(xs1_ref, cmk_ref, cmv_ref, gc_ref, wq_ref, wo_ref, xs2_ref, sq_s, soc_s)

    x1 = x1_ref[0]
    qc = _dot(_rmsnorm(x1, gc_ref[...]).astype(BF16), wq_ref[...]).astype(BF16)
    hsl = [slice(h * MEM_HEAD_DIM, (h + 1) * MEM_HEAD_DIM) for h in range(N_MEM_HEADS)]
    ss = [_dot_nt(qc[:, hs], mk_ref[0, :, hs]) for hs in hsl]
    for h, hs in enumerate(hsl):
        s = ss[h]
        m = jnp.max(s, axis=-1, keepdims=True)
        p = jnp.exp2((s - m) * (MEM_SCALE * LOG2E))
        l = jnp.sum(p, axis=-1, keepdims=True)
        o = _dot(p.astype(BF16), mv_ref[0, :, hs]) * (1.0 / l)
        oc_s[:, hs] = o.astype(BF16)
    x2 = x1 + _dot(oc_s[...], wo_ref[...])
    n3 = _rmsnorm(x2, gf_ref[...]).astype(BF16)

    def gprev(c, g):
        cs = slice(c * FF_CHUNK, (c + 1) * FF_CHUNK)
        gbuf_s[0:8, :] = gcar_s[:, cs]
        gbuf_s[8:TF + 8, :] = g
        return gbuf_s[6:6 + TF, :], gbuf_s[7:7 + TF, :]

    def on_gate(c, g):
        cs = slice(c * FF_CHUNK, (c + 1) * FF_CHUNK)
        tail = g[TF - 8:, :]
        gcar_s[:, cs] = tail
        pffn_ref[0, :, cs] = tail

    acc = _ffn_chunks(n3, x2, gprev, wg_ref, wu_ref, wfc_ref, bfc_ref, wd_ref, on_gate)
    y_ref[0] = _rmsnorm(acc, gfin_ref[...])


def _sample_mixer_kernel(sinks_ref, x_ref, prev4_ref, h0_ref, ck_ref, cv_ref, g_ref, win_ref, wconv_ref, bconv_ref,
                         wax_ref, ba_ref, bx_ref, lam_ref, wout_ref, wkt_ref, wvt_ref,
                         x1_ref, sk_ref, sv_ref, xr_ref, h_ref,
                         q_s, yatt_s):
    R = S_ROWS
    x = x_ref[...]
    n1 = _rmsnorm(x, g_ref[...]).astype(BF16)
    proj = _dot(n1, win_ref[...])
    xr = proj[:, 0:D_LRU]
    gate = proj[:, D_LRU:2 * D_LRU]
    q_s[...] = proj[:, 1024:1536] * ATTN_SCALE
    xr_ref[...] = xr

    tmod = lax.broadcasted_iota(jnp.int32, (R, D_LRU), 0) & 3
    prev4 = prev4_ref[...]
    xs1 = jnp.where(tmod >= 1, pltpu.roll(xr, 1, 0), pltpu.roll(prev4, R - 2, 0))
    xs2 = jnp.where(tmod >= 2, pltpu.roll(xr, 2, 0), pltpu.roll(prev4, R - 1, 0))
    xs3 = jnp.where(tmod >= 3, pltpu.roll(xr, 3, 0), prev4)
    wc = wconv_ref[...]
    xc = bconv_ref[...] + wc[0:1] * xs3 + wc[1:2] * xs2 + wc[2:3] * xs1 + wc[3:4] * xr

    a, b = _lru_gates(xc, wax_ref, ba_ref[...], bx_ref[...], lam_ref[...])
    for s in (1, 2):
        m = tmod >= s
        a_sh = pltpu.roll(a, s, 0)
        b_sh = pltpu.roll(b, s, 0)
        b = jnp.where(m, a * b_sh + b, b)
        a = jnp.where(m, a * a_sh, a)
    h = a * h0_ref[...] + b
    h_ref[...] = h
    y_lru = h * _gelu(gate)

    kt_new = _dot_nt(wkt_ref[...], n1)
    vt_new = _dot_nt(wvt_ref[...], n1)
    kt_pad, vt_pad = kt_new, vt_new
    if R < WINDOW:
        zpad = jnp.zeros((D_KV, WINDOW - R), F32)
        kt_pad = jnp.concatenate([kt_new, zpad], axis=1)
        vt_pad = jnp.concatenate([vt_new, zpad], axis=1)
    kt_pad_b = kt_pad.astype(BF16)
    vt_pad_b = vt_pad.astype(BF16)

    r64 = lax.broadcasted_iota(jnp.int32, (8 * N_Q_HEADS, WINDOW), 0)
    c64 = lax.broadcasted_iota(jnp.int32, (8 * N_Q_HEADS, WINDOW), 1)
    t64 = r64 & 3
    slope = jnp.zeros((8 * N_Q_HEADS, WINDOW), F32)
    sinkcol = jnp.zeros((8 * N_Q_HEADS, 1), F32)
    for hq in range(N_Q_HEADS):
        slope = jnp.where((r64 >> 3) == hq, SLOPES[hq], slope)
        sinkcol = jnp.where((r64[:, 0:1] >> 3) == hq, sinks_ref[hq], sinkcol)
    bias_c = jnp.where(c64 > t64, -slope * (t64 + WINDOW - c64).astype(F32), NEG_INF)
    bat0 = (r64 & 7) < 4
    lane_lo = lax.broadcasted_iota(jnp.int32, (8, D_KV), 1) < HEAD_DIM
    lane128 = lax.broadcasted_iota(jnp.int32, (D_KV, WINDOW), 1)

    def pair_scores(p):
        r0 = p * 8
        q8 = q_s[r0:r0 + 8, :]
        blocks = []
        for hq in range(N_Q_HEADS):
            chunk = q8[:, (hq // 2) * 128:(hq // 2 + 1) * 128]
            hk = hq // (N_Q_HEADS // N_KV_HEADS)
            src = chunk if (hq % 2) == hk else pltpu.roll(chunk, HEAD_DIM, 1)
            blocks.append(jnp.where(lane_lo == (hk == 0), src, 0.0))
        lhs = jnp.concatenate(blocks, axis=0).astype(BF16)
        sb = []
        for bb in range(2):
            bidx = 2 * p + bb
            kt = ck_ref[bidx]
            vt = cv_ref[bidx]
            sb.append(_dot(lhs, kt.astype(BF16)))
            shift = (WINDOW - 4 - 4 * bidx) % WINDOW
            sk_ref[bidx] = jnp.where(lane128 >= WINDOW - 4, pltpu.roll(kt_pad, shift, 1),
                                     pltpu.roll(kt, WINDOW - 4, 1))
            sv_ref[bidx] = jnp.where(lane128 >= WINDOW - 4, pltpu.roll(vt_pad, shift, 1),
                                     pltpu.roll(vt, WINDOW - 4, 1))
        s_c = jnp.where(bat0, sb[0], sb[1]) + bias_c
        same = (c64 >> 2) == (2 * p + ((r64 & 7) >> 2))
        dn = t64 - (c64 & 3)
        bias_n = jnp.where(same & (dn >= 0), -slope * dn.astype(F32), NEG_INF)
        s_n = _dot(lhs, kt_pad_b) + bias_n
        return s_c, s_n

    def pair_finish(p, s_c, s_n):
        r0 = p * 8
        m = jnp.maximum(jnp.maximum(jnp.max(s_c, axis=-1, keepdims=True), jnp.max(s_n, axis=-1, keepdims=True)),
                        sinkcol)
        pc = jnp.exp(s_c - m)
        pn = jnp.exp(s_n - m)
        l = jnp.sum(pc, axis=-1, keepdims=True) + jnp.sum(pn, axis=-1, keepdims=True) + jnp.exp(sinkcol - m)
        pcb = pc.astype(BF16)
        ob = [_dot_nt(pcb, cv_ref[2 * p + bb].astype(BF16)) for bb in range(2)]
        o = (jnp.where(bat0, ob[0], ob[1]) + _dot_nt(pn.astype(BF16), vt_pad_b)) * (1.0 / l)
        for c in range(N_Q_HEADS // 2):
            hk = c // 2
            ev = o[16 * c:16 * c + 8, :]
            od = o[16 * c + 8:16 * c + 16, :]
            if hk == 0:
                od = pltpu.roll(od, HEAD_DIM, 1)
            else:
                ev = pltpu.roll(ev, HEAD_DIM, 1)
            yatt_s[r0:r0 + 8, c * 128:(c + 1) * 128] = jnp.where(lane_lo, ev, od)

    n_pairs = S_NB // 2
    cur = pair_scores(0)
    for p in range(n_pairs):
        nxt = pair_scores(p + 1) if p + 1 < n_pairs else None
        pair_finish(p, *cur)
        cur = nxt

    ymix = jnp.concatenate([y_lru, yatt_s[...]], axis=1).astype(BF16)
    x1_ref[...] = x + _dot(ymix, wout_ref[...])


def _load_mem_heads(ref, b):
    return jnp.concatenate([ref[b, pl.ds(h, MEM_LEN, stride=N_MEM_HEADS), :] for h in range(N_MEM_HEADS)],
                           axis=1).astype(BF16)


def _sample_cross(x1_ref, mk_ref, mv_ref, gc_ref, wq_ref, wo_ref, x2_ref, q_s, oc_s):
    lane_head = lax.broadcasted_iota(jnp.int32, (8, D_MEM), 1) // MEM_HEAD_DIM
    bat0 = (lax.broadcasted_iota(jnp.int32, (32, D_MEM), 0) & 7) < 4
    bat0_s = (lax.broadcasted_iota(jnp.int32, (32, MEM_LEN), 0) & 7) < 4

    def q_proj():
        q_s[...] = _dot(_rmsnorm(x1_ref[...], gc_ref[...]).astype(BF16), wq_ref[...])

    def scores(p):
        q8 = q_s[p * 8:(p + 1) * 8, :]
        lhs = jnp.concatenate([jnp.where(lane_head == h, q8, 0.0) for h in range(N_MEM_HEADS)],
                              axis=0).astype(BF16)
        sb = [_dot_nt(lhs, _load_mem_heads(mk_ref, 2 * p + bb)) for bb in range(2)]
        return jnp.where(bat0_s, sb[0], sb[1]) * MEM_SCALE

    def attend():
        ss = [scores(p) for p in range(C_NB // 2)]
        for p, s in enumerate(ss):
            finish(p, s)

    def finish(p, s):
        m = jnp.max(s, axis=-1, keepdims=True)
        pe = jnp.exp(s - m)
        l = jnp.sum(pe, axis=-1, keepdims=True)
        pb = pe.astype(BF16)
        ob = [_dot(pb, _load_mem_heads(mv_ref, 2 * p + bb)) for bb in range(2)]
        o = jnp.where(bat0, ob[0], ob[1]) * (1.0 / l)
        out = jnp.where(lane_head == 0, o[0:8], 0.0)
        for h in range(1, N_MEM_HEADS):
            out = out + jnp.where(lane_head == h, o[h * 8:(h + 1) * 8], 0.0)
        oc_s[p * 8:(p + 1) * 8, :] = out

    def out_proj():
        x2_ref[...] = x1_ref[...] + _dot(oc_s[...].astype(BF16), wo_ref[...])

    q_proj()
    attend()
    out_proj()


def _sample_ffn_kernel(x2_ref, prev_ref, gf_ref, wg_ref, wu_ref, wfc_ref, bfc_ref, wd_ref, gfin_ref,
                       y_ref, gate_ref, slab_s, p1_s, p2_s):
    R = x2_ref.shape[0]
    nb = R // 4
    nl = FF_CHUNK // 128
    x2 = x2_ref[...]
    n3 = _rmsnorm(x2, gf_ref[...]).astype(BF16)
    tmod = lax.broadcasted_iota(jnp.int32, (R, FF_CHUNK), 0) & 3
    p1_s[...] = jnp.zeros(p1_s.shape, F32)
    p2_s[...] = jnp.zeros(p2_s.shape, F32)

    def gprev(c, g):
        for l in range(nl):
            cols = slice(c * FF_CHUNK + l * 128, c * FF_CHUNK + (l + 1) * 128)
            s0 = prev_ref[0, :, cols]
            s1 = prev_ref[1, :, cols]
            p1_s[l, pl.ds(0, nb, stride=4), :] = s1
            p2_s[l, pl.ds(0, nb, stride=4), :] = s0
            p2_s[l, pl.ds(1, nb, stride=4), :] = s1
        p1 = jnp.concatenate([p1_s[l] for l in range(nl)], axis=1)
        p2 = jnp.concatenate([p2_s[l] for l in range(nl)], axis=1)
        g1 = jnp.where(tmod >= 1, pltpu.roll(g, 1, 0), p1)
        g2 = jnp.where(tmod >= 2, pltpu.roll(g, 2, 0), p2)
        return g2, g1

    def on_gate(c, g):
        for l in range(nl):
            slab_s[l] = g[:, l * 128:(l + 1) * 128]
        for tt in range(2):
            gate_ref[tt, :, c * FF_CHUNK:(c + 1) * FF_CHUNK] = jnp.concatenate(
                [slab_s[l, pl.ds(2 + tt, nb, stride=4), :] for l in range(nl)], axis=1)

    acc = _ffn_chunks(n3, x2, gprev, wg_ref, wu_ref, wfc_ref, bfc_ref, wd_ref, on_gate)
    y_ref[...] = _rmsnorm(acc, gfin_ref[...])


def _const_spec(shape, grid_rank):
    zeros = (0,) * len(shape)
    if grid_rank == 1:
        return pl.BlockSpec(shape, lambda i: zeros, pipeline_mode=pl.Buffered(1))
    return pl.BlockSpec(shape, lambda i, j: zeros, pipeline_mode=pl.Buffered(1))


def _block_diag4(w):
    eye = jnp.eye(4, dtype=w.dtype)
    return (w[:, :, None, :] * eye[:, None, :, None]).reshape(4 * LRU_BLOCK, 4 * LRU_BLOCK)


def _params(sem):
    return pltpu.CompilerParams(dimension_semantics=sem, vmem_limit_bytes=VMEM_LIMIT)


def kernel(x_prompt, x_sample, cache_swa_k, cache_swa_v, cache_mem_k, cache_mem_v, state_lru_conv, state_lru_h, state_ffn_conv, mem_prompt, g_mix, w_in, w_lru_conv, b_lru_conv, w_lru_a, b_lru_a, w_lru_x, b_lru_x, lru_lambda, attn_sinks, w_out, g_cross, g_mem, w_mem_q, w_mem_k, w_mem_v, w_mem_o, g_ffn, w_ffn_gate, w_ffn_up, w_ffn_conv, b_ffn_conv, w_ffn_down, g_final):
    B, T, _ = x_prompt.shape
    NB = x_sample.shape[0]
    NT = T // TM
    SR = NB * 4

    win = w_in[0].astype(BF16)
    wout = w_out[0].astype(BF16)
    wq = w_mem_q[0].astype(BF16)
    wk = w_mem_k[0].astype(BF16)
    wv = w_mem_v[0].astype(BF16)
    wo = w_mem_o[0].astype(BF16)
    wg = w_ffn_gate[0].astype(BF16)
    wu = w_ffn_up[0].astype(BF16)
    wd = w_ffn_down[0].astype(BF16)
    wax = jnp.stack([
        jnp.concatenate([_block_diag4(w_lru_a[0, 4 * gi:4 * gi + 4]), _block_diag4(w_lru_x[0, 4 * gi:4 * gi + 4])],
                        axis=1) for gi in range(2)]).astype(BF16)
    gmix, gcross, gmem, gffn = g_mix, g_cross, g_mem, g_ffn
    gfin = g_final.reshape(1, D_MODEL)
    wconv, bconv = w_lru_conv[0], b_lru_conv
    ba, bx, lam = b_lru_a, b_lru_x, lru_lambda
    wfc, bfc = w_ffn_conv[0], b_ffn_conv
    sinks = attn_sinks[0]
    smem = pl.BlockSpec(memory_space=pltpu.SMEM)

    mk, mv, mkb, mvb = pl.pallas_call(
        _mem_kv_kernel,
        grid=(B,),
        in_specs=[pl.BlockSpec((1, MEM_LEN, D_MODEL), lambda b: (b, 0, 0)),
                  _const_spec((1, D_MODEL), 1), _const_spec((D_MODEL, D_MEM), 1), _const_spec((D_MODEL, D_MEM), 1)],
        out_specs=[pl.BlockSpec((1, MEM_LEN, D_MEM), lambda b: (b, 0, 0))] * 4,
        out_shape=[jax.ShapeDtypeStruct((B, MEM_LEN, D_MEM), F32)] * 2
        + [jax.ShapeDtypeStruct((B, MEM_LEN, D_MEM), BF16)] * 2,
        compiler_params=_params(("arbitrary",)),
        name="mem_kv",
    )(mem_prompt, gmem, wk, wv)

    x1p, pk, pv, pconv8, ph8 = pl.pallas_call(
        _prompt_mixer_kernel,
        grid=(B, NT),
        in_specs=[smem,
                  pl.BlockSpec((1, TM, D_MODEL), lambda b, t: (b, t, 0)),
                  _const_spec((1, D_MODEL), 2), _const_spec((D_MODEL, D_IN), 2),
                  _const_spec((LRU_CONV_W, D_LRU), 2), _const_spec((1, D_LRU), 2),
                  _const_spec((2, 256, 512), 2), _const_spec((1, D_LRU), 2), _const_spec((1, D_LRU), 2),
                  _const_spec((1, D_LRU), 2), _const_spec((D_MODEL, D_MODEL), 2)],
        out_specs=[pl.BlockSpec((1, TM, D_MODEL), lambda b, t: (b, t, 0)),
                   pl.BlockSpec((1, WINDOW, D_KV), lambda b, t: (b, 0, 0)),
                   pl.BlockSpec((1, WINDOW, D_KV), lambda b, t: (b, 0, 0)),
                   pl.BlockSpec((1, 8, D_LRU), lambda b, t: (b, 0, 0)),
                   pl.BlockSpec((1, 8, D_LRU), lambda b, t: (b, 0, 0))],
        out_shape=[jax.ShapeDtypeStruct((B, T, D_MODEL), F32),
                   jax.ShapeDtypeStruct((B, WINDOW, D_KV), F32),
                   jax.ShapeDtypeStruct((B, WINDOW, D_KV), F32),
                   jax.ShapeDtypeStruct((B, 8, D_LRU), F32),
                   jax.ShapeDtypeStruct((B, 8, D_LRU), F32)],
        scratch_shapes=[pltpu.VMEM((TM, D_IN), F32),
                        pltpu.VMEM((D_LRU // 128, HR, 128), F32),
                        pltpu.VMEM((D_LRU // 128, HR, 128), F32),
                        pltpu.VMEM((24, D_LRU), F32),
                        pltpu.VMEM((TM, D_LRU), F32),
                        pltpu.VMEM((TM, D_LRU), F32),
                        pltpu.VMEM((HR, D_LRU), F32),
                        pltpu.VMEM((TM, D_LRU), F32),
                        pltpu.VMEM((8, D_LRU), F32),
                        pltpu.VMEM((8, TM + WINDOW, D_KV), BF16),
                        pltpu.VMEM((TM, D_MODEL), BF16)],
        compiler_params=_params(("arbitrary", "arbitrary")),
        name="prompt_mixer",
    )(sinks, x_prompt, gmix, win, wconv, bconv, wax, ba, bx, lam, wout)

    xs = x_sample.reshape(SR, D_MODEL)
    conv_prev4 = jnp.pad(state_lru_conv[0], ((0, 0), (0, 1), (0, 0))).reshape(SR, D_LRU)
    h0rep = jnp.repeat(state_lru_h[0], 4, axis=0)
    ck = jnp.transpose(cache_swa_k[0], (0, 2, 3, 1)).reshape(NB, D_KV, WINDOW)
    cv = jnp.transpose(cache_swa_v[0], (0, 2, 3, 1)).reshape(NB, D_KV, WINDOW)
    wkt = jnp.transpose(w_in[0][:, 1536:1664]).astype(BF16)
    wvt = jnp.transpose(w_in[0][:, 1664:1792]).astype(BF16)
    row_spec = lambda w: pl.BlockSpec((S_ROWS, w), lambda i: (i, 0))
    cache_spec = pl.BlockSpec((S_NB, D_KV, WINDOW), lambda i: (i, 0, 0))
    x1s, sk, sv, xr_s, h_s = pl.pallas_call(
        _sample_mixer_kernel,
        grid=(NB // S_NB,),
        in_specs=[smem, row_spec(D_MODEL), row_spec(D_LRU), row_spec(D_LRU), cache_spec, cache_spec,
                  _const_spec((1, D_MODEL), 1), _const_spec((D_MODEL, D_IN), 1),
                  _const_spec((LRU_CONV_W, D_LRU), 1), _const_spec((1, D_LRU), 1),
                  _const_spec((2, 256, 512), 1), _const_spec((1, D_LRU), 1), _const_spec((1, D_LRU), 1),
                  _const_spec((1, D_LRU), 1), _const_spec((D_MODEL, D_MODEL), 1),
                  _const_spec((D_KV, D_MODEL), 1), _const_spec((D_KV, D_MODEL), 1)],
        out_specs=[row_spec(D_MODEL), cache_spec, cache_spec, row_spec(D_LRU), row_spec(D_LRU)],
        out_shape=[jax.ShapeDtypeStruct((SR, D_MODEL), F32),
                   jax.ShapeDtypeStruct((NB, D_KV, WINDOW), F32),
                   jax.ShapeDtypeStruct((NB, D_KV, WINDOW), F32),
                   jax.ShapeDtypeStruct((SR, D_LRU), F32),
                   jax.ShapeDtypeStruct((SR, D_LRU), F32)],
        scratch_shapes=[pltpu.VMEM((S_ROWS, D_ATTN), F32),
                        pltpu.VMEM((S_ROWS, D_ATTN), F32)],
        compiler_params=_params(("arbitrary",)),
        name="sample_mixer",
    )(sinks, xs, conv_prev4, h0rep, ck, cv, gmix, win, wconv, bconv, wax, ba, bx, lam, wout, wkt, wvt)

    cmk = cache_mem_k.reshape(NB, MEM_LEN * N_MEM_HEADS, MEM_HEAD_DIM)
    cmv = cache_mem_v.reshape(NB, MEM_LEN * N_MEM_HEADS, MEM_HEAD_DIM)
    ntf = T // TF
    assert NB == C_NB * B * ntf
    crow = pl.BlockSpec((C_ROWS, D_MODEL), lambda b, t: (b * ntf + t, 0))
    cmem = pl.BlockSpec((C_NB, MEM_LEN * N_MEM_HEADS, MEM_HEAD_DIM), lambda b, t: (b * ntf + t, 0, 0))
    y_prompt, pffn8, x2s = pl.pallas_call(
        _prompt_ffn_kernel,
        grid=(B, ntf),
        in_specs=[pl.BlockSpec((1, TF, D_MODEL), lambda b, t: (b, t, 0)),
                  pl.BlockSpec((1, MEM_LEN, D_MEM), lambda b, t: (b, 0, 0)),
                  pl.BlockSpec((1, MEM_LEN, D_MEM), lambda b, t: (b, 0, 0)),
                  crow, cmem, cmem,
                  _const_spec((1, D_MODEL), 2), _const_spec((D_MODEL, D_MEM), 2), _const_spec((D_MEM, D_MODEL), 2),
                  _const_spec((1, D_MODEL), 2), _const_spec((D_MODEL, D_FF), 2), _const_spec((D_MODEL, D_FF), 2),
                  _const_spec((FFN_CONV_W, D_FF), 2), _const_spec((1, D_FF), 2), _const_spec((D_FF, D_MODEL), 2),
                  _const_spec((1, D_MODEL), 2)],
        out_specs=[pl.BlockSpec((1, TF, D_MODEL), lambda b, t: (b, t, 0)),
                   pl.BlockSpec((1, 8, D_FF), lambda b, t: (b, 0, 0)),
                   crow],
        out_shape=[jax.ShapeDtypeStruct((B, T, D_MODEL), F32),
                   jax.ShapeDtypeStruct((B, 8, D_FF), F32),
                   jax.ShapeDtypeStruct((SR, D_MODEL), F32)],
        scratch_shapes=[pltpu.VMEM((TF, D_MEM), BF16),
                        pltpu.VMEM((TF + 8, FF_CHUNK), F32),
                        pltpu.VMEM((8, D_FF), F32),
                        pltpu.VMEM((C_ROWS, D_MEM), F32),
                        pltpu.VMEM((C_ROWS, D_MEM), F32)],
        compiler_params=_params(("arbitrary", "arbitrary")),
        name="prompt_ffn",
    )(x1p, mkb, mvb, x1s, cmk, cmv, gcross, wq, wo, gffn, wg, wu, wfc, bfc, wd, gfin)

    ffn_prev_tm = jnp.transpose(state_ffn_conv[0], (1, 0, 2))
    slab = pltpu.VMEM((FF_CHUNK // 128, SR, 128), F32)
    y_s, gate_tm = pl.pallas_call(
        _sample_ffn_kernel,
        grid=(1,),
        in_specs=[_const_spec((SR, D_MODEL), 1), _const_spec((FFN_CONV_W - 1, NB, D_FF), 1),
                  _const_spec((1, D_MODEL), 1), _const_spec((D_MODEL, D_FF), 1), _const_spec((D_MODEL, D_FF), 1),
                  _const_spec((FFN_CONV_W, D_FF), 1), _const_spec((1, D_FF), 1), _const_spec((D_FF, D_MODEL), 1),
                  _const_spec((1, D_MODEL), 1)],
        out_specs=[pl.BlockSpec((SR, D_MODEL), lambda i: (0, 0)),
                   pl.BlockSpec((FFN_CONV_W - 1, NB, D_FF), lambda i: (0, 0, 0))],
        out_shape=[jax.ShapeDtypeStruct((SR, D_MODEL), F32),
                   jax.ShapeDtypeStruct((FFN_CONV_W - 1, NB, D_FF), F32)],
        scratch_shapes=[slab, slab, slab],
        compiler_params=_params(("arbitrary",)),
        name="sample_ffn",
    )(x2s, ffn_prev_tm, gffn, wg, wu, wfc, bfc, wd, gfin)

    p_swa_k = pk.reshape(1, B, WINDOW, N_KV_HEADS, HEAD_DIM)
    p_swa_v = pv.reshape(1, B, WINDOW, N_KV_HEADS, HEAD_DIM)
    p_mem_k = mk.reshape(1, B, MEM_LEN, N_MEM_HEADS, MEM_HEAD_DIM)
    p_mem_v = mv.reshape(1, B, MEM_LEN, N_MEM_HEADS, MEM_HEAD_DIM)
    p_lru_conv = pconv8[None, :, 8 - (LRU_CONV_W - 1):, :]
    p_lru_h = ph8[None, :, 0, :]
    p_ffn_conv = pffn8[None, :, 8 - (FFN_CONV_W - 1):, :]
    y_sample = y_s.reshape(NB, 4, D_MODEL)
    s_swa_k = jnp.transpose(sk.reshape(NB, N_KV_HEADS, HEAD_DIM, WINDOW), (0, 3, 1, 2))[None]
    s_swa_v = jnp.transpose(sv.reshape(NB, N_KV_HEADS, HEAD_DIM, WINDOW), (0, 3, 1, 2))[None]
    s_lru_conv = xr_s.reshape(NB, 4, D_LRU)[None, :, 1:, :]
    s_lru_h = h_s.reshape(NB, 4, D_LRU)[None, :, 3, :]
    s_ffn_conv = jnp.transpose(gate_tm, (1, 0, 2))[None]
    return (y_prompt, y_sample, p_swa_k, p_swa_v, p_mem_k, p_mem_v, p_lru_conv, p_lru_h, p_ffn_conv,
            s_swa_k, s_swa_v, s_lru_conv, s_lru_h, s_ffn_conv)
```
